```python
import math
import jax, jax.numpy as jnp
from jax import lax
import numpy as np

D_MODEL = 1024
BATCH = 4
SEQ = 4096
DEPTH = 1

CHUNK = 64
LEFT_CHUNKS = 8
BAND = LEFT_CHUNKS + 1
N_HEADS = 16
HEAD_DIM = 64
D_ATTN = N_HEADS * HEAD_DIM
MAX_REL = 128
D_RNN = D_MODEL
N_RG_BLOCKS = 16
RG_BLOCK = D_RNN // N_RG_BLOCKS
CONV_W = 4
RG_C = 8.0
D_FF = ((8 * D_MODEL // 3 + 255) // 256) * 256
IN_COLS = 3 * D_ATTN + 2 * D_RNN + 2 * D_MODEL
EPS = 1e-6
NEG_INF = -1e30

kernel_name = "hybrid_chunked_attn_rglru_gated_block"


def rmsnorm(x, g):
    xf = x.astype(jnp.float32)
    var = jnp.mean(xf * xf, axis=-1, keepdims=True)
    return (xf * lax.rsqrt(var + EPS) * g.astype(jnp.float32)).astype(x.dtype)


def rel_bias_band(table):
    i = jnp.arange(CHUNK)[:, None]
    m = jnp.arange(BAND * CHUNK)[None, :]
    dist = i - m + LEFT_CHUNKS * CHUNK
    idx = jnp.clip(dist, -MAX_REL, MAX_REL) + MAX_REL
    return table[:, idx]


def chunked_band_attention(q, k, v, rel_table):
    B, S, _ = q.shape
    nc = S // CHUNK
    q = q.reshape(B, nc, CHUNK, N_HEADS, HEAD_DIM)
    k = k.reshape(B, nc, CHUNK, N_HEADS, HEAD_DIM)
    v = v.reshape(B, nc, CHUNK, N_HEADS, HEAD_DIM)
    pad = jnp.zeros((B, LEFT_CHUNKS, CHUNK, N_HEADS, HEAD_DIM), k.dtype)
    kp = jnp.concatenate([pad, k], axis=1)
    vp = jnp.concatenate([pad, v], axis=1)
    kb = jnp.concatenate([kp[:, j:j + nc] for j in range(BAND)], axis=2)
    vb = jnp.concatenate([vp[:, j:j + nc] for j in range(BAND)], axis=2)
    scale = 1.0 / math.sqrt(HEAD_DIM)
    s = jnp.einsum('bnqhd,bnkhd->bhnqk', q, kb).astype(jnp.float32) * scale
    s = s + rel_bias_band(rel_table).astype(jnp.float32)[None, :, None, :, :]
    key_chunk = jnp.arange(nc)[:, None] - LEFT_CHUNKS + (jnp.arange(BAND * CHUNK)[None, :] // CHUNK)
    valid = key_chunk >= 0
    s = jnp.where(valid[None, None, :, None, :], s, NEG_INF)
    p = jax.nn.softmax(s, axis=-1).astype(v.dtype)
    o = jnp.einsum('bhnqk,bnkhd->bnqhd', p, vb)
    return o.reshape(B, S, D_ATTN)


def causal_depthwise_conv(x, w, b):
    S = x.shape[1]
    xp = jnp.pad(x, ((0, 0), (CONV_W - 1, 0), (0, 0)))
    y = sum(xp[:, j:j + S] * w[j] for j in range(CONV_W))
    return y + b


def rg_lru(x, w_r, b_r, w_i, b_i, lam):
    B, S, D = x.shape
    xb = x.reshape(B, S, N_RG_BLOCKS, RG_BLOCK)
    r = jax.nn.sigmoid(jnp.einsum('bsnd,nde->bsne', xb, w_r).reshape(B, S, D) + b_r)
    i = jax.nn.sigmoid(jnp.einsum('bsnd,nde->bsne', xb, w_i).reshape(B, S, D) + b_i)
    log_a = -RG_C * r.astype(jnp.float32) * jax.nn.softplus(-lam.astype(jnp.float32))
    a = jnp.exp(log_a)
    mult = jnp.sqrt(-jnp.expm1(2.0 * log_a))
    bx = mult * (i * x).astype(jnp.float32)

    def combine(left, right):
        a1, b1 = left
        a2, b2 = right
        return a1 * a2, a2 * b1 + b2

    _, h = lax.associative_scan(combine, (a, bx), axis=1)
    return h.astype(x.dtype)


def setup_inputs(seed: int = 0) -> dict:
    key = jax.random.key(seed)
    ks = jax.random.split(key, 20)
    L = DEPTH

    def nrm(k, shape, fan_in):
        return jax.random.normal(k, shape, jnp.float32) * (fan_in ** -0.5)

    x = jax.random.normal(ks[0], (BATCH, SEQ, D_MODEL), jnp.float32)
    norm_mix_g = 1.0 + 0.05 * jax.random.normal(ks[1], (L, D_MODEL), jnp.float32)
    w_in = nrm(ks[2], (L, D_MODEL, IN_COLS), D_MODEL)
    b_merge = 0.01 * jax.random.normal(ks[3], (L, 2 * D_MODEL), jnp.float32)
    rel_table = 0.5 * jax.random.normal(ks[4], (L, N_HEADS, 2 * MAX_REL + 1), jnp.float32)
    w_attn_out = nrm(ks[5], (L, D_ATTN, D_MODEL), D_ATTN)
    conv_w = nrm(ks[6], (L, CONV_W, D_RNN), CONV_W)
    conv_b = 0.01 * jax.random.normal(ks[7], (L, D_RNN), jnp.float32)
    w_rg_r = nrm(ks[8], (L, N_RG_BLOCKS, RG_BLOCK, RG_BLOCK), RG_BLOCK)
    b_rg_r = 0.01 * jax.random.normal(ks[9], (L, D_RNN), jnp.float32)
    w_rg_i = nrm(ks[10], (L, N_RG_BLOCKS, RG_BLOCK, RG_BLOCK), RG_BLOCK)
    b_rg_i = 0.01 * jax.random.normal(ks[11], (L, D_RNN), jnp.float32)
    u = jax.random.uniform(ks[12], (L, D_RNN), jnp.float32, 0.9, 0.999)
    a0 = u ** (1.0 / RG_C)
    rg_lambda = jnp.log(a0) - jnp.log1p(-a0)
    w_rnn_out = nrm(ks[13], (L, D_RNN, D_MODEL), D_RNN)
    w_o = nrm(ks[14], (L, D_MODEL, D_MODEL), D_MODEL)
    norm_ffn_g = 1.0 + 0.05 * jax.random.normal(ks[15], (L, D_MODEL), jnp.float32)
    w_ffn_in = nrm(ks[16], (L, D_MODEL, 2 * D_FF), D_MODEL)
    w_ffn_out = nrm(ks[17], (L, D_FF, D_MODEL), D_FF)
    final_norm_g = 1.0 + 0.05 * jax.random.normal(ks[18], (D_MODEL,), jnp.float32)
    return {"x": x, "norm_mix_g": norm_mix_g, "w_in": w_in, "b_merge": b_merge,
            "rel_table": rel_table, "w_attn_out": w_attn_out, "conv_w": conv_w,
            "conv_b": conv_b, "w_rg_r": w_rg_r, "b_rg_r": b_rg_r, "w_rg_i": w_rg_i,
            "b_rg_i": b_rg_i, "rg_lambda": rg_lambda, "w_rnn_out": w_rnn_out, "w_o": w_o,
            "norm_ffn_g": norm_ffn_g, "w_ffn_in": w_ffn_in, "w_ffn_out": w_ffn_out,
            "final_norm_g": final_norm_g}


def reference(x, norm_mix_g, w_in, b_merge, rel_table, w_attn_out, conv_w, conv_b,
              w_rg_r, b_rg_r, w_rg_i, b_rg_i, rg_lambda, w_rnn_out, w_o,
              norm_ffn_g, w_ffn_in, w_ffn_out, final_norm_g):
    h = x
    splits = [D_ATTN, 2 * D_ATTN, 3 * D_ATTN, 3 * D_ATTN + D_RNN,
              3 * D_ATTN + 2 * D_RNN, 3 * D_ATTN + 2 * D_RNN + D_MODEL]
    for l in range(DEPTH):
        xn = rmsnorm(h, norm_mix_g[l])
        u = jnp.einsum('bsd,de->bse', xn, w_in[l])
        q, k, v, xr, gr, ga, gb = jnp.split(u, splits, axis=-1)
        ya = chunked_band_attention(q, k, v, rel_table[l]) @ w_attn_out[l]
        xc = causal_depthwise_conv(xr, conv_w[l], conv_b[l])
        hr = rg_lru(xc, w_rg_r[l], b_rg_r[l], w_rg_i[l], b_rg_i[l], rg_lambda[l])
        yb = (hr * jax.nn.gelu(gr)) @ w_rnn_out[l]
        gates = jax.nn.sigmoid(jnp.concatenate([ga, gb], axis=-1) + b_merge[l])
        g_a, g_b = jnp.split(gates, 2, axis=-1)
        h = h + (g_a * ya + g_b * yb) @ w_o[l]
        hn = rmsnorm(h, norm_ffn_g[l])
        gu = hn @ w_ffn_in[l]
        g, up = jnp.split(gu, 2, axis=-1)
        h = h + (jax.nn.silu(g) * up) @ w_ffn_out[l]
    return rmsnorm(h, final_norm_g)
```

```python
import functools
import math

import jax
import jax.numpy as jnp
from jax import lax
from jax.experimental import pallas as pl
from jax.experimental.pallas import tpu as pltpu

F32 = jnp.float32
BF16 = jnp.bfloat16

D_MODEL = 1024
CHUNK = 64
LEFT_CHUNKS = 8
N_HEADS = 16
HEAD_DIM = 64
MAX_REL = 128
N_RG_BLOCKS = 16
RG_BLOCK = D_MODEL // N_RG_BLOCKS
CONV_W = 4
RG_C = 8.0
EPS = 1e-6
NEG_INF = -1e30

LANES = 128
SUBLANES = 8
MXU_DIM = 256
HEAD_PAIR = 2 * HEAD_DIM
N_PAIRS = N_HEADS // 2
GROUP = 4 * CHUNK
BAND_TILES = LEFT_CHUNKS * CHUNK // GROUP + 1
BAND = BAND_TILES * GROUP
VMEM_LIMIT = 56 * 1024 * 1024


def _rms(x, g):
    var = jnp.mean(x * x, axis=-1, keepdims=True)
    return x * lax.rsqrt(var + EPS) * g


def _resident(shape):
    zeros = (0,) * len(shape)
    return pl.BlockSpec(shape, lambda *_: zeros, pipeline_mode=pl.Buffered(1))


def _inproj_kernel(x_ref, g_ref, w_ref, qkv_ref, rest_ref):
    xn = _rms(x_ref[...], g_ref[...]).astype(BF16)
    for j in range(7):
        acc = jnp.dot(xn, w_ref[:, j * D_MODEL:(j + 1) * D_MODEL],
                      preferred_element_type=F32)
        if j < 3:
            for hp in range(N_PAIRS):
                qkv_ref[j * N_PAIRS + hp] = acc[:, hp * HEAD_PAIR:(hp + 1) * HEAD_PAIR].astype(BF16)
        else:
            rest_ref[:, (j - 3) * D_MODEL:(j - 2) * D_MODEL] = acc.astype(BF16)


def _inproj(x2, g, w, tm):
    t = x2.shape[0]
    return pl.pallas_call(
        _inproj_kernel,
        grid=(t // tm,),
        in_specs=[
            pl.BlockSpec((tm, D_MODEL), lambda i: (i, 0)),
            _resident((1, D_MODEL)),
            _resident((D_MODEL, 7 * D_MODEL)),
        ],
        out_specs=[
            pl.BlockSpec((3 * N_PAIRS, tm, HEAD_PAIR), lambda i: (0, i, 0)),
            pl.BlockSpec((tm, 4 * D_MODEL), lambda i: (i, 0)),
        ],
        out_shape=[
            jax.ShapeDtypeStruct((3 * N_PAIRS, t, HEAD_PAIR), BF16),
            jax.ShapeDtypeStruct((t, 4 * D_MODEL), BF16),
        ],
        compiler_params=pltpu.CompilerParams(
            dimension_semantics=("parallel",), vmem_limit_bytes=VMEM_LIMIT),
        name="inproj",
    )(x2, g, w)


def _attn_group(q_ref, k_ref, v_ref, bias_ref, o_ref, s_ref, p_ref, l_ref, r0, ntiles):
    kw = ntiles * GROUP
    c_off = BAND - kw
    k0 = r0 + GROUP - kw
    lane = lax.broadcasted_iota(jnp.int32, (GROUP, HEAD_PAIR), 1)
    first_head = lane < HEAD_DIM

    qp = q_ref[0, pl.ds(r0, GROUP), :] * jnp.asarray(1.0 / math.sqrt(HEAD_DIM), BF16)
    zero = jnp.zeros_like(qp)
    qq = jnp.concatenate([jnp.where(first_head, qp, zero),
                          jnp.where(first_head, zero, qp)], axis=0)
    kk = k_ref[0, pl.ds(k0, kw), :]
    s_ref[:, c_off:] = lax.dot_general(qq, kk, (((1,), (1,)), ((), ())),
                                       preferred_element_type=F32)

    rb = 32
    lane_rb = lax.broadcasted_iota(jnp.int32, (rb, LANES), 1)
    for hh in range(2):
        for i in range(GROUP // CHUNK):
            lo_col = max(i * CHUNK, c_off)
            hi_col = i * CHUNK + (LEFT_CHUNKS + 1) * CHUNK
            ws = (lo_col // LANES) * LANES
            we = -(-hi_col // LANES) * LANES
            nv = (we - ws) // LANES
            rows_all = slice(hh * GROUP + i * CHUNK, hh * GROUP + (i + 1) * CHUNK)
            if ws > c_off:
                p_ref[rows_all, c_off:ws] = jnp.zeros((CHUNK, ws - c_off), BF16)
            if we < BAND:
                p_ref[rows_all, we:BAND] = jnp.zeros((CHUNK, BAND - we), BF16)
            for b in range(CHUNK // rb):
                rows = slice(hh * GROUP + i * CHUNK + b * rb, hh * GROUP + i * CHUNK + (b + 1) * rb)
                brows = slice(i * CHUNK + b * rb, i * CHUNK + (b + 1) * rb)
                x = s_ref[rows, ws:we] + bias_ref[hh, brows, ws:we]
                cols = [x[:, c * LANES:(c + 1) * LANES] for c in range(nv)]
                if lo_col % LANES:
                    cols[0] = jnp.where(lane_rb >= HEAD_DIM, cols[0], NEG_INF)
                if hi_col % LANES:
                    cols[-1] = jnp.where(lane_rb < HEAD_DIM, cols[-1], NEG_INF)
                x = jnp.concatenate(cols, axis=1)
                m = jnp.max(x, axis=-1, keepdims=True)
                e = jnp.exp(x - m)
                l = jnp.sum(e, axis=-1, keepdims=True)
                p_ref[rows, ws:we] = e.astype(BF16)
                l_ref[rows, :] = jnp.broadcast_to(l, (rb, LANES))

    vv = v_ref[0, pl.ds(k0, kw), :]
    o2 = jnp.dot(p_ref[:, c_off:], vv, preferred_element_type=F32)
    o2 = o2 / l_ref[...]
    o = jnp.where(first_head, o2[:GROUP], o2[GROUP:])
    o_ref[pl.ds(r0, GROUP), :] = o.astype(BF16)


def _attn_kernel(q_ref, k_ref, v_ref, bias_ref, o_ref, s_ref, p_ref, l_ref):
    n_groups = q_ref.shape[1] // GROUP
    args = (q_ref, k_ref, v_ref, bias_ref, o_ref, s_ref, p_ref, l_ref)
    for g in range(BAND_TILES - 1):
        _attn_group(*args, g * GROUP, g + 1)

    def body(g, carry):
        _attn_group(*args, pl.multiple_of(g * GROUP, GROUP), BAND_TILES)
        return carry

    lax.fori_loop(BAND_TILES - 1, n_groups, body, 0)


def _attention(qkv, bias, batch, seq):
    t = batch * seq
    return pl.pallas_call(
        _attn_kernel,
        grid=(N_PAIRS, batch),
        in_specs=[
            pl.BlockSpec((1, seq, HEAD_PAIR), lambda hp, b: (hp, b, 0)),
            pl.BlockSpec((1, seq, HEAD_PAIR), lambda hp, b: (N_PAIRS + hp, b, 0)),
            pl.BlockSpec((1, seq, HEAD_PAIR), lambda hp, b: (2 * N_PAIRS + hp, b, 0)),
            pl.BlockSpec((2, GROUP, BAND), lambda hp, b: (hp, 0, 0)),
        ],
        out_specs=pl.BlockSpec((seq, HEAD_PAIR), lambda hp, b: (b, hp)),
        out_shape=jax.ShapeDtypeStruct((t, D_MODEL), BF16),
        scratch_shapes=[
            pltpu.VMEM((2 * GROUP, BAND), F32),
            pltpu.VMEM((2 * GROUP, BAND), BF16),
            pltpu.VMEM((2 * GROUP, LANES), F32),
        ],
        compiler_params=pltpu.CompilerParams(
            dimension_semantics=("parallel", "parallel"), vmem_limit_bytes=VMEM_LIMIT),
        name="band_attention",
    )(qkv, qkv, qkv, bias)


def _band_bias(rel_table):
    r = jnp.arange(GROUP)[:, None]
    m = jnp.arange(BAND)[None, :]
    dist = r - m + (BAND - GROUP)
    idx = jnp.clip(dist, -MAX_REL, MAX_REL) + MAX_REL
    return rel_table[:, idx]


def _rglru_kernel(xr_ref, gr_ref, cw_ref, cb_ref, wr_ref, br_ref, wi_ref, bi_ref, lam_ref,
                  o_ref, xs_ref, a_ref, b_ref, h_ref):
    tm = xr_ref.shape[0]
    seg = tm // SUBLANES

    @pl.when(pl.program_id(1) == 0)
    def _():
        xs_ref[0:SUBLANES, :] = jnp.zeros((SUBLANES, D_MODEL), F32)
        h_ref[...] = jnp.zeros_like(h_ref)

    x = xr_ref[...].astype(F32)
    xs_ref[SUBLANES:, :] = x
    xc = x * cw_ref[CONV_W - 1:CONV_W, :] + cb_ref[...]
    for j in range(CONV_W - 1):
        shift = CONV_W - 1 - j
        xc = xc + xs_ref[SUBLANES - shift:SUBLANES - shift + tm, :] * cw_ref[j:j + 1, :]
    xs_ref[0:SUBLANES, :] = x[tm - SUBLANES:, :]

    xcb = xc.astype(BF16)
    n_diag = D_MODEL // MXU_DIM
    r_pre = jnp.concatenate(
        [jnp.dot(xcb[:, d * MXU_DIM:(d + 1) * MXU_DIM], wr_ref[d], preferred_element_type=F32)
         for d in range(n_diag)], axis=1)
    i_pre = jnp.concatenate(
        [jnp.dot(xcb[:, d * MXU_DIM:(d + 1) * MXU_DIM], wi_ref[d], preferred_element_type=F32)
         for d in range(n_diag)], axis=1)
    r = jax.nn.sigmoid(r_pre + br_ref[...])
    ig = jax.nn.sigmoid(i_pre + bi_ref[...])
    nlam = -lam_ref[...]
    softplus = jnp.maximum(nlam, 0.0) + jnp.log1p(jnp.exp(-jnp.abs(nlam)))
    log_a = (-RG_C) * r * softplus
    a = jnp.exp(log_a)
    mult = jnp.sqrt(-jnp.tanh(log_a) * (a * a + 1.0))
    bx = mult * (ig * xc)
    n_lane_tiles = D_MODEL // LANES
    for c in range(n_lane_tiles):
        a_ref[c] = a[:, c * LANES:(c + 1) * LANES]
        b_ref[c] = bx[:, c * LANES:(c + 1) * LANES]

    def step(t, carry):
        hs, ps = carry
        idx = pl.ds(t, SUBLANES, stride=seg)
        new_h, new_p = [], []
        for c in range(n_lane_tiles):
            a_t = a_ref[c, idx, :]
            h = a_t * hs[c] + b_ref[c, idx, :]
            p = a_t * ps[c]
            b_ref[c, idx, :] = h
            a_ref[c, idx, :] = p
            new_h.append(h)
            new_p.append(p)
        return tuple(new_h), tuple(new_p)

    zeros = tuple(jnp.zeros((SUBLANES, LANES), F32) for _ in range(n_lane_tiles))
    ones = tuple(jnp.ones((SUBLANES, LANES), F32) for _ in range(n_lane_tiles))
    h_end, p_end = lax.fori_loop(0, seg, step, (zeros, ones))
    h_end = jnp.concatenate(h_end, axis=1)
    p_end = jnp.concatenate(p_end, axis=1)

    h_in = h_ref[0:1, :]
    for j in range(SUBLANES):
        rows = slice(j * seg, (j + 1) * seg)
        local = jnp.concatenate([b_ref[c, rows, :] for c in range(n_lane_tiles)], axis=1)
        decay = jnp.concatenate([a_ref[c, rows, :] for c in range(n_lane_tiles)], axis=1)
        hj = local + decay * h_in
        o_ref[rows, :] = (hj * jax.nn.gelu(gr_ref[rows, :].astype(F32))).astype(BF16)
        h_in = p_end[j:j + 1, :] * h_in + h_end[j:j + 1, :]
    h_ref[0:1, :] = h_in


def _rglru(rest, cw, cb, wr, br, wi, bi, lam, batch, seq, tm):
    t = batch * seq
    nt = seq // tm
    n_diag = D_MODEL // MXU_DIM
    return pl.pallas_call(
        _rglru_kernel,
        grid=(batch, nt),
        in_specs=[
            pl.BlockSpec((tm, D_MODEL), lambda b, i: (b * nt + i, 0)),
            pl.BlockSpec((tm, D_MODEL), lambda b, i: (b * nt + i, 1)),
            _resident((CONV_W, D_MODEL)),
            _resident((1, D_MODEL)),
            _resident((n_diag, MXU_DIM, MXU_DIM)),
            _resident((1, D_MODEL)),
            _resident((n_diag, MXU_DIM, MXU_DIM)),
            _resident((1, D_MODEL)),
            _resident((1, D_MODEL)),
        ],
        out_specs=pl.BlockSpec((tm, D_MODEL), lambda b, i: (b * nt + i, 0)),
        out_shape=jax.ShapeDtypeStruct((t, D_MODEL), BF16),
        scratch_shapes=[
            pltpu.VMEM((tm + SUBLANES, D_MODEL), F32),
            pltpu.VMEM((D_MODEL // LANES, tm, LANES), F32),
            pltpu.VMEM((D_MODEL // LANES, tm, LANES), F32),
            pltpu.VMEM((SUBLANES, D_MODEL), F32),
        ],
        compiler_params=pltpu.CompilerParams(
            dimension_semantics=("parallel", "arbitrary"), vmem_limit_bytes=VMEM_LIMIT),
        name="rglru",
    )(rest, rest, cw, cb, wr, br, wi, bi, lam)


def _block_diag_tiles(w):
    per = MXU_DIM // RG_BLOCK
    n_diag = D_MODEL // MXU_DIM
    w4 = w.reshape(n_diag, per, RG_BLOCK, RG_BLOCK)
    eye = jnp.eye(per, dtype=w.dtype)
    tiles = jnp.einsum('dpij,pq->dpiqj', w4, eye)
    return tiles.reshape(n_diag, MXU_DIM, MXU_DIM)


def _merge_kernel(x_ref, at_ref, hb_ref, ga_ref, gb_ref, bm_ref, wa_ref, wb_ref, wo_ref, h_ref):
    ya = jnp.dot(at_ref[...], wa_ref[...], preferred_element_type=F32)
    yb = jnp.dot(hb_ref[...], wb_ref[...], preferred_element_type=F32)
    g_a = jax.nn.sigmoid(ga_ref[...].astype(F32) + bm_ref[:, :D_MODEL])
    g_b = jax.nn.sigmoid(gb_ref[...].astype(F32) + bm_ref[:, D_MODEL:])
    mix = (g_a * ya + g_b * yb).astype(BF16)
    h_ref[...] = x_ref[...] + jnp.dot(mix, wo_ref[...], preferred_element_type=F32)


def _merge(x2, attn, hb, rest, bm, wa, wb, wo, tm):
    t = x2.shape[0]
    row = lambda i: (i, 0)
    return pl.pallas_call(
        _merge_kernel,
        grid=(t // tm,),
        in_specs=[
            pl.BlockSpec((tm, D_MODEL), row),
            pl.BlockSpec((tm, D_MODEL), row),
            pl.BlockSpec((tm, D_MODEL), row),
            pl.BlockSpec((tm, D_MODEL), lambda i: (i, 2)),
            pl.BlockSpec((tm, D_MODEL), lambda i: (i, 3)),
            _resident((1, 2 * D_MODEL)),
            _resident((D_MODEL, D_MODEL)),
            _resident((D_MODEL, D_MODEL)),
            _resident((D_MODEL, D_MODEL)),
        ],
        out_specs=pl.BlockSpec((tm, D_MODEL), row),
        out_shape=jax.ShapeDtypeStruct((t, D_MODEL), F32),
        compiler_params=pltpu.CompilerParams(
            dimension_semantics=("parallel",), vmem_limit_bytes=VMEM_LIMIT),
        name="merge",
    )(x2, attn, hb, rest, rest, bm, wa, wb, wo)


def _ffn_kernel(h_ref, gn_ref, win_ref, wout_ref, gf_ref, o_ref, *, splits):
    h = h_ref[...]
    hn = _rms(h, gn_ref[...]).astype(BF16)
    d_ff = wout_ref.shape[0]
    acc = h
    for c0, c1 in splits:
        g = jnp.dot(hn, win_ref[:, c0:c1], preferred_element_type=F32)
        up = jnp.dot(hn, win_ref[:, d_ff + c0:d_ff + c1], preferred_element_type=F32)
        act = (jax.nn.silu(g) * up).astype(BF16)
        acc = acc + jnp.dot(act, wout_ref[c0:c1, :], preferred_element_type=F32)
    o_ref[...] = _rms(acc, gf_ref[...])


def _ffn(h, gn, win, wout, gf, tm):
    t = h.shape[0]
    d_ff = wout.shape[0]
    n_tiles = d_ff // MXU_DIM
    half = (n_tiles + 1) // 2 * MXU_DIM
    splits = ((0, half), (half, d_ff))
    row = lambda i: (i, 0)
    return pl.pallas_call(
        functools.partial(_ffn_kernel, splits=splits),
        grid=(t // tm,),
        in_specs=[
            pl.BlockSpec((tm, D_MODEL), row),
            _resident((1, D_MODEL)),
            _resident((D_MODEL, 2 * d_ff)),
            _resident((d_ff, D_MODEL)),
            _resident((1, D_MODEL)),
        ],
        out_specs=pl.BlockSpec((tm, D_MODEL), row),
        out_shape=jax.ShapeDtypeStruct((t, D_MODEL), F32),
        compiler_params=pltpu.CompilerParams(
            dimension_semantics=("parallel",), vmem_limit_bytes=VMEM_LIMIT),
        name="ffn",
    )(h, gn, win, wout, gf)


def kernel(x, norm_mix_g, w_in, b_merge, rel_table, w_attn_out, conv_w, conv_b, w_rg_r, b_rg_r,
           w_rg_i, b_rg_i, rg_lambda, w_rnn_out, w_o, norm_ffn_g, w_ffn_in, w_ffn_out, final_norm_g):
    batch, seq, d = x.shape
    assert d == D_MODEL and seq % GROUP == 0 and seq // GROUP >= BAND_TILES
    assert w_in.shape[0] == 1, "single-layer block"
    x2 = x.reshape(batch * seq, d)
    row = lambda v: v.reshape(1, -1)

    qkv, rest = _inproj(x2, row(norm_mix_g[0]), w_in[0].astype(BF16), tm=512)
    attn = _attention(qkv, _band_bias(rel_table[0]), batch, seq)
    hb = _rglru(rest, conv_w[0], row(conv_b[0]),
                _block_diag_tiles(w_rg_r[0]).astype(BF16), row(b_rg_r[0]),
                _block_diag_tiles(w_rg_i[0]).astype(BF16), row(b_rg_i[0]),
                row(rg_lambda[0]), batch, seq, tm=512)
    h = _merge(x2, attn, hb, rest, row(b_merge[0]), w_attn_out[0].astype(BF16),
               w_rnn_out[0].astype(BF16), w_o[0].astype(BF16), tm=512)
    out = _ffn(h, row(norm_ffn_g[0]), w_ffn_in[0].astype(BF16), w_ffn_out[0].astype(BF16),
               row(final_norm_g), tm=512)
    return out.reshape(batch, seq, d)
```

```python
import functools
import math

import jax
import jax.numpy as jnp
from jax import lax
from jax.experimental import pallas as pl
from jax.experimental.pallas import tpu as pltpu

F32 = jnp.float32
BF16 = jnp.bfloat16

D_MODEL = 1024
CHUNK = 64
LEFT_CHUNKS = 8
N_HEADS = 16
HEAD_DIM = 64
MAX_REL = 128
N_RG_BLOCKS = 16
RG_BLOCK = D_MODEL // N_RG_BLOCKS
CONV_W = 4
RG_C = 8.0
EPS = 1e-6
NEG_INF = -1e30

LANES = 128
SUBLANES = 8
MXU_DIM = 256
HEAD_PAIR = 2 * HEAD_DIM
N_PAIRS = N_HEADS // 2
GROUP = 4 * CHUNK
BAND_TILES = LEFT_CHUNKS * CHUNK // GROUP + 1
BAND = BAND_TILES * GROUP
VMEM_LIMIT = 56 * 1024 * 1024
RG_TILE = 512
SEG_PITCH = RG_TILE // SUBLANES + SUBLANES // 2


def _rms(x, g):
    var = jnp.mean(x * x, axis=-1, keepdims=True)
    return x * lax.rsqrt(var + EPS) * g


def _resident(shape):
    zeros = (0,) * len(shape)
    return pl.BlockSpec(shape, lambda *_: zeros, pipeline_mode=pl.Buffered(1))


def _inproj_kernel(x_ref, g_ref, w_ref, qkv_ref, rest_ref):
    xn = _rms(x_ref[...], g_ref[...]).astype(BF16)
    for j in range(7):
        acc = jnp.dot(xn, w_ref[:, j * D_MODEL:(j + 1) * D_MODEL],
                      preferred_element_type=F32)
        if j < 3:
            for hp in range(N_PAIRS):
                qkv_ref[j * N_PAIRS + hp] = acc[:, hp * HEAD_PAIR:(hp + 1) * HEAD_PAIR].astype(BF16)
        else:
            rest_ref[:, (j - 3) * D_MODEL:(j - 2) * D_MODEL] = acc.astype(BF16)


def _inproj(x2, g, w, tm):
    t = x2.shape[0]
    return pl.pallas_call(
        _inproj_kernel,
        grid=(t // tm,),
        in_specs=[
            pl.BlockSpec((tm, D_MODEL), lambda i: (i, 0)),
            _resident((1, D_MODEL)),
            _resident((D_MODEL, 7 * D_MODEL)),
        ],
        out_specs=[
            pl.BlockSpec((3 * N_PAIRS, tm, HEAD_PAIR), lambda i: (0, i, 0)),
            pl.BlockSpec((tm, 4 * D_MODEL), lambda i: (i, 0)),
        ],
        out_shape=[
            jax.ShapeDtypeStruct((3 * N_PAIRS, t, HEAD_PAIR), BF16),
            jax.ShapeDtypeStruct((t, 4 * D_MODEL), BF16),
        ],
        compiler_params=pltpu.CompilerParams(
            dimension_semantics=("parallel",), vmem_limit_bytes=VMEM_LIMIT),
        name="inproj",
    )(x2, g, w)


def _attn_group(q_ref, k_ref, v_ref, bias_ref, o_ref, s_ref, p_ref, l_ref, r0, ntiles):
    kw = ntiles * GROUP
    c_off = BAND - kw
    k0 = r0 + GROUP - kw
    lane = lax.broadcasted_iota(jnp.int32, (GROUP, HEAD_PAIR), 1)
    first_head = lane < HEAD_DIM

    qp = q_ref[0, pl.ds(r0, GROUP), :] * jnp.asarray(1.0 / math.sqrt(HEAD_DIM), BF16)
    zero = jnp.zeros_like(qp)
    qq = jnp.concatenate([jnp.where(first_head, qp, zero),
                          jnp.where(first_head, zero, qp)], axis=0)
    kk = k_ref[0, pl.ds(k0, kw), :]
    s_ref[:, c_off:] = lax.dot_general(qq, kk, (((1,), (1,)), ((), ())),
                                       preferred_element_type=F32)

    rb = 32
    lane_rb = lax.broadcasted_iota(jnp.int32, (rb, LANES), 1)
    for hh in range(2):
        for i in range(GROUP // CHUNK):
            lo_col = max(i * CHUNK, c_off)
            hi_col = i * CHUNK + (LEFT_CHUNKS + 1) * CHUNK
            ws = (lo_col // LANES) * LANES
            we = -(-hi_col // LANES) * LANES
            nv = (we - ws) // LANES
            rows_all = slice(hh * GROUP + i * CHUNK, hh * GROUP + (i + 1) * CHUNK)
            if ws > c_off:
                p_ref[rows_all, c_off:ws] = jnp.zeros((CHUNK, ws - c_off), BF16)
            if we < BAND:
                p_ref[rows_all, we:BAND] = jnp.zeros((CHUNK, BAND - we), BF16)
            for b in range(CHUNK // rb):
                rows = slice(hh * GROUP + i * CHUNK + b * rb, hh * GROUP + i * CHUNK + (b + 1) * rb)
                brows = slice(i * CHUNK + b * rb, i * CHUNK + (b + 1) * rb)
                x = s_ref[rows, ws:we] + bias_ref[hh, brows, ws:we]
                cols = [x[:, c * LANES:(c + 1) * LANES] for c in range(nv)]
                if lo_col % LANES:
                    cols[0] = jnp.where(lane_rb >= HEAD_DIM, cols[0], NEG_INF)
                if hi_col % LANES:
                    cols[-1] = jnp.where(lane_rb < HEAD_DIM, cols[-1], NEG_INF)
                x = jnp.concatenate(cols, axis=1)
                m = jnp.max(x, axis=-1, keepdims=True)
                e = jnp.exp(x - m)
                l = jnp.sum(e, axis=-1, keepdims=True)
                p_ref[rows, ws:we] = e.astype(BF16)
                l_ref[rows, :] = jnp.broadcast_to(l, (rb, LANES))

    vv = v_ref[0, pl.ds(k0, kw), :]
    o2 = jnp.dot(p_ref[:, c_off:], vv, preferred_element_type=F32)
    o2 = o2 / l_ref[...]
    o = jnp.where(first_head, o2[:GROUP], o2[GROUP:])
    o_ref[pl.ds(r0, GROUP), :] = o.astype(BF16)


def _attn_kernel(q_ref, k_ref, v_ref, bias_ref, o_ref, s_ref, p_ref, l_ref):
    n_groups = q_ref.shape[1] // GROUP
    args = (q_ref, k_ref, v_ref, bias_ref, o_ref, s_ref, p_ref, l_ref)
    for g in range(BAND_TILES - 1):
        _attn_group(*args, g * GROUP, g + 1)

    def body(g, carry):
        _attn_group(*args, pl.multiple_of(g * GROUP, GROUP), BAND_TILES)
        return carry

    lax.fori_loop(BAND_TILES - 1, n_groups, body, 0)


def _attention(qkv, bias, batch, seq):
    t = batch * seq
    return pl.pallas_call(
        _attn_kernel,
        grid=(N_PAIRS, batch),
        in_specs=[
            pl.BlockSpec((1, seq, HEAD_PAIR), lambda hp, b: (hp, b, 0)),
            pl.BlockSpec((1, seq, HEAD_PAIR), lambda hp, b: (N_PAIRS + hp, b, 0)),
            pl.BlockSpec((1, seq, HEAD_PAIR), lambda hp, b: (2 * N_PAIRS + hp, b, 0)),
            pl.BlockSpec((2, GROUP, BAND), lambda hp, b: (hp, 0, 0)),
        ],
        out_specs=pl.BlockSpec((seq, HEAD_PAIR), lambda hp, b: (b, hp)),
        out_shape=jax.ShapeDtypeStruct((t, D_MODEL), BF16),
        scratch_shapes=[
            pltpu.VMEM((2 * GROUP, BAND), F32),
            pltpu.VMEM((2 * GROUP, BAND), BF16),
            pltpu.VMEM((2 * GROUP, LANES), F32),
        ],
        compiler_params=pltpu.CompilerParams(
            dimension_semantics=("parallel", "parallel"), vmem_limit_bytes=VMEM_LIMIT),
        name="band_attention",
    )(qkv, qkv, qkv, bias)


def _band_bias(rel_table):
    n_diag = GROUP + BAND - 1
    j = jnp.arange(n_diag)
    dist = (GROUP - 1 - j) + (BAND - GROUP)
    w = rel_table[:, jnp.clip(dist, -MAX_REL, MAX_REL) + MAX_REL]
    flat = jnp.tile(jnp.pad(w, ((0, 0), (0, 1))), (1, GROUP))[:, :GROUP * n_diag]
    return flat.reshape(-1, GROUP, n_diag)[:, :, GROUP - 1:GROUP - 1 + BAND]


def _gelu_tanh(x):
    k = math.sqrt(2.0 / math.pi)
    inner = x * (k + (k * 0.044715) * (x * x))
    return x * (0.5 + 0.5 * jnp.tanh(inner))


def _rglru_kernel(xr_ref, gr_ref, cw_ref, cb_ref, wr_ref, br_ref, wi_ref, bi_ref, lam_ref,
                  o_ref, xs_ref, a_ref, b_ref, hl_ref, pp_ref, h_ref):
    tm = xr_ref.shape[0]
    seg = tm // SUBLANES
    n_lane_tiles = D_MODEL // LANES
    lane_tile = lambda c: slice(c * LANES, (c + 1) * LANES)

    @pl.when(pl.program_id(1) == 0)
    def _():
        xs_ref[:, 0:SUBLANES, :] = jnp.zeros((n_lane_tiles, SUBLANES, LANES), F32)
        h_ref[...] = jnp.zeros_like(h_ref)

    pieces = []
    for c in range(n_lane_tiles):
        xs_ref[c, SUBLANES:, :] = xr_ref[:, lane_tile(c)].astype(F32)
        acc = cb_ref[:, lane_tile(c)]
        for j in range(CONV_W):
            off = SUBLANES - (CONV_W - 1) + j
            acc = acc + xs_ref[c, off:off + tm, :] * cw_ref[j:j + 1, lane_tile(c)]
        pieces.append(acc)
        xs_ref[c, 0:SUBLANES, :] = xs_ref[c, tm:tm + SUBLANES, :]
    xc = jnp.concatenate(pieces, axis=1)

    xcb = xc.astype(BF16)
    n_diag = D_MODEL // MXU_DIM
    r_pre = jnp.concatenate(
        [jnp.dot(xcb[:, d * MXU_DIM:(d + 1) * MXU_DIM], wr_ref[d], preferred_element_type=F32)
         for d in range(n_diag)], axis=1)
    i_pre = jnp.concatenate(
        [jnp.dot(xcb[:, d * MXU_DIM:(d + 1) * MXU_DIM], wi_ref[d], preferred_element_type=F32)
         for d in range(n_diag)], axis=1)
    r = jax.nn.sigmoid(r_pre + br_ref[...])
    ig = jax.nn.sigmoid(i_pre + bi_ref[...])
    nlam = -lam_ref[...]
    softplus = jnp.maximum(nlam, 0.0) + jnp.log1p(jnp.exp(-jnp.abs(nlam)))
    log_a = r * ((-RG_C) * softplus)
    a = jnp.exp(log_a)
    y = jnp.tanh(log_a) * (-1.0 - a * a)
    mult = jnp.where(y > 0.0, y * lax.rsqrt(y), 0.0)
    bx = mult * (ig * xc)

    for c in range(n_lane_tiles):
        for j in range(SUBLANES):
            a_ref[c, j * SEG_PITCH:j * SEG_PITCH + seg, :] = a[j * seg:(j + 1) * seg, lane_tile(c)]
            b_ref[c, j * SEG_PITCH:j * SEG_PITCH + seg, :] = bx[j * seg:(j + 1) * seg, lane_tile(c)]

    def step(t, carry):
        hs, ps = carry
        idx = pl.ds(t, SUBLANES, stride=SEG_PITCH)
        new_h, new_p = [], []
        for c in range(n_lane_tiles):
            a_t = a_ref[c, idx, :]
            h = a_t * hs[c] + b_ref[c, idx, :]
            p = a_t * ps[c]
            hl_ref[c, idx, :] = h
            pp_ref[c, idx, :] = p
            new_h.append(h)
            new_p.append(p)
        return tuple(new_h), tuple(new_p)

    zeros = tuple(jnp.zeros((SUBLANES, LANES), F32) for _ in range(n_lane_tiles))
    ones = tuple(jnp.ones((SUBLANES, LANES), F32) for _ in range(n_lane_tiles))
    h_end, p_end = lax.fori_loop(0, seg, step, (zeros, ones))
    h_end = jnp.concatenate(h_end, axis=1)
    p_end = jnp.concatenate(p_end, axis=1)

    h_in = h_ref[0:1, :]
    for j in range(SUBLANES):
        rows = slice(j * seg, (j + 1) * seg)
        srows = slice(j * SEG_PITCH, j * SEG_PITCH + seg)
        local = jnp.concatenate([hl_ref[c, srows, :] for c in range(n_lane_tiles)], axis=1)
        decay = jnp.concatenate([pp_ref[c, srows, :] for c in range(n_lane_tiles)], axis=1)
        hj = local + decay * h_in
        o_ref[rows, :] = (hj * _gelu_tanh(gr_ref[rows, :].astype(F32))).astype(BF16)
        h_in = p_end[j:j + 1, :] * h_in + h_end[j:j + 1, :]
    h_ref[0:1, :] = h_in


def _rglru(rest, cw, cb, wr, br, wi, bi, lam, batch, seq, tm):
    t = batch * seq
    nt = seq // tm
    n_diag = D_MODEL // MXU_DIM
    return pl.pallas_call(
        _rglru_kernel,
        grid=(batch, nt),
        in_specs=[
            pl.BlockSpec((tm, D_MODEL), lambda b, i: (b * nt + i, 0)),
            pl.BlockSpec((tm, D_MODEL), lambda b, i: (b * nt + i, 1)),
            _resident((CONV_W, D_MODEL)),
            _resident((1, D_MODEL)),
            _resident((n_diag, MXU_DIM, MXU_DIM)),
            _resident((1, D_MODEL)),
            _resident((n_diag, MXU_DIM, MXU_DIM)),
            _resident((1, D_MODEL)),
            _resident((1, D_MODEL)),
        ],
        out_specs=pl.BlockSpec((tm, D_MODEL), lambda b, i: (b * nt + i, 0)),
        out_shape=jax.ShapeDtypeStruct((t, D_MODEL), BF16),
        scratch_shapes=[
            pltpu.VMEM((D_MODEL // LANES, tm + SUBLANES, LANES), F32),
            pltpu.VMEM((D_MODEL // LANES, SUBLANES * SEG_PITCH, LANES), F32),
            pltpu.VMEM((D_MODEL // LANES, SUBLANES * SEG_PITCH, LANES), F32),
            pltpu.VMEM((D_MODEL // LANES, SUBLANES * SEG_PITCH, LANES), F32),
            pltpu.VMEM((D_MODEL // LANES, SUBLANES * SEG_PITCH, LANES), F32),
            pltpu.VMEM((SUBLANES, D_MODEL), F32),
        ],
        compiler_params=pltpu.CompilerParams(
            dimension_semantics=("parallel", "arbitrary"), vmem_limit_bytes=VMEM_LIMIT),
        name="rglru",
    )(rest, rest, cw, cb, wr, br, wi, bi, lam)


def _block_diag_tiles(w):
    per = MXU_DIM // RG_BLOCK
    n_diag = D_MODEL // MXU_DIM
    w4 = w.reshape(n_diag, per, RG_BLOCK, RG_BLOCK)
    eye = jnp.eye(per, dtype=w.dtype)
    tiles = jnp.einsum('dpij,pq->dpiqj', w4, eye)
    return tiles.reshape(n_diag, MXU_DIM, MXU_DIM)


def _merge_kernel(x_ref, at_ref, hb_ref, ga_ref, gb_ref, bm_ref, wa_ref, wb_ref, wo_ref, h_ref):
    ya = jnp.dot(at_ref[...], wa_ref[...], preferred_element_type=F32)
    yb = jnp.dot(hb_ref[...], wb_ref[...], preferred_element_type=F32)
    g_a = jax.nn.sigmoid(ga_ref[...].astype(F32) + bm_ref[:, :D_MODEL])
    g_b = jax.nn.sigmoid(gb_ref[...].astype(F32) + bm_ref[:, D_MODEL:])
    mix = (g_a * ya + g_b * yb).astype(BF16)
    h_ref[...] = x_ref[...] + jnp.dot(mix, wo_ref[...], preferred_element_type=F32)


def _merge(x2, attn, hb, rest, bm, wa, wb, wo, tm):
    t = x2.shape[0]
    row = lambda i: (i, 0)
    return pl.pallas_call(
        _merge_kernel,
        grid=(t // tm,),
        in_specs=[
            pl.BlockSpec((tm, D_MODEL), row),
            pl.BlockSpec((tm, D_MODEL), row),
            pl.BlockSpec((tm, D_MODEL), row),
            pl.BlockSpec((tm, D_MODEL), lambda i: (i, 2)),
            pl.BlockSpec((tm, D_MODEL), lambda i: (i, 3)),
            _resident((1, 2 * D_MODEL)),
            _resident((D_MODEL, D_MODEL)),
            _resident((D_MODEL, D_MODEL)),
            _resident((D_MODEL, D_MODEL)),
        ],
        out_specs=pl.BlockSpec((tm, D_MODEL), row),
        out_shape=jax.ShapeDtypeStruct((t, D_MODEL), F32),
        compiler_params=pltpu.CompilerParams(
            dimension_semantics=("parallel",), vmem_limit_bytes=VMEM_LIMIT),
        name="merge",
    )(x2, attn, hb, rest, rest, bm, wa, wb, wo)


def _ffn_kernel(h_ref, gn_ref, win_ref, wout_ref, gf_ref, o_ref, *, splits):
    h = h_ref[...]
    hn = _rms(h, gn_ref[...]).astype(BF16)
    d_ff = wout_ref.shape[0]
    acc = h
    for c0, c1 in splits:
        g = jnp.dot(hn, win_ref[:, c0:c1], preferred_element_type=F32)
        up = jnp.dot(hn, win_ref[:, d_ff + c0:d_ff + c1], preferred_element_type=F32)
        act = (jax.nn.silu(g) * up).astype(BF16)
        acc = acc + jnp.dot(act, wout_ref[c0:c1, :], preferred_element_type=F32)
    o_ref[...] = _rms(acc, gf_ref[...])


def _ffn(h, gn, win, wout, gf, tm):
    t = h.shape[0]
    d_ff = wout.shape[0]
    n_tiles = d_ff // MXU_DIM
    half = (n_tiles + 1) // 2 * MXU_DIM
    splits = ((0, half), (half, d_ff))
    row = lambda i: (i, 0)
    return pl.pallas_call(
        functools.partial(_ffn_kernel, splits=splits),
        grid=(t // tm,),
        in_specs=[
            pl.BlockSpec((tm, D_MODEL), row),
            _resident((1, D_MODEL)),
            _resident((D_MODEL, 2 * d_ff)),
            _resident((d_ff, D_MODEL)),
            _resident((1, D_MODEL)),
        ],
        out_specs=pl.BlockSpec((tm, D_MODEL), row),
        out_shape=jax.ShapeDtypeStruct((t, D_MODEL), F32),
        compiler_params=pltpu.CompilerParams(
            dimension_semantics=("parallel",), vmem_limit_bytes=VMEM_LIMIT),
        name="ffn",
    )(h, gn, win, wout, gf)


def kernel(x, norm_mix_g, w_in, b_merge, rel_table, w_attn_out, conv_w, conv_b, w_rg_r, b_rg_r,
           w_rg_i, b_rg_i, rg_lambda, w_rnn_out, w_o, norm_ffn_g, w_ffn_in, w_ffn_out, final_norm_g):
    batch, seq, d = x.shape
    assert d == D_MODEL and seq % GROUP == 0 and seq // GROUP >= BAND_TILES
    assert w_in.shape[0] == 1, "single-layer block"
    x2 = x.reshape(batch * seq, d)
    row = lambda v: v.reshape(1, -1)

    qkv, rest = _inproj(x2, row(norm_mix_g[0]), w_in[0].astype(BF16), tm=512)
    attn = _attention(qkv, _band_bias(rel_table[0]), batch, seq)
    hb = _rglru(rest, conv_w[0], row(conv_b[0]),
                _block_diag_tiles(w_rg_r[0]).astype(BF16), row(b_rg_r[0]),
                _block_diag_tiles(w_rg_i[0]).astype(BF16), row(b_rg_i[0]),
                row(rg_lambda[0]), batch, seq, tm=RG_TILE)
    h = _merge(x2, attn, hb, rest, row(b_merge[0]), w_attn_out[0].astype(BF16),
               w_rnn_out[0].astype(BF16), w_o[0].astype(BF16), tm=512)
    out = _ffn(h, row(norm_ffn_g[0]), w_ffn_in[0].astype(BF16), w_ffn_out[0].astype(BF16),
               row(final_norm_g), tm=512)
    return out.reshape(batch, seq, d)
```

```python
import functools
import math

import jax
import jax.numpy as jnp
from jax import lax
from jax.experimental import pallas as pl
from jax.experimental.pallas import tpu as pltpu

F32 = jnp.float32
BF16 = jnp.bfloat16

D_MODEL = 1024
CHUNK = 64
LEFT_CHUNKS = 8
N_HEADS = 16
HEAD_DIM = 64
MAX_REL = 128
N_RG_BLOCKS = 16
RG_BLOCK = D_MODEL // N_RG_BLOCKS
CONV_W = 4
RG_C = 8.0
EPS = 1e-6
NEG_INF = -1e30

LANES = 128
SUBLANES = 8
MXU_DIM = 256
HEAD_PAIR = 2 * HEAD_DIM
N_PAIRS = N_HEADS // 2
GROUP = 4 * CHUNK
BAND_TILES = LEFT_CHUNKS * CHUNK // GROUP + 1
BAND = BAND_TILES * GROUP
VMEM_LIMIT = 56 * 1024 * 1024
RG_TILE = 512
SEG_PITCH = RG_TILE // SUBLANES + SUBLANES // 2


def _rms(x, g):
    var = jnp.mean(x * x, axis=-1, keepdims=True)
    return x * lax.rsqrt(var + EPS) * g


def _resident(shape):
    zeros = (0,) * len(shape)
    return pl.BlockSpec(shape, lambda *_: zeros, pipeline_mode=pl.Buffered(1))


def _inproj_kernel(x_ref, g_ref, w_ref, qkv_ref, rest_ref):
    xn = _rms(x_ref[...], g_ref[...]).astype(BF16)
    for j in range(7):
        acc = jnp.dot(xn, w_ref[:, j * D_MODEL:(j + 1) * D_MODEL],
                      preferred_element_type=F32)
        if j < 3:
            for hp in range(N_PAIRS):
                qkv_ref[j * N_PAIRS + hp] = acc[:, hp * HEAD_PAIR:(hp + 1) * HEAD_PAIR].astype(BF16)
        else:
            rest_ref[:, (j - 3) * D_MODEL:(j - 2) * D_MODEL] = acc.astype(BF16)


def _inproj(x2, g, w, tm):
    t = x2.shape[0]
    return pl.pallas_call(
        _inproj_kernel,
        grid=(t // tm,),
        in_specs=[
            pl.BlockSpec((tm, D_MODEL), lambda i: (i, 0)),
            _resident((1, D_MODEL)),
            _resident((D_MODEL, 7 * D_MODEL)),
        ],
        out_specs=[
            pl.BlockSpec((3 * N_PAIRS, tm, HEAD_PAIR), lambda i: (0, i, 0)),
            pl.BlockSpec((tm, 4 * D_MODEL), lambda i: (i, 0)),
        ],
        out_shape=[
            jax.ShapeDtypeStruct((3 * N_PAIRS, t, HEAD_PAIR), BF16),
            jax.ShapeDtypeStruct((t, 4 * D_MODEL), BF16),
        ],
        compiler_params=pltpu.CompilerParams(
            dimension_semantics=("parallel",), vmem_limit_bytes=VMEM_LIMIT),
        name="inproj",
    )(x2, g, w)


def _band_geometry(ntiles):
    kw = ntiles * GROUP
    return kw, BAND - kw


def _attn_scores(q_ref, k_ref, s_ref, r0, ntiles):
    kw, c_off = _band_geometry(ntiles)
    lane = lax.broadcasted_iota(jnp.int32, (GROUP, HEAD_PAIR), 1)
    first_head = lane < HEAD_DIM
    qp = q_ref[0, pl.ds(r0, GROUP), :] * jnp.asarray(1.0 / math.sqrt(HEAD_DIM), BF16)
    zero = jnp.zeros_like(qp)
    qq = jnp.concatenate([jnp.where(first_head, qp, zero),
                          jnp.where(first_head, zero, qp)], axis=0)
    kk = k_ref[0, pl.ds(r0 + GROUP - kw, kw), :]
    s_ref[:, c_off:] = lax.dot_general(qq, kk, (((1,), (1,)), ((), ())),
                                       preferred_element_type=F32)


def _attn_softmax(s_ref, bias_ref, p_ref, l_ref, ntiles):
    _, c_off = _band_geometry(ntiles)
    rb = 32
    lane_rb = lax.broadcasted_iota(jnp.int32, (rb, LANES), 1)
    for hh in range(2):
        for i in range(GROUP // CHUNK):
            lo_col = max(i * CHUNK, c_off)
            hi_col = i * CHUNK + (LEFT_CHUNKS + 1) * CHUNK
            ws = (lo_col // LANES) * LANES
            we = -(-hi_col // LANES) * LANES
            nv = (we - ws) // LANES
            rows_all = slice(hh * GROUP + i * CHUNK, hh * GROUP + (i + 1) * CHUNK)
            if ws > c_off:
                p_ref[rows_all, c_off:ws] = jnp.zeros((CHUNK, ws - c_off), BF16)
            if we < BAND:
                p_ref[rows_all, we:BAND] = jnp.zeros((CHUNK, BAND - we), BF16)
            for b in range(CHUNK // rb):
                rows = slice(hh * GROUP + i * CHUNK + b * rb, hh * GROUP + i * CHUNK + (b + 1) * rb)
                brows = slice(i * CHUNK + b * rb, i * CHUNK + (b + 1) * rb)
                x = s_ref[rows, ws:we] + bias_ref[hh, brows, ws:we]
                cols = [x[:, c * LANES:(c + 1) * LANES] for c in range(nv)]
                if lo_col % LANES:
                    cols[0] = jnp.where(lane_rb >= HEAD_DIM, cols[0], NEG_INF)
                if hi_col % LANES:
                    cols[-1] = jnp.where(lane_rb < HEAD_DIM, cols[-1], NEG_INF)
                x = jnp.concatenate(cols, axis=1)
                m = jnp.max(x, axis=-1, keepdims=True)
                e = jnp.exp(x - m)
                l = jnp.sum(e, axis=-1, keepdims=True)
                p_ref[rows, ws:we] = e.astype(BF16)
                l_ref[rows, :] = jnp.broadcast_to(l, (rb, LANES))

def _attn_values(p_ref, l_ref, v_ref, o_ref, r0, ntiles):
    kw, c_off = _band_geometry(ntiles)
    lane = lax.broadcasted_iota(jnp.int32, (GROUP, HEAD_PAIR), 1)
    vv = v_ref[0, pl.ds(r0 + GROUP - kw, kw), :]
    o2 = jnp.dot(p_ref[:, c_off:], vv, preferred_element_type=F32)
    o2 = o2 / l_ref[...]
    o = jnp.where(lane < HEAD_DIM, o2[:GROUP], o2[GROUP:])
    o_ref[pl.ds(r0, GROUP), :] = o.astype(BF16)


def _attn_kernel(q_ref, k_ref, v_ref, bias_ref, o_ref, s0, s1, p0, p1, l0, l1):
    n_groups = q_ref.shape[1] // GROUP
    full = BAND_TILES
    row0 = lambda g: pl.multiple_of(g * GROUP, GROUP)

    def scores(g, s_ref, ntiles=full):
        _attn_scores(q_ref, k_ref, s_ref, row0(g), ntiles)

    def softmax(s_ref, p_ref, l_ref, ntiles=full):
        _attn_softmax(s_ref, bias_ref, p_ref, l_ref, ntiles)

    def values(g, p_ref, l_ref, ntiles=full):
        _attn_values(p_ref, l_ref, v_ref, o_ref, row0(g), ntiles)

    first_full = full - 1
    for g in range(first_full):
        scores(g, s0, g + 1)
        softmax(s0, p0, l0, g + 1)
        values(g, p0, l0, g + 1)

    scores(first_full, s0)
    softmax(s0, p0, l0)
    scores(first_full + 1, s1)

    def pair(g, last):
        values(g, p0, l0)
        softmax(s1, p1, l1)
        if not last:
            scores(g + 2, s0)
        values(g + 1, p1, l1)
        if not last:
            softmax(s0, p0, l0)
            scores(g + 3, s1)

    n_pairs = (n_groups - first_full) // 2

    def body(k, carry):
        pair(first_full + 2 * k, last=False)
        return carry

    lax.fori_loop(0, n_pairs - 1, body, 0)
    pair(n_groups - 2, last=True)


def _attention(qkv, bias, batch, seq):
    t = batch * seq
    return pl.pallas_call(
        _attn_kernel,
        grid=(N_PAIRS, batch),
        in_specs=[
            pl.BlockSpec((1, seq, HEAD_PAIR), lambda hp, b: (hp, b, 0)),
            pl.BlockSpec((1, seq, HEAD_PAIR), lambda hp, b: (N_PAIRS + hp, b, 0)),
            pl.BlockSpec((1, seq, HEAD_PAIR), lambda hp, b: (2 * N_PAIRS + hp, b, 0)),
            pl.BlockSpec((2, GROUP, BAND), lambda hp, b: (hp, 0, 0)),
        ],
        out_specs=pl.BlockSpec((seq, HEAD_PAIR), lambda hp, b: (b, hp)),
        out_shape=jax.ShapeDtypeStruct((t, D_MODEL), BF16),
        scratch_shapes=[
            pltpu.VMEM((2 * GROUP, BAND), F32),
            pltpu.VMEM((2 * GROUP, BAND), F32),
            pltpu.VMEM((2 * GROUP, BAND), BF16),
            pltpu.VMEM((2 * GROUP, BAND), BF16),
            pltpu.VMEM((2 * GROUP, LANES), F32),
            pltpu.VMEM((2 * GROUP, LANES), F32),
        ],
        compiler_params=pltpu.CompilerParams(
            dimension_semantics=("parallel", "parallel"), vmem_limit_bytes=VMEM_LIMIT),
        name="band_attention",
    )(qkv, qkv, qkv, bias)


def _band_bias(rel_table):
    n_diag = GROUP + BAND - 1
    j = jnp.arange(n_diag)
    dist = (GROUP - 1 - j) + (BAND - GROUP)
    w = rel_table[:, jnp.clip(dist, -MAX_REL, MAX_REL) + MAX_REL]
    flat = jnp.tile(jnp.pad(w, ((0, 0), (0, 1))), (1, GROUP))[:, :GROUP * n_diag]
    return flat.reshape(-1, GROUP, n_diag)[:, :, GROUP - 1:GROUP - 1 + BAND]


def _gelu_tanh(x):
    k = math.sqrt(2.0 / math.pi)
    inner = x * (k + (k * 0.044715) * (x * x))
    return x * (0.5 + 0.5 * jnp.tanh(inner))


def _rglru_kernel(xr_ref, gr_ref, cw_ref, cb_ref, wr_ref, br_ref, wi_ref, bi_ref, lam_ref,
                  o_ref, xs_ref, a_ref, b_ref, hl_ref, pp_ref, h_ref):
    tm = xr_ref.shape[0]
    seg = tm // SUBLANES
    n_lane_tiles = D_MODEL // LANES
    lane_tile = lambda c: slice(c * LANES, (c + 1) * LANES)

    @pl.when(pl.program_id(1) == 0)
    def _():
        xs_ref[:, 0:SUBLANES, :] = jnp.zeros((n_lane_tiles, SUBLANES, LANES), F32)
        h_ref[...] = jnp.zeros_like(h_ref)

    pieces = []
    for c in range(n_lane_tiles):
        xs_ref[c, SUBLANES:, :] = xr_ref[:, lane_tile(c)].astype(F32)
        acc = cb_ref[:, lane_tile(c)]
        for j in range(CONV_W):
            off = SUBLANES - (CONV_W - 1) + j
            acc = acc + xs_ref[c, off:off + tm, :] * cw_ref[j:j + 1, lane_tile(c)]
        pieces.append(acc)
        xs_ref[c, 0:SUBLANES, :] = xs_ref[c, tm:tm + SUBLANES, :]
    xc = jnp.concatenate(pieces, axis=1)

    xcb = xc.astype(BF16)
    n_diag = D_MODEL // MXU_DIM
    r_pre = jnp.concatenate(
        [jnp.dot(xcb[:, d * MXU_DIM:(d + 1) * MXU_DIM], wr_ref[d], preferred_element_type=F32)
         for d in range(n_diag)], axis=1)
    i_pre = jnp.concatenate(
        [jnp.dot(xcb[:, d * MXU_DIM:(d + 1) * MXU_DIM], wi_ref[d], preferred_element_type=F32)
         for d in range(n_diag)], axis=1)
    r = jax.nn.sigmoid(r_pre + br_ref[...])
    ig = jax.nn.sigmoid(i_pre + bi_ref[...])
    nlam = -lam_ref[...]
    softplus = jnp.maximum(nlam, 0.0) + jnp.log1p(jnp.exp(-jnp.abs(nlam)))
    log_a = r * ((-RG_C) * softplus)
    a = jnp.exp(log_a)
    y = jnp.tanh(log_a) * (-1.0 - a * a)
    mult = jnp.where(y > 0.0, y * lax.rsqrt(y), 0.0)
    bx = mult * (ig * xc)

    for c in range(n_lane_tiles):
        for j in range(SUBLANES):
            a_ref[c, j * SEG_PITCH:j * SEG_PITCH + seg, :] = a[j * seg:(j + 1) * seg, lane_tile(c)]
            b_ref[c, j * SEG_PITCH:j * SEG_PITCH + seg, :] = bx[j * seg:(j + 1) * seg, lane_tile(c)]

    def step(t, carry):
        hs, ps = carry
        idx = pl.ds(t, SUBLANES, stride=SEG_PITCH)
        new_h, new_p = [], []
        for c in range(n_lane_tiles):
            a_t = a_ref[c, idx, :]
            h = a_t * hs[c] + b_ref[c, idx, :]
            p = a_t * ps[c]
            hl_ref[c, idx, :] = h
            pp_ref[c, idx, :] = p
            new_h.append(h)
            new_p.append(p)
        return tuple(new_h), tuple(new_p)

    zeros = tuple(jnp.zeros((SUBLANES, LANES), F32) for _ in range(n_lane_tiles))
    ones = tuple(jnp.ones((SUBLANES, LANES), F32) for _ in range(n_lane_tiles))
    h_end, p_end = lax.fori_loop(0, seg, step, (zeros, ones))
    h_end = jnp.concatenate(h_end, axis=1)
    p_end = jnp.concatenate(p_end, axis=1)

    h_in = h_ref[0:1, :]
    for j in range(SUBLANES):
        rows = slice(j * seg, (j + 1) * seg)
        srows = slice(j * SEG_PITCH, j * SEG_PITCH + seg)
        local = jnp.concatenate([hl_ref[c, srows, :] for c in range(n_lane_tiles)], axis=1)
        decay = jnp.concatenate([pp_ref[c, srows, :] for c in range(n_lane_tiles)], axis=1)
        hj = local + decay * h_in
        o_ref[rows, :] = (hj * _gelu_tanh(gr_ref[rows, :].astype(F32))).astype(BF16)
        h_in = p_end[j:j + 1, :] * h_in + h_end[j:j + 1, :]
    h_ref[0:1, :] = h_in


def _rglru(rest, cw, cb, wr, br, wi, bi, lam, batch, seq, tm):
    t = batch * seq
    nt = seq // tm
    n_diag = D_MODEL // MXU_DIM
    return pl.pallas_call(
        _rglru_kernel,
        grid=(batch, nt),
        in_specs=[
            pl.BlockSpec((tm, D_MODEL), lambda b, i: (b * nt + i, 0)),
            pl.BlockSpec((tm, D_MODEL), lambda b, i: (b * nt + i, 1)),
            _resident((CONV_W, D_MODEL)),
            _resident((1, D_MODEL)),
            _resident((n_diag, MXU_DIM, MXU_DIM)),
            _resident((1, D_MODEL)),
            _resident((n_diag, MXU_DIM, MXU_DIM)),
            _resident((1, D_MODEL)),
            _resident((1, D_MODEL)),
        ],
        out_specs=pl.BlockSpec((tm, D_MODEL), lambda b, i: (b * nt + i, 0)),
        out_shape=jax.ShapeDtypeStruct((t, D_MODEL), BF16),
        scratch_shapes=[
            pltpu.VMEM((D_MODEL // LANES, tm + SUBLANES, LANES), F32),
            pltpu.VMEM((D_MODEL // LANES, SUBLANES * SEG_PITCH, LANES), F32),
            pltpu.VMEM((D_MODEL // LANES, SUBLANES * SEG_PITCH, LANES), F32),
            pltpu.VMEM((D_MODEL // LANES, SUBLANES * SEG_PITCH, LANES), F32),
            pltpu.VMEM((D_MODEL // LANES, SUBLANES * SEG_PITCH, LANES), F32),
            pltpu.VMEM((SUBLANES, D_MODEL), F32),
        ],
        compiler_params=pltpu.CompilerParams(
            dimension_semantics=("parallel", "arbitrary"), vmem_limit_bytes=VMEM_LIMIT),
        name="rglru",
    )(rest, rest, cw, cb, wr, br, wi, bi, lam)


def _block_diag_tiles(w):
    per = MXU_DIM // RG_BLOCK
    n_diag = D_MODEL // MXU_DIM
    w4 = w.reshape(n_diag, per, RG_BLOCK, RG_BLOCK)
    eye = jnp.eye(per, dtype=w.dtype)
    tiles = jnp.einsum('dpij,pq->dpiqj', w4, eye)
    return tiles.reshape(n_diag, MXU_DIM, MXU_DIM)


def _merge_kernel(x_ref, at_ref, hb_ref, ga_ref, gb_ref, bm_ref, wa_ref, wb_ref, wo_ref, h_ref):
    ya = jnp.dot(at_ref[...], wa_ref[...], preferred_element_type=F32)
    yb = jnp.dot(hb_ref[...], wb_ref[...], preferred_element_type=F32)
    g_a = jax.nn.sigmoid(ga_ref[...].astype(F32) + bm_ref[:, :D_MODEL])
    g_b = jax.nn.sigmoid(gb_ref[...].astype(F32) + bm_ref[:, D_MODEL:])
    mix = (g_a * ya + g_b * yb).astype(BF16)
    h_ref[...] = x_ref[...] + jnp.dot(mix, wo_ref[...], preferred_element_type=F32)


def _merge(x2, attn, hb, rest, bm, wa, wb, wo, tm):
    t = x2.shape[0]
    row = lambda i: (i, 0)
    return pl.pallas_call(
        _merge_kernel,
        grid=(t // tm,),
        in_specs=[
            pl.BlockSpec((tm, D_MODEL), row),
            pl.BlockSpec((tm, D_MODEL), row),
            pl.BlockSpec((tm, D_MODEL), row),
            pl.BlockSpec((tm, D_MODEL), lambda i: (i, 2)),
            pl.BlockSpec((tm, D_MODEL), lambda i: (i, 3)),
            _resident((1, 2 * D_MODEL)),
            _resident((D_MODEL, D_MODEL)),
            _resident((D_MODEL, D_MODEL)),
            _resident((D_MODEL, D_MODEL)),
        ],
        out_specs=pl.BlockSpec((tm, D_MODEL), row),
        out_shape=jax.ShapeDtypeStruct((t, D_MODEL), F32),
        compiler_params=pltpu.CompilerParams(
            dimension_semantics=("parallel",), vmem_limit_bytes=VMEM_LIMIT),
        name="merge",
    )(x2, attn, hb, rest, rest, bm, wa, wb, wo)


def _ffn_kernel(h_ref, gn_ref, win_ref, wout_ref, gf_ref, o_ref, *, splits):
    h = h_ref[...]
    hn = _rms(h, gn_ref[...]).astype(BF16)
    d_ff = wout_ref.shape[0]
    acc = h
    for c0, c1 in splits:
        g = jnp.dot(hn, win_ref[:, c0:c1], preferred_element_type=F32)
        up = jnp.dot(hn, win_ref[:, d_ff + c0:d_ff + c1], preferred_element_type=F32)
        act = (jax.nn.silu(g) * up).astype(BF16)
        acc = acc + jnp.dot(act, wout_ref[c0:c1, :], preferred_element_type=F32)
    o_ref[...] = _rms(acc, gf_ref[...])


def _ffn(h, gn, win, wout, gf, tm):
    t = h.shape[0]
    d_ff = wout.shape[0]
    n_tiles = d_ff // MXU_DIM
    half = (n_tiles + 1) // 2 * MXU_DIM
    splits = ((0, half), (half, d_ff))
    row = lambda i: (i, 0)
    return pl.pallas_call(
        functools.partial(_ffn_kernel, splits=splits),
        grid=(t // tm,),
        in_specs=[
            pl.BlockSpec((tm, D_MODEL), row),
            _resident((1, D_MODEL)),
            _resident((D_MODEL, 2 * d_ff)),
            _resident((d_ff, D_MODEL)),
            _resident((1, D_MODEL)),
        ],
        out_specs=pl.BlockSpec((tm, D_MODEL), row),
        out_shape=jax.ShapeDtypeStruct((t, D_MODEL), F32),
        compiler_params=pltpu.CompilerParams(
            dimension_semantics=("parallel",), vmem_limit_bytes=VMEM_LIMIT),
        name="ffn",
    )(h, gn, win, wout, gf)


def kernel(x, norm_mix_g, w_in, b_merge, rel_table, w_attn_out, conv_w, conv_b, w_rg_r, b_rg_r,
           w_rg_i, b_rg_i, rg_lambda, w_rnn_out, w_o, norm_ffn_g, w_ffn_in, w_ffn_out, final_norm_g):
    batch, seq, d = x.shape
    assert d == D_MODEL and seq % GROUP == 0 and seq // GROUP >= BAND_TILES
    assert w_in.shape[0] == 1, "single-layer block"
    x2 = x.reshape(batch * seq, d)
    row = lambda v: v.reshape(1, -1)

    qkv, rest = _inproj(x2, row(norm_mix_g[0]), w_in[0].astype(BF16), tm=512)
    attn = _attention(qkv, _band_bias(rel_table[0]), batch, seq)
    hb = _rglru(rest, conv_w[0], row(conv_b[0]),
                _block_diag_tiles(w_rg_r[0]).astype(BF16), row(b_rg_r[0]),
                _block_diag_tiles(w_rg_i[0]).astype(BF16), row(b_rg_i[0]),
                row(rg_lambda[0]), batch, seq, tm=RG_TILE)
    h = _merge(x2, attn, hb, rest, row(b_merge[0]), w_attn_out[0].astype(BF16),
               w_rnn_out[0].astype(BF16), w_o[0].astype(BF16), tm=512)
    out = _ffn(h, row(norm_ffn_g[0]), w_ffn_in[0].astype(BF16), w_ffn_out[0].astype(BF16),
               row(final_norm_g), tm=512)
    return out.reshape(batch, seq, d)
```

```python
import functools
import math

import jax
import jax.numpy as jnp
from jax import lax
from jax.experimental import pallas as pl
from jax.experimental.pallas import tpu as pltpu

F32 = jnp.float32
BF16 = jnp.bfloat16

D_MODEL = 1024
CHUNK = 64
LEFT_CHUNKS = 8
N_HEADS = 16
HEAD_DIM = 64
MAX_REL = 128
N_RG_BLOCKS = 16
RG_BLOCK = D_MODEL // N_RG_BLOCKS
CONV_W = 4
RG_C = 8.0
EPS = 1e-6
NEG_INF = -1e30

LANES = 128
SUBLANES = 8
MXU_DIM = 256
HEAD_PAIR = 2 * HEAD_DIM
N_PAIRS = N_HEADS // 2
GROUP = 4 * CHUNK
BAND_TILES = LEFT_CHUNKS * CHUNK // GROUP + 1
BAND = BAND_TILES * GROUP
VMEM_LIMIT = 56 * 1024 * 1024
RG_TILE = 512
SEG_PITCH = RG_TILE // SUBLANES + SUBLANES // 2
COST_CONV, COST_GATE, COST_SCAN_STEP, COST_OUT = 16, 52, 23, 24


def _rms(x, g):
    var = jnp.mean(x * x, axis=-1, keepdims=True)
    return x * lax.rsqrt(var + EPS) * g


def _resident(shape):
    zeros = (0,) * len(shape)
    return pl.BlockSpec(shape, lambda *_: zeros, pipeline_mode=pl.Buffered(1))


def _inproj_kernel(x_ref, g_ref, w_ref, cw_ref, cb_ref, wr_ref, br_ref, wi_ref, bi_ref, lam_ref,
                   qkv_ref, gates_ref, hb_ref, xn_ref, gr_ref, xs_ref, xc_ref, rp_ref, ip_ref,
                   a_ref, b_ref, hl_ref, pp_ref, h_ref, *, tiles_per_seq):
    @pl.when(pl.program_id(0) % tiles_per_seq == 0)
    def _():
        xs_ref[:, 0:SUBLANES, :] = jnp.zeros((D_MODEL // LANES, SUBLANES, LANES), F32)
        h_ref[...] = jnp.zeros_like(h_ref)

    xn_ref[...] = _rms(x_ref[...], g_ref[...]).astype(BF16)

    def u_cols(c0, width):
        return jnp.dot(xn_ref[...], w_ref[:, c0:c0 + width], preferred_element_type=F32)

    def unit(j, n):
        acc = u_cols(j * D_MODEL + n * MXU_DIM, MXU_DIM)
        if j < 3:
            for k in range(MXU_DIM // HEAD_PAIR):
                hp = n * (MXU_DIM // HEAD_PAIR) + k
                qkv_ref[j * N_PAIRS + hp] = acc[:, k * HEAD_PAIR:(k + 1) * HEAD_PAIR].astype(BF16)
        elif j == 4:
            gr_ref[:, n * MXU_DIM:(n + 1) * MXU_DIM] = acc
        else:
            c0 = (j - 5) * D_MODEL + n * MXU_DIM
            gates_ref[:, c0:c0 + MXU_DIM] = acc.astype(BF16)

    units = [(j, n) for j in (4, 0, 1, 2, 5, 6) for n in range(D_MODEL // MXU_DIM)]
    n_pieces = SUBLANES * (D_MODEL // LANES)
    total_cost = n_pieces * (COST_CONV + COST_GATE + COST_OUT) + (x_ref.shape[0] // SUBLANES) * COST_SCAN_STEP
    cost_per_unit = total_cost / len(units)
    spent = [0.0]

    def side_work(cost):
        spent[0] += cost
        while units and spent[0] >= cost_per_unit:
            spent[0] -= cost_per_unit
            unit(*units.pop(0))

    _rglru_tile(u_cols(3 * D_MODEL, D_MODEL), gr_ref, side_work,
                cw_ref, cb_ref, wr_ref, br_ref, wi_ref, bi_ref, lam_ref,
                hb_ref, xs_ref, xc_ref, rp_ref, ip_ref, a_ref, b_ref, hl_ref, pp_ref, h_ref)
    while units:
        unit(*units.pop(0))


def _inproj(x2, g, w, cw, cb, wr, br, wi, bi, lam, seq):
    t = x2.shape[0]
    tm = RG_TILE
    n_diag = D_MODEL // MXU_DIM
    scan_scratch = pltpu.VMEM((D_MODEL // LANES, SUBLANES * SEG_PITCH, LANES), F32)
    slab_scratch = pltpu.VMEM((tm, MXU_DIM), F32)
    return pl.pallas_call(
        functools.partial(_inproj_kernel, tiles_per_seq=seq // tm),
        grid=(t // tm,),
        in_specs=[
            pl.BlockSpec((tm, D_MODEL), lambda i: (i, 0)),
            _resident((1, D_MODEL)),
            _resident((D_MODEL, 7 * D_MODEL)),
            _resident((CONV_W, D_MODEL)),
            _resident((1, D_MODEL)),
            _resident((n_diag, MXU_DIM, MXU_DIM)),
            _resident((1, D_MODEL)),
            _resident((n_diag, MXU_DIM, MXU_DIM)),
            _resident((1, D_MODEL)),
            _resident((1, D_MODEL)),
        ],
        out_specs=[
            pl.BlockSpec((3 * N_PAIRS, tm, HEAD_PAIR), lambda i: (0, i, 0)),
            pl.BlockSpec((tm, 2 * D_MODEL), lambda i: (i, 0)),
            pl.BlockSpec((tm, D_MODEL), lambda i: (i, 0)),
        ],
        out_shape=[
            jax.ShapeDtypeStruct((3 * N_PAIRS, t, HEAD_PAIR), BF16),
            jax.ShapeDtypeStruct((t, 2 * D_MODEL), BF16),
            jax.ShapeDtypeStruct((t, D_MODEL), BF16),
        ],
        scratch_shapes=[
            pltpu.VMEM((tm, D_MODEL), BF16),
            pltpu.VMEM((tm, D_MODEL), F32),
            pltpu.VMEM((D_MODEL // LANES, tm + SUBLANES, LANES), F32),
            slab_scratch, slab_scratch, slab_scratch,
            scan_scratch, scan_scratch, scan_scratch, scan_scratch,
            pltpu.VMEM((SUBLANES, D_MODEL), F32),
        ],
        compiler_params=pltpu.CompilerParams(
            dimension_semantics=("arbitrary",), vmem_limit_bytes=VMEM_LIMIT),
        name="inproj_rglru",
    )(x2, g, w, cw, cb, wr, br, wi, bi, lam)


def _band_geometry(ntiles):
    kw = ntiles * GROUP
    return kw, BAND - kw


def _attn_scores(q_ref, k_ref, s_ref, r0, ntiles):
    kw, c_off = _band_geometry(ntiles)
    lane = lax.broadcasted_iota(jnp.int32, (GROUP, HEAD_PAIR), 1)
    first_head = lane < HEAD_DIM
    qp = q_ref[0, pl.ds(r0, GROUP), :] * jnp.asarray(1.0 / math.sqrt(HEAD_DIM), BF16)
    zero = jnp.zeros_like(qp)
    qq = jnp.concatenate([jnp.where(first_head, qp, zero),
                          jnp.where(first_head, zero, qp)], axis=0)
    kk = k_ref[0, pl.ds(r0 + GROUP - kw, kw), :]
    s_ref[:, c_off:] = lax.dot_general(qq, kk, (((1,), (1,)), ((), ())),
                                       preferred_element_type=F32)


def _attn_softmax(s_ref, bias_ref, p_ref, l_ref, ntiles):
    _, c_off = _band_geometry(ntiles)
    rb = 32
    lane_rb = lax.broadcasted_iota(jnp.int32, (rb, LANES), 1)
    for hh in range(2):
        for i in range(GROUP // CHUNK):
            lo_col = max(i * CHUNK, c_off)
            hi_col = i * CHUNK + (LEFT_CHUNKS + 1) * CHUNK
            ws = (lo_col // LANES) * LANES
            we = -(-hi_col // LANES) * LANES
            nv = (we - ws) // LANES
            rows_all = slice(hh * GROUP + i * CHUNK, hh * GROUP + (i + 1) * CHUNK)
            if ws > c_off:
                p_ref[rows_all, c_off:ws] = jnp.zeros((CHUNK, ws - c_off), BF16)
            if we < BAND:
                p_ref[rows_all, we:BAND] = jnp.zeros((CHUNK, BAND - we), BF16)
            for b in range(CHUNK // rb):
                rows = slice(hh * GROUP + i * CHUNK + b * rb, hh * GROUP + i * CHUNK + (b + 1) * rb)
                brows = slice(i * CHUNK + b * rb, i * CHUNK + (b + 1) * rb)
                x = s_ref[rows, ws:we] + bias_ref[hh, brows, ws:we]
                cols = [x[:, c * LANES:(c + 1) * LANES] for c in range(nv)]
                if lo_col % LANES:
                    cols[0] = jnp.where(lane_rb >= HEAD_DIM, cols[0], NEG_INF)
                if hi_col % LANES:
                    cols[-1] = jnp.where(lane_rb < HEAD_DIM, cols[-1], NEG_INF)
                x = jnp.concatenate(cols, axis=1)
                m = jnp.max(x, axis=-1, keepdims=True)
                e = jnp.exp(x - m)
                l = jnp.sum(e, axis=-1, keepdims=True)
                p_ref[rows, ws:we] = e.astype(BF16)
                l_ref[rows, :] = jnp.broadcast_to(l, (rb, LANES))

def _attn_values(p_ref, l_ref, v_ref, o_ref, r0, ntiles):
    kw, c_off = _band_geometry(ntiles)
    lane = lax.broadcasted_iota(jnp.int32, (GROUP, HEAD_PAIR), 1)
    vv = v_ref[0, pl.ds(r0 + GROUP - kw, kw), :]
    o2 = jnp.dot(p_ref[:, c_off:], vv, preferred_element_type=F32)
    o2 = o2 / l_ref[...]
    o = jnp.where(lane < HEAD_DIM, o2[:GROUP], o2[GROUP:])
    o_ref[pl.ds(r0, GROUP), :] = o.astype(BF16)


def _attn_kernel(q_ref, k_ref, v_ref, bias_ref, o_ref, s0, s1, p0, p1, l0, l1):
    n_groups = q_ref.shape[1] // GROUP
    full = BAND_TILES
    row0 = lambda g: pl.multiple_of(g * GROUP, GROUP)

    def scores(g, s_ref, ntiles=full):
        _attn_scores(q_ref, k_ref, s_ref, row0(g), ntiles)

    def softmax(s_ref, p_ref, l_ref, ntiles=full):
        _attn_softmax(s_ref, bias_ref, p_ref, l_ref, ntiles)

    def values(g, p_ref, l_ref, ntiles=full):
        _attn_values(p_ref, l_ref, v_ref, o_ref, row0(g), ntiles)

    first_full = full - 1
    for g in range(first_full):
        scores(g, s0, g + 1)
        softmax(s0, p0, l0, g + 1)
        values(g, p0, l0, g + 1)

    scores(first_full, s0)
    softmax(s0, p0, l0)
    scores(first_full + 1, s1)

    def pair(g, last):
        values(g, p0, l0)
        softmax(s1, p1, l1)
        if not last:
            scores(g + 2, s0)
        values(g + 1, p1, l1)
        if not last:
            softmax(s0, p0, l0)
            scores(g + 3, s1)

    n_pairs = (n_groups - first_full) // 2

    def body(k, carry):
        pair(first_full + 2 * k, last=False)
        return carry

    lax.fori_loop(0, n_pairs - 1, body, 0)
    pair(n_groups - 2, last=True)


def _attention(qkv, bias, batch, seq):
    t = batch * seq
    return pl.pallas_call(
        _attn_kernel,
        grid=(N_PAIRS, batch),
        in_specs=[
            pl.BlockSpec((1, seq, HEAD_PAIR), lambda hp, b: (hp, b, 0)),
            pl.BlockSpec((1, seq, HEAD_PAIR), lambda hp, b: (N_PAIRS + hp, b, 0)),
            pl.BlockSpec((1, seq, HEAD_PAIR), lambda hp, b: (2 * N_PAIRS + hp, b, 0)),
            pl.BlockSpec((2, GROUP, BAND), lambda hp, b: (hp, 0, 0)),
        ],
        out_specs=pl.BlockSpec((seq, HEAD_PAIR), lambda hp, b: (b, hp)),
        out_shape=jax.ShapeDtypeStruct((t, D_MODEL), BF16),
        scratch_shapes=[
            pltpu.VMEM((2 * GROUP, BAND), F32),
            pltpu.VMEM((2 * GROUP, BAND), F32),
            pltpu.VMEM((2 * GROUP, BAND), BF16),
            pltpu.VMEM((2 * GROUP, BAND), BF16),
            pltpu.VMEM((2 * GROUP, LANES), F32),
            pltpu.VMEM((2 * GROUP, LANES), F32),
        ],
        compiler_params=pltpu.CompilerParams(
            dimension_semantics=("parallel", "parallel"), vmem_limit_bytes=VMEM_LIMIT),
        name="band_attention",
    )(qkv, qkv, qkv, bias)


def _band_bias(rel_table):
    n_diag = GROUP + BAND - 1
    j = jnp.arange(n_diag)
    dist = (GROUP - 1 - j) + (BAND - GROUP)
    w = rel_table[:, jnp.clip(dist, -MAX_REL, MAX_REL) + MAX_REL]
    flat = jnp.tile(jnp.pad(w, ((0, 0), (0, 1))), (1, GROUP))[:, :GROUP * n_diag]
    return flat.reshape(-1, GROUP, n_diag)[:, :, GROUP - 1:GROUP - 1 + BAND]


def _gelu_tanh(x):
    k = math.sqrt(2.0 / math.pi)
    inner = x * (k + (k * 0.044715) * (x * x))
    return x * (0.5 + 0.5 * jnp.tanh(inner))


def _rglru_tile(xr, gr_ref, side_work, cw_ref, cb_ref, wr_ref, br_ref, wi_ref, bi_ref, lam_ref,
                o_ref, xs_ref, xc_ref, rp_ref, ip_ref, a_ref, b_ref, hl_ref, pp_ref, h_ref):
    tm = xr.shape[0]
    seg = tm // SUBLANES
    n_lane_tiles = D_MODEL // LANES
    tiles_per_slab = MXU_DIM // LANES
    lane_tile = lambda c: slice(c * LANES, (c + 1) * LANES)

    for c in range(n_lane_tiles):
        xs_ref[c, SUBLANES:, :] = xr[:, lane_tile(c)]

    for d in range(D_MODEL // MXU_DIM):
        for k in range(tiles_per_slab):
            c = d * tiles_per_slab + k
            for j in range(SUBLANES):
                acc = cb_ref[:, lane_tile(c)]
                for tap in range(CONV_W):
                    off = SUBLANES - (CONV_W - 1) + tap + j * seg
                    acc = acc + xs_ref[c, off:off + seg, :] * cw_ref[tap:tap + 1, lane_tile(c)]
                xc_ref[j * seg:(j + 1) * seg, lane_tile(k)] = acc
                side_work(COST_CONV)
            xs_ref[c, 0:SUBLANES, :] = xs_ref[c, tm:tm + SUBLANES, :]
        xcb = xc_ref[...].astype(BF16)
        rp_ref[...] = jnp.dot(xcb, wr_ref[d], preferred_element_type=F32)
        ip_ref[...] = jnp.dot(xcb, wi_ref[d], preferred_element_type=F32)
        for k in range(tiles_per_slab):
            c = d * tiles_per_slab + k
            nlam = -lam_ref[:, lane_tile(c)]
            softplus = jnp.maximum(nlam, 0.0) + jnp.log1p(jnp.exp(-jnp.abs(nlam)))
            log_a_scale = (-RG_C) * softplus
            for j in range(SUBLANES):
                rows = slice(j * seg, (j + 1) * seg)
                xc = xc_ref[rows, lane_tile(k)]
                r = jax.nn.sigmoid(rp_ref[rows, lane_tile(k)] + br_ref[:, lane_tile(c)])
                ig = jax.nn.sigmoid(ip_ref[rows, lane_tile(k)] + bi_ref[:, lane_tile(c)])
                log_a = r * log_a_scale
                a = jnp.exp(log_a)
                y = jnp.tanh(log_a) * (-1.0 - a * a)
                mult = jnp.where(y > 0.0, y * lax.rsqrt(y), 0.0)
                srows = slice(j * SEG_PITCH, j * SEG_PITCH + seg)
                a_ref[c, srows, :] = a
                b_ref[c, srows, :] = mult * (ig * xc)
                side_work(COST_GATE)

    hs = [jnp.zeros((SUBLANES, LANES), F32) for _ in range(n_lane_tiles)]
    ps = [jnp.ones((SUBLANES, LANES), F32) for _ in range(n_lane_tiles)]
    for t in range(seg):
        idx = pl.ds(t, SUBLANES, stride=SEG_PITCH)
        for c in range(n_lane_tiles):
            a_t = a_ref[c, idx, :]
            hs[c] = a_t * hs[c] + b_ref[c, idx, :]
            ps[c] = a_t * ps[c]
            hl_ref[c, idx, :] = hs[c]
            pp_ref[c, idx, :] = ps[c]
        side_work(COST_SCAN_STEP)
    h_end = jnp.concatenate(hs, axis=1)
    p_end = jnp.concatenate(ps, axis=1)

    h_in = h_ref[0:1, :]
    for j in range(SUBLANES):
        rows = slice(j * seg, (j + 1) * seg)
        srows = slice(j * SEG_PITCH, j * SEG_PITCH + seg)
        for c in range(n_lane_tiles):
            hj = hl_ref[c, srows, :] + pp_ref[c, srows, :] * h_in[:, lane_tile(c)]
            o_ref[rows, lane_tile(c)] = (hj * _gelu_tanh(gr_ref[rows, lane_tile(c)])).astype(BF16)
            side_work(COST_OUT)
        h_in = p_end[j:j + 1, :] * h_in + h_end[j:j + 1, :]
    h_ref[0:1, :] = h_in


def _block_diag_tiles(w):
    per = MXU_DIM // RG_BLOCK
    n_diag = D_MODEL // MXU_DIM
    w4 = w.reshape(n_diag, per, RG_BLOCK, RG_BLOCK)
    eye = jnp.eye(per, dtype=w.dtype)
    tiles = jnp.einsum('dpij,pq->dpiqj', w4, eye)
    return tiles.reshape(n_diag, MXU_DIM, MXU_DIM)


def _merge_kernel(x_ref, at_ref, hb_ref, ga_ref, gb_ref, bm_ref, wa_ref, wb_ref, wo_ref, h_ref):
    ya = jnp.dot(at_ref[...], wa_ref[...], preferred_element_type=F32)
    yb = jnp.dot(hb_ref[...], wb_ref[...], preferred_element_type=F32)
    g_a = jax.nn.sigmoid(ga_ref[...].astype(F32) + bm_ref[:, :D_MODEL])
    g_b = jax.nn.sigmoid(gb_ref[...].astype(F32) + bm_ref[:, D_MODEL:])
    mix = (g_a * ya + g_b * yb).astype(BF16)
    h_ref[...] = x_ref[...] + jnp.dot(mix, wo_ref[...], preferred_element_type=F32)


def _merge(x2, attn, hb, gates, bm, wa, wb, wo, tm):
    t = x2.shape[0]
    row = lambda i: (i, 0)
    return pl.pallas_call(
        _merge_kernel,
        grid=(t // tm,),
        in_specs=[
            pl.BlockSpec((tm, D_MODEL), row),
            pl.BlockSpec((tm, D_MODEL), row),
            pl.BlockSpec((tm, D_MODEL), row),
            pl.BlockSpec((tm, D_MODEL), lambda i: (i, 0)),
            pl.BlockSpec((tm, D_MODEL), lambda i: (i, 1)),
            _resident((1, 2 * D_MODEL)),
            _resident((D_MODEL, D_MODEL)),
            _resident((D_MODEL, D_MODEL)),
            _resident((D_MODEL, D_MODEL)),
        ],
        out_specs=pl.BlockSpec((tm, D_MODEL), row),
        out_shape=jax.ShapeDtypeStruct((t, D_MODEL), F32),
        compiler_params=pltpu.CompilerParams(
            dimension_semantics=("parallel",), vmem_limit_bytes=VMEM_LIMIT),
        name="merge",
    )(x2, attn, hb, gates, gates, bm, wa, wb, wo)


def _ffn_kernel(h_ref, gn_ref, win_ref, wout_ref, gf_ref, o_ref, *, splits):
    h = h_ref[...]
    hn = _rms(h, gn_ref[...]).astype(BF16)
    d_ff = wout_ref.shape[0]
    acc = h
    for c0, c1 in splits:
        g = jnp.dot(hn, win_ref[:, c0:c1], preferred_element_type=F32)
        up = jnp.dot(hn, win_ref[:, d_ff + c0:d_ff + c1], preferred_element_type=F32)
        act = (jax.nn.silu(g) * up).astype(BF16)
        acc = acc + jnp.dot(act, wout_ref[c0:c1, :], preferred_element_type=F32)
    o_ref[...] = _rms(acc, gf_ref[...])


def _ffn(h, gn, win, wout, gf, tm):
    t = h.shape[0]
    d_ff = wout.shape[0]
    n_tiles = d_ff // MXU_DIM
    half = (n_tiles + 1) // 2 * MXU_DIM
    splits = ((0, half), (half, d_ff))
    row = lambda i: (i, 0)
    return pl.pallas_call(
        functools.partial(_ffn_kernel, splits=splits),
        grid=(t // tm,),
        in_specs=[
            pl.BlockSpec((tm, D_MODEL), row),
            _resident((1, D_MODEL)),
            _resident((D_MODEL, 2 * d_ff)),
            _resident((d_ff, D_MODEL)),
            _resident((1, D_MODEL)),
        ],
        out_specs=pl.BlockSpec((tm, D_MODEL), row),
        out_shape=jax.ShapeDtypeStruct((t, D_MODEL), F32),
        compiler_params=pltpu.CompilerParams(
            dimension_semantics=("parallel",), vmem_limit_bytes=VMEM_LIMIT),
        name="ffn",
    )(h, gn, win, wout, gf)


def kernel(x, norm_mix_g, w_in, b_merge, rel_table, w_attn_out, conv_w, conv_b, w_rg_r, b_rg_r,
           w_rg_i, b_rg_i, rg_lambda, w_rnn_out, w_o, norm_ffn_g, w_ffn_in, w_ffn_out, final_norm_g):
    batch, seq, d = x.shape
    assert d == D_MODEL and seq % GROUP == 0 and seq // GROUP >= BAND_TILES
    assert w_in.shape[0] == 1, "single-layer block"
    x2 = x.reshape(batch * seq, d)
    row = lambda v: v.reshape(1, -1)

    assert seq % RG_TILE == 0
    qkv, gates, hb = _inproj(
        x2, row(norm_mix_g[0]), w_in[0].astype(BF16), conv_w[0], row(conv_b[0]),
        _block_diag_tiles(w_rg_r[0]).astype(BF16), row(b_rg_r[0]),
        _block_diag_tiles(w_rg_i[0]).astype(BF16), row(b_rg_i[0]), row(rg_lambda[0]), seq)
    attn = _attention(qkv, _band_bias(rel_table[0]), batch, seq)
    h = _merge(x2, attn, hb, gates, row(b_merge[0]), w_attn_out[0].astype(BF16),
               w_rnn_out[0].astype(BF16), w_o[0].astype(BF16), tm=512)
    out = _ffn(h, row(norm_ffn_g[0]), w_ffn_in[0].astype(BF16), w_ffn_out[0].astype(BF16),
               row(final_norm_g), tm=512)
    return out.reshape(batch, seq, d)
```

```python
import functools
import math

import jax
import jax.numpy as jnp
from jax import lax
from jax.experimental import pallas as pl
from jax.experimental.pallas import tpu as pltpu

F32 = jnp.float32
BF16 = jnp.bfloat16

D_MODEL = 1024
CHUNK = 64
LEFT_CHUNKS = 8
N_HEADS = 16
HEAD_DIM = 64
MAX_REL = 128
N_RG_BLOCKS = 16
RG_BLOCK = D_MODEL // N_RG_BLOCKS
CONV_W = 4
RG_C = 8.0
EPS = 1e-6
NEG_INF = -1e30

LANES = 128
SUBLANES = 8
MXU_DIM = 256
HEAD_PAIR = 2 * HEAD_DIM
N_PAIRS = N_HEADS // 2
GROUP = 4 * CHUNK
BAND_TILES = LEFT_CHUNKS * CHUNK // GROUP + 1
BAND = BAND_TILES * GROUP
DIAG_LANES = -(-(GROUP + BAND - 1) // LANES) * LANES
VMEM_LIMIT = 56 * 1024 * 1024
RG_TILE = 512
SEG_PITCH = RG_TILE // SUBLANES + SUBLANES // 2
COST_CONV, COST_GATE, COST_SCAN_STEP, COST_OUT = 16, 52, 23, 24


def _rms(x, g):
    var = jnp.mean(x * x, axis=-1, keepdims=True)
    return x * lax.rsqrt(var + EPS) * g


def _resident(shape):
    zeros = (0,) * len(shape)
    return pl.BlockSpec(shape, lambda *_: zeros, pipeline_mode=pl.Buffered(1))


def _inproj_kernel(x_ref, g_ref, w_ref, cw_ref, cb_ref, wr_ref, br_ref, wi_ref, bi_ref, lam_ref,
                   qkv_ref, gates_ref, hb_ref, xn_ref, gr_ref, xs_ref, xc_ref, rp_ref, ip_ref,
                   a_ref, b_ref, hl_ref, pp_ref, h_ref, *, tiles_per_seq):
    @pl.when(pl.program_id(0) % tiles_per_seq == 0)
    def _():
        xs_ref[:, 0:SUBLANES, :] = jnp.zeros((D_MODEL // LANES, SUBLANES, LANES), F32)
        h_ref[...] = jnp.zeros_like(h_ref)

    xn_ref[...] = _rms(x_ref[...], g_ref[...]).astype(BF16)

    def u_cols(c0, width):
        return jnp.dot(xn_ref[...], w_ref[:, c0:c0 + width], preferred_element_type=F32)

    def unit(j, n):
        acc = u_cols(j * D_MODEL + n * MXU_DIM, MXU_DIM)
        if j < 3:
            for k in range(MXU_DIM // HEAD_PAIR):
                hp = n * (MXU_DIM // HEAD_PAIR) + k
                qkv_ref[j * N_PAIRS + hp] = acc[:, k * HEAD_PAIR:(k + 1) * HEAD_PAIR].astype(BF16)
        elif j == 4:
            gr_ref[:, n * MXU_DIM:(n + 1) * MXU_DIM] = acc
        else:
            c0 = (j - 5) * D_MODEL + n * MXU_DIM
            gates_ref[:, c0:c0 + MXU_DIM] = acc.astype(BF16)

    units = [(j, n) for j in (4, 0, 1, 2, 5, 6) for n in range(D_MODEL // MXU_DIM)]
    n_pieces = SUBLANES * (D_MODEL // LANES)
    total_cost = n_pieces * (COST_CONV + COST_GATE + COST_OUT) + (x_ref.shape[0] // SUBLANES) * COST_SCAN_STEP
    cost_per_unit = total_cost / len(units)
    spent = [0.0]

    def side_work(cost):
        spent[0] += cost
        while units and spent[0] >= cost_per_unit:
            spent[0] -= cost_per_unit
            unit(*units.pop(0))

    _rglru_tile(u_cols(3 * D_MODEL, D_MODEL), gr_ref, side_work,
                cw_ref, cb_ref, wr_ref, br_ref, wi_ref, bi_ref, lam_ref,
                hb_ref, xs_ref, xc_ref, rp_ref, ip_ref, a_ref, b_ref, hl_ref, pp_ref, h_ref)
    while units:
        unit(*units.pop(0))


def _inproj(x2, g, w, cw, cb, wr, br, wi, bi, lam, seq):
    t = x2.shape[0]
    tm = RG_TILE
    n_diag = D_MODEL // MXU_DIM
    scan_scratch = pltpu.VMEM((D_MODEL // LANES, SUBLANES * SEG_PITCH, LANES), F32)
    slab_scratch = pltpu.VMEM((tm, MXU_DIM), F32)
    return pl.pallas_call(
        functools.partial(_inproj_kernel, tiles_per_seq=seq // tm),
        grid=(t // tm,),
        in_specs=[
            pl.BlockSpec((tm, D_MODEL), lambda i: (i, 0)),
            _resident((1, D_MODEL)),
            _resident((D_MODEL, 7 * D_MODEL)),
            _resident((CONV_W, D_MODEL)),
            _resident((1, D_MODEL)),
            _resident((n_diag, MXU_DIM, MXU_DIM)),
            _resident((1, D_MODEL)),
            _resident((n_diag, MXU_DIM, MXU_DIM)),
            _resident((1, D_MODEL)),
            _resident((1, D_MODEL)),
        ],
        out_specs=[
            pl.BlockSpec((3 * N_PAIRS, tm, HEAD_PAIR), lambda i: (0, i, 0)),
            pl.BlockSpec((tm, 2 * D_MODEL), lambda i: (i, 0)),
            pl.BlockSpec((tm, D_MODEL), lambda i: (i, 0)),
        ],
        out_shape=[
            jax.ShapeDtypeStruct((3 * N_PAIRS, t, HEAD_PAIR), BF16),
            jax.ShapeDtypeStruct((t, 2 * D_MODEL), BF16),
            jax.ShapeDtypeStruct((t, D_MODEL), BF16),
        ],
        scratch_shapes=[
            pltpu.VMEM((tm, D_MODEL), BF16),
            pltpu.VMEM((tm, D_MODEL), F32),
            pltpu.VMEM((D_MODEL // LANES, tm + SUBLANES, LANES), F32),
            slab_scratch, slab_scratch, slab_scratch,
            scan_scratch, scan_scratch, scan_scratch, scan_scratch,
            pltpu.VMEM((SUBLANES, D_MODEL), F32),
        ],
        compiler_params=pltpu.CompilerParams(
            dimension_semantics=("arbitrary",), vmem_limit_bytes=VMEM_LIMIT),
        name="inproj_rglru",
    )(x2, g, w, cw, cb, wr, br, wi, bi, lam)


def _band_geometry(ntiles):
    kw = ntiles * GROUP
    return kw, BAND - kw


def _attn_scores(q_ref, k_ref, s_ref, r0, ntiles):
    kw, c_off = _band_geometry(ntiles)
    lane = lax.broadcasted_iota(jnp.int32, (GROUP, HEAD_PAIR), 1)
    first_head = lane < HEAD_DIM
    qp = q_ref[0, pl.ds(r0, GROUP), :] * jnp.asarray(1.0 / math.sqrt(HEAD_DIM), BF16)
    zero = jnp.zeros_like(qp)
    qq = jnp.concatenate([jnp.where(first_head, qp, zero),
                          jnp.where(first_head, zero, qp)], axis=0)
    kk = k_ref[0, pl.ds(r0 + GROUP - kw, kw), :]
    s_ref[:, c_off:] = lax.dot_general(qq, kk, (((1,), (1,)), ((), ())),
                                       preferred_element_type=F32)


def _attn_softmax(s_ref, bias_ref, p_ref, l_ref, ntiles):
    _, c_off = _band_geometry(ntiles)
    rb = 32
    lane_rb = lax.broadcasted_iota(jnp.int32, (rb, LANES), 1)
    for hh in range(2):
        for i in range(GROUP // CHUNK):
            lo_col = max(i * CHUNK, c_off)
            hi_col = i * CHUNK + (LEFT_CHUNKS + 1) * CHUNK
            ws = (lo_col // LANES) * LANES
            we = -(-hi_col // LANES) * LANES
            nv = (we - ws) // LANES
            rows_all = slice(hh * GROUP + i * CHUNK, hh * GROUP + (i + 1) * CHUNK)
            if ws > c_off:
                p_ref[rows_all, c_off:ws] = jnp.zeros((CHUNK, ws - c_off), BF16)
            if we < BAND:
                p_ref[rows_all, we:BAND] = jnp.zeros((CHUNK, BAND - we), BF16)
            for b in range(CHUNK // rb):
                rows = slice(hh * GROUP + i * CHUNK + b * rb, hh * GROUP + i * CHUNK + (b + 1) * rb)
                brows = slice(i * CHUNK + b * rb, i * CHUNK + (b + 1) * rb)
                x = s_ref[rows, ws:we] + bias_ref[hh, brows, ws:we]
                cols = [x[:, c * LANES:(c + 1) * LANES] for c in range(nv)]
                if lo_col % LANES:
                    cols[0] = jnp.where(lane_rb >= HEAD_DIM, cols[0], NEG_INF)
                if hi_col % LANES:
                    cols[-1] = jnp.where(lane_rb < HEAD_DIM, cols[-1], NEG_INF)
                x = jnp.concatenate(cols, axis=1)
                m = jnp.max(x, axis=-1, keepdims=True)
                e = jnp.exp(x - m)
                l = jnp.sum(e, axis=-1, keepdims=True)
                p_ref[rows, ws:we] = e.astype(BF16)
                l_ref[rows, :] = jnp.broadcast_to(l, (rb, LANES))

def _attn_values(p_ref, l_ref, v_ref, o_ref, r0, ntiles):
    kw, c_off = _band_geometry(ntiles)
    lane = lax.broadcasted_iota(jnp.int32, (GROUP, HEAD_PAIR), 1)
    vv = v_ref[0, pl.ds(r0 + GROUP - kw, kw), :]
    o2 = jnp.dot(p_ref[:, c_off:], vv, preferred_element_type=F32)
    o2 = o2 / l_ref[...]
    o = jnp.where(lane < HEAD_DIM, o2[:GROUP], o2[GROUP:])
    o_ref[pl.ds(r0, GROUP), :] = o.astype(BF16)


def _attn_kernel(q_ref, k_ref, v_ref, w_ref, o_ref, bias_ref, s0, s1, p0, p1, l0, l1):
    n_groups = q_ref.shape[1] // GROUP
    full = BAND_TILES
    row0 = lambda g: pl.multiple_of(g * GROUP, GROUP)

    @pl.when(pl.program_id(1) == 0)
    def _():
        for hh in range(2):
            diag = jnp.broadcast_to(w_ref[hh], (GROUP, DIAG_LANES))
            rolled = pltpu.roll(diag, DIAG_LANES - (GROUP - 1), 1, stride=1, stride_axis=0)
            bias_ref[hh] = rolled[:, :BAND]

    def scores(g, s_ref, ntiles=full):
        _attn_scores(q_ref, k_ref, s_ref, row0(g), ntiles)

    def softmax(s_ref, p_ref, l_ref, ntiles=full):
        _attn_softmax(s_ref, bias_ref, p_ref, l_ref, ntiles)

    def values(g, p_ref, l_ref, ntiles=full):
        _attn_values(p_ref, l_ref, v_ref, o_ref, row0(g), ntiles)

    first_full = full - 1
    for g in range(first_full):
        scores(g, s0, g + 1)
        softmax(s0, p0, l0, g + 1)
        values(g, p0, l0, g + 1)

    scores(first_full, s0)
    softmax(s0, p0, l0)
    scores(first_full + 1, s1)

    def pair(g, last):
        values(g, p0, l0)
        softmax(s1, p1, l1)
        if not last:
            scores(g + 2, s0)
        values(g + 1, p1, l1)
        if not last:
            softmax(s0, p0, l0)
            scores(g + 3, s1)

    n_pairs = (n_groups - first_full) // 2

    def body(k, carry):
        pair(first_full + 2 * k, last=False)
        return carry

    lax.fori_loop(0, n_pairs - 1, body, 0)
    pair(n_groups - 2, last=True)


def _attention(qkv, diagonals, batch, seq):
    t = batch * seq
    return pl.pallas_call(
        _attn_kernel,
        grid=(N_PAIRS, batch),
        in_specs=[
            pl.BlockSpec((1, seq, HEAD_PAIR), lambda hp, b: (hp, b, 0)),
            pl.BlockSpec((1, seq, HEAD_PAIR), lambda hp, b: (N_PAIRS + hp, b, 0)),
            pl.BlockSpec((1, seq, HEAD_PAIR), lambda hp, b: (2 * N_PAIRS + hp, b, 0)),
            pl.BlockSpec((2, 1, DIAG_LANES), lambda hp, b: (hp, 0, 0)),
        ],
        out_specs=pl.BlockSpec((seq, HEAD_PAIR), lambda hp, b: (b, hp)),
        out_shape=jax.ShapeDtypeStruct((t, D_MODEL), BF16),
        scratch_shapes=[
            pltpu.VMEM((2, GROUP, BAND), F32),
            pltpu.VMEM((2 * GROUP, BAND), F32),
            pltpu.VMEM((2 * GROUP, BAND), F32),
            pltpu.VMEM((2 * GROUP, BAND), BF16),
            pltpu.VMEM((2 * GROUP, BAND), BF16),
            pltpu.VMEM((2 * GROUP, LANES), F32),
            pltpu.VMEM((2 * GROUP, LANES), F32),
        ],
        compiler_params=pltpu.CompilerParams(
            dimension_semantics=("parallel", "arbitrary"), vmem_limit_bytes=VMEM_LIMIT),
        name="band_attention",
    )(qkv, qkv, qkv, diagonals)


def _band_bias_diagonals(rel_table):
    j = jnp.arange(DIAG_LANES)
    dist = (GROUP - 1 - j) + (BAND - GROUP)
    w = rel_table[:, jnp.clip(dist, -MAX_REL, MAX_REL) + MAX_REL]
    return w[:, None, :]


def _gelu_tanh(x):
    k = math.sqrt(2.0 / math.pi)
    inner = x * (k + (k * 0.044715) * (x * x))
    return x * (0.5 + 0.5 * jnp.tanh(inner))


def _rglru_tile(xr, gr_ref, side_work, cw_ref, cb_ref, wr_ref, br_ref, wi_ref, bi_ref, lam_ref,
                o_ref, xs_ref, xc_ref, rp_ref, ip_ref, a_ref, b_ref, hl_ref, pp_ref, h_ref):
    tm = xr.shape[0]
    seg = tm // SUBLANES
    n_lane_tiles = D_MODEL // LANES
    tiles_per_slab = MXU_DIM // LANES
    lane_tile = lambda c: slice(c * LANES, (c + 1) * LANES)

    for c in range(n_lane_tiles):
        xs_ref[c, SUBLANES:, :] = xr[:, lane_tile(c)]

    for d in range(D_MODEL // MXU_DIM):
        for k in range(tiles_per_slab):
            c = d * tiles_per_slab + k
            for j in range(SUBLANES):
                acc = cb_ref[:, lane_tile(c)]
                for tap in range(CONV_W):
                    off = SUBLANES - (CONV_W - 1) + tap + j * seg
                    acc = acc + xs_ref[c, off:off + seg, :] * cw_ref[tap:tap + 1, lane_tile(c)]
                xc_ref[j * seg:(j + 1) * seg, lane_tile(k)] = acc
                side_work(COST_CONV)
            xs_ref[c, 0:SUBLANES, :] = xs_ref[c, tm:tm + SUBLANES, :]
        xcb = xc_ref[...].astype(BF16)
        rp_ref[...] = jnp.dot(xcb, wr_ref[d], preferred_element_type=F32)
        ip_ref[...] = jnp.dot(xcb, wi_ref[d], preferred_element_type=F32)
        for k in range(tiles_per_slab):
            c = d * tiles_per_slab + k
            nlam = -lam_ref[:, lane_tile(c)]
            softplus = jnp.maximum(nlam, 0.0) + jnp.log1p(jnp.exp(-jnp.abs(nlam)))
            log_a_scale = (-RG_C) * softplus
            for j in range(SUBLANES):
                rows = slice(j * seg, (j + 1) * seg)
                xc = xc_ref[rows, lane_tile(k)]
                r = jax.nn.sigmoid(rp_ref[rows, lane_tile(k)] + br_ref[:, lane_tile(c)])
                ig = jax.nn.sigmoid(ip_ref[rows, lane_tile(k)] + bi_ref[:, lane_tile(c)])
                log_a = r * log_a_scale
                a = jnp.exp(log_a)
                y = jnp.tanh(log_a) * (-1.0 - a * a)
                mult = jnp.where(y > 0.0, y * lax.rsqrt(y), 0.0)
                srows = slice(j * SEG_PITCH, j * SEG_PITCH + seg)
                a_ref[c, srows, :] = a
                b_ref[c, srows, :] = mult * (ig * xc)
                side_work(COST_GATE)

    hs = [jnp.zeros((SUBLANES, LANES), F32) for _ in range(n_lane_tiles)]
    ps = [jnp.ones((SUBLANES, LANES), F32) for _ in range(n_lane_tiles)]
    for t in range(seg):
        idx = pl.ds(t, SUBLANES, stride=SEG_PITCH)
        for c in range(n_lane_tiles):
            a_t = a_ref[c, idx, :]
            hs[c] = a_t * hs[c] + b_ref[c, idx, :]
            ps[c] = a_t * ps[c]
            hl_ref[c, idx, :] = hs[c]
            pp_ref[c, idx, :] = ps[c]
        side_work(COST_SCAN_STEP)
    h_end = jnp.concatenate(hs, axis=1)
    p_end = jnp.concatenate(ps, axis=1)

    h_in = h_ref[0:1, :]
    for j in range(SUBLANES):
        rows = slice(j * seg, (j + 1) * seg)
        srows = slice(j * SEG_PITCH, j * SEG_PITCH + seg)
        for c in range(n_lane_tiles):
            hj = hl_ref[c, srows, :] + pp_ref[c, srows, :] * h_in[:, lane_tile(c)]
            o_ref[rows, lane_tile(c)] = (hj * _gelu_tanh(gr_ref[rows, lane_tile(c)])).astype(BF16)
            side_work(COST_OUT)
        h_in = p_end[j:j + 1, :] * h_in + h_end[j:j + 1, :]
    h_ref[0:1, :] = h_in


def _block_diag_tiles(w):
    per = MXU_DIM // RG_BLOCK
    n_diag = D_MODEL // MXU_DIM
    w4 = w.reshape(n_diag, per, RG_BLOCK, RG_BLOCK)
    eye = jnp.eye(per, dtype=w.dtype)
    tiles = jnp.einsum('dpij,pq->dpiqj', w4, eye)
    return tiles.reshape(n_diag, MXU_DIM, MXU_DIM)


def _merge_kernel(x_ref, at_ref, hb_ref, ga_ref, gb_ref, bm_ref, wa_ref, wb_ref, wo_ref, h_ref):
    ya = jnp.dot(at_ref[...], wa_ref[...], preferred_element_type=F32)
    yb = jnp.dot(hb_ref[...], wb_ref[...], preferred_element_type=F32)
    g_a = jax.nn.sigmoid(ga_ref[...].astype(F32) + bm_ref[:, :D_MODEL])
    g_b = jax.nn.sigmoid(gb_ref[...].astype(F32) + bm_ref[:, D_MODEL:])
    mix = (g_a * ya + g_b * yb).astype(BF16)
    h_ref[...] = x_ref[...] + jnp.dot(mix, wo_ref[...], preferred_element_type=F32)


def _merge(x2, attn, hb, gates, bm, wa, wb, wo, tm):
    t = x2.shape[0]
    row = lambda i: (i, 0)
    return pl.pallas_call(
        _merge_kernel,
        grid=(t // tm,),
        in_specs=[
            pl.BlockSpec((tm, D_MODEL), row),
            pl.BlockSpec((tm, D_MODEL), row),
            pl.BlockSpec((tm, D_MODEL), row),
            pl.BlockSpec((tm, D_MODEL), lambda i: (i, 0)),
            pl.BlockSpec((tm, D_MODEL), lambda i: (i, 1)),
            _resident((1, 2 * D_MODEL)),
            _resident((D_MODEL, D_MODEL)),
            _resident((D_MODEL, D_MODEL)),
            _resident((D_MODEL, D_MODEL)),
        ],
        out_specs=pl.BlockSpec((tm, D_MODEL), row),
        out_shape=jax.ShapeDtypeStruct((t, D_MODEL), F32),
        compiler_params=pltpu.CompilerParams(
            dimension_semantics=("parallel",), vmem_limit_bytes=VMEM_LIMIT),
        name="merge",
    )(x2, attn, hb, gates, gates, bm, wa, wb, wo)


def _ffn_kernel(h_ref, gn_ref, win_ref, wout_ref, gf_ref, o_ref, *, splits):
    h = h_ref[...]
    hn = _rms(h, gn_ref[...]).astype(BF16)
    d_ff = wout_ref.shape[0]
    acc = h
    for c0, c1 in splits:
        g = jnp.dot(hn, win_ref[:, c0:c1], preferred_element_type=F32)
        up = jnp.dot(hn, win_ref[:, d_ff + c0:d_ff + c1], preferred_element_type=F32)
        act = (jax.nn.silu(g) * up).astype(BF16)
        acc = acc + jnp.dot(act, wout_ref[c0:c1, :], preferred_element_type=F32)
    o_ref[...] = _rms(acc, gf_ref[...])


def _ffn(h, gn, win, wout, gf, tm):
    t = h.shape[0]
    d_ff = wout.shape[0]
    n_tiles = d_ff // MXU_DIM
    half = (n_tiles + 1) // 2 * MXU_DIM
    splits = ((0, half), (half, d_ff))
    row = lambda i: (i, 0)
    return pl.pallas_call(
        functools.partial(_ffn_kernel, splits=splits),
        grid=(t // tm,),
        in_specs=[
            pl.BlockSpec((tm, D_MODEL), row),
            _resident((1, D_MODEL)),
            _resident((D_MODEL, 2 * d_ff)),
            _resident((d_ff, D_MODEL)),
            _resident((1, D_MODEL)),
        ],
        out_specs=pl.BlockSpec((tm, D_MODEL), row),
        out_shape=jax.ShapeDtypeStruct((t, D_MODEL), F32),
        compiler_params=pltpu.CompilerParams(
            dimension_semantics=("parallel",), vmem_limit_bytes=VMEM_LIMIT),
        name="ffn",
    )(h, gn, win, wout, gf)


def kernel(x, norm_mix_g, w_in, b_merge, rel_table, w_attn_out, conv_w, conv_b, w_rg_r, b_rg_r,
           w_rg_i, b_rg_i, rg_lambda, w_rnn_out, w_o, norm_ffn_g, w_ffn_in, w_ffn_out, final_norm_g):
    batch, seq, d = x.shape
    assert d == D_MODEL and seq % GROUP == 0 and seq // GROUP >= BAND_TILES
    assert w_in.shape[0] == 1, "single-layer block"
    x2 = x.reshape(batch * seq, d)
    row = lambda v: v.reshape(1, -1)

    assert seq % RG_TILE == 0
    qkv, gates, hb = _inproj(
        x2, row(norm_mix_g[0]), w_in[0].astype(BF16), conv_w[0], row(conv_b[0]),
        _block_diag_tiles(w_rg_r[0]).astype(BF16), row(b_rg_r[0]),
        _block_diag_tiles(w_rg_i[0]).astype(BF16), row(b_rg_i[0]), row(rg_lambda[0]), seq)
    attn = _attention(qkv, _band_bias_diagonals(rel_table[0]), batch, seq)
    h = _merge(x2, attn, hb, gates, row(b_merge[0]), w_attn_out[0], w_rnn_out[0], w_o[0], tm=512)
    out = _ffn(h, row(norm_ffn_g[0]), w_ffn_in[0], w_ffn_out[0], row(final_norm_g), tm=512)
    return out.reshape(batch, seq, d)
```

```python
import functools
import math

import jax
import jax.numpy as jnp
from jax import lax
from jax.experimental import pallas as pl
from jax.experimental.pallas import tpu as pltpu

F32 = jnp.float32
BF16 = jnp.bfloat16

D_MODEL = 1024
CHUNK = 64
LEFT_CHUNKS = 8
N_HEADS = 16
HEAD_DIM = 64
MAX_REL = 128
N_RG_BLOCKS = 16
RG_BLOCK = D_MODEL // N_RG_BLOCKS
CONV_W = 4
RG_C = 8.0
EPS = 1e-6
NEG_INF = -1e30

LANES = 128
SUBLANES = 8
MXU_DIM = 256
HEAD_PAIR = 2 * HEAD_DIM
N_PAIRS = N_HEADS // 2
GROUP = 4 * CHUNK
BAND_TILES = LEFT_CHUNKS * CHUNK // GROUP + 1
BAND = BAND_TILES * GROUP
DIAG_LANES = -(-(GROUP + BAND - 1) // LANES) * LANES
VMEM_LIMIT = 56 * 1024 * 1024
RG_TILE = 512
SEG_PITCH = RG_TILE // SUBLANES + SUBLANES // 2
COST_CONV, COST_GATE, COST_SCAN_STEP, COST_OUT = 16, 52, 23, 24


def _rms(x, g):
    var = jnp.mean(x * x, axis=-1, keepdims=True)
    return x * lax.rsqrt(var + EPS) * g


def _resident(shape):
    zeros = (0,) * len(shape)
    return pl.BlockSpec(shape, lambda *_: zeros, pipeline_mode=pl.Buffered(1))


def _inproj_kernel(x_ref, g_ref, w_ref, cw_ref, cb_ref, wr_ref, br_ref, wi_ref, bi_ref, lam_ref,
                   qkv_ref, gates_ref, hb_ref, xn_ref, gr_ref, xs_ref, xc_ref, rp_ref, ip_ref,
                   a_ref, b_ref, hl_ref, pp_ref, h_ref, *, tiles_per_seq):
    @pl.when(pl.program_id(0) % tiles_per_seq == 0)
    def _():
        xs_ref[:, 0:SUBLANES, :] = jnp.zeros((D_MODEL // LANES, SUBLANES, LANES), F32)
        h_ref[...] = jnp.zeros_like(h_ref)

    xn_ref[...] = _rms(x_ref[...], g_ref[...]).astype(BF16)

    def u_cols(c0, width):
        return jnp.dot(xn_ref[...], w_ref[:, c0:c0 + width], preferred_element_type=F32)

    def unit(j, n):
        acc = u_cols(j * D_MODEL + n * MXU_DIM, MXU_DIM)
        if j < 3:
            for k in range(MXU_DIM // HEAD_PAIR):
                hp = n * (MXU_DIM // HEAD_PAIR) + k
                qkv_ref[j * N_PAIRS + hp] = acc[:, k * HEAD_PAIR:(k + 1) * HEAD_PAIR].astype(BF16)
        elif j == 4:
            gr_ref[:, n * MXU_DIM:(n + 1) * MXU_DIM] = acc
        else:
            c0 = (j - 5) * D_MODEL + n * MXU_DIM
            gates_ref[:, c0:c0 + MXU_DIM] = acc.astype(BF16)

    units = [(j, n) for j in (4, 0, 1, 2, 5, 6) for n in range(D_MODEL // MXU_DIM)]
    n_pieces = SUBLANES * (D_MODEL // LANES)
    total_cost = n_pieces * (COST_CONV + COST_GATE + COST_OUT) + (x_ref.shape[0] // SUBLANES) * COST_SCAN_STEP
    cost_per_unit = total_cost / len(units)
    spent = [0.0]

    def side_work(cost):
        spent[0] += cost
        while units and spent[0] >= cost_per_unit:
            spent[0] -= cost_per_unit
            unit(*units.pop(0))

    _rglru_tile(u_cols(3 * D_MODEL, D_MODEL), gr_ref, side_work,
                cw_ref, cb_ref, wr_ref, br_ref, wi_ref, bi_ref, lam_ref,
                hb_ref, xs_ref, xc_ref, rp_ref, ip_ref, a_ref, b_ref, hl_ref, pp_ref, h_ref)
    while units:
        unit(*units.pop(0))


def _inproj(x2, g, w, cw, cb, wr, br, wi, bi, lam, seq):
    t = x2.shape[0]
    tm = RG_TILE
    n_diag = D_MODEL // MXU_DIM
    scan_scratch = pltpu.VMEM((D_MODEL // LANES, SUBLANES * SEG_PITCH, LANES), F32)
    slab_scratch = pltpu.VMEM((tm, MXU_DIM), F32)
    return pl.pallas_call(
        functools.partial(_inproj_kernel, tiles_per_seq=seq // tm),
        grid=(t // tm,),
        in_specs=[
            pl.BlockSpec((tm, D_MODEL), lambda i: (i, 0)),
            _resident((1, D_MODEL)),
            _resident((D_MODEL, 7 * D_MODEL)),
            _resident((CONV_W, D_MODEL)),
            _resident((1, D_MODEL)),
            _resident((n_diag, MXU_DIM, MXU_DIM)),
            _resident((1, D_MODEL)),
            _resident((n_diag, MXU_DIM, MXU_DIM)),
            _resident((1, D_MODEL)),
            _resident((1, D_MODEL)),
        ],
        out_specs=[
            pl.BlockSpec((3 * N_PAIRS, tm, HEAD_PAIR), lambda i: (0, i, 0)),
            pl.BlockSpec((tm, 2 * D_MODEL), lambda i: (i, 0)),
            pl.BlockSpec((tm, D_MODEL), lambda i: (i, 0)),
        ],
        out_shape=[
            jax.ShapeDtypeStruct((3 * N_PAIRS, t, HEAD_PAIR), BF16),
            jax.ShapeDtypeStruct((t, 2 * D_MODEL), BF16),
            jax.ShapeDtypeStruct((t, D_MODEL), BF16),
        ],
        scratch_shapes=[
            pltpu.VMEM((tm, D_MODEL), BF16),
            pltpu.VMEM((tm, D_MODEL), F32),
            pltpu.VMEM((D_MODEL // LANES, tm + SUBLANES, LANES), F32),
            slab_scratch, slab_scratch, slab_scratch,
            scan_scratch, scan_scratch, scan_scratch, scan_scratch,
            pltpu.VMEM((SUBLANES, D_MODEL), F32),
        ],
        compiler_params=pltpu.CompilerParams(
            dimension_semantics=("arbitrary",), vmem_limit_bytes=VMEM_LIMIT),
        name="inproj_rglru",
    )(x2, g, w, cw, cb, wr, br, wi, bi, lam)


def _band_geometry(ntiles):
    kw = ntiles * GROUP
    return kw, BAND - kw


def _attn_scores(q_ref, k_ref, s_ref, r0, ntiles):
    kw, c_off = _band_geometry(ntiles)
    lane = lax.broadcasted_iota(jnp.int32, (GROUP, HEAD_PAIR), 1)
    first_head = lane < HEAD_DIM
    qp = q_ref[0, pl.ds(r0, GROUP), :] * jnp.asarray(1.0 / math.sqrt(HEAD_DIM), BF16)
    zero = jnp.zeros_like(qp)
    qq = jnp.concatenate([jnp.where(first_head, qp, zero),
                          jnp.where(first_head, zero, qp)], axis=0)
    kk = k_ref[0, pl.ds(r0 + GROUP - kw, kw), :]
    s_ref[:, c_off:] = lax.dot_general(qq, kk, (((1,), (1,)), ((), ())),
                                       preferred_element_type=F32)


def _attn_softmax(s_ref, bias_ref, p_ref, l_ref, ntiles):
    _, c_off = _band_geometry(ntiles)
    rb = 32
    lane_rb = lax.broadcasted_iota(jnp.int32, (rb, LANES), 1)
    for hh in range(2):
        for i in range(GROUP // CHUNK):
            lo_col = max(i * CHUNK, c_off)
            hi_col = i * CHUNK + (LEFT_CHUNKS + 1) * CHUNK
            ws = (lo_col // LANES) * LANES
            we = -(-hi_col // LANES) * LANES
            nv = (we - ws) // LANES
            rows_all = slice(hh * GROUP + i * CHUNK, hh * GROUP + (i + 1) * CHUNK)
            if ws > c_off:
                p_ref[rows_all, c_off:ws] = jnp.zeros((CHUNK, ws - c_off), BF16)
            if we < BAND:
                p_ref[rows_all, we:BAND] = jnp.zeros((CHUNK, BAND - we), BF16)
            for b in range(CHUNK // rb):
                rows = slice(hh * GROUP + i * CHUNK + b * rb, hh * GROUP + i * CHUNK + (b + 1) * rb)
                brows = slice(i * CHUNK + b * rb, i * CHUNK + (b + 1) * rb)
                x = s_ref[rows, ws:we] + bias_ref[hh, brows, ws:we]
                cols = [x[:, c * LANES:(c + 1) * LANES] for c in range(nv)]
                if lo_col % LANES:
                    cols[0] = jnp.where(lane_rb >= HEAD_DIM, cols[0], NEG_INF)
                if hi_col % LANES:
                    cols[-1] = jnp.where(lane_rb < HEAD_DIM, cols[-1], NEG_INF)
                x = jnp.concatenate(cols, axis=1)
                m = jnp.max(x, axis=-1, keepdims=True)
                e = jnp.exp(x - m)
                l = jnp.sum(e, axis=-1, keepdims=True)
                p_ref[rows, ws:we] = e.astype(BF16)
                l_ref[rows, :] = jnp.broadcast_to(l, (rb, LANES))

def _attn_values(p_ref, l_ref, v_ref, o_ref, r0, ntiles):
    kw, c_off = _band_geometry(ntiles)
    lane = lax.broadcasted_iota(jnp.int32, (GROUP, HEAD_PAIR), 1)
    vv = v_ref[0, pl.ds(r0 + GROUP - kw, kw), :]
    o2 = jnp.dot(p_ref[:, c_off:], vv, preferred_element_type=F32)
    o2 = o2 / l_ref[...]
    o = jnp.where(lane < HEAD_DIM, o2[:GROUP], o2[GROUP:])
    o_ref[pl.ds(r0, GROUP), :] = o.astype(BF16)


def _attn_kernel(q_ref, k_ref, v_ref, w_ref, o_ref, bias_ref, s0, s1, p0, p1, l0, l1):
    n_groups = q_ref.shape[1] // GROUP
    full = BAND_TILES
    row0 = lambda g: pl.multiple_of(g * GROUP, GROUP)

    @pl.when(pl.program_id(1) == 0)
    def _():
        for hh in range(2):
            diag = jnp.broadcast_to(w_ref[hh], (GROUP, DIAG_LANES))
            rolled = pltpu.roll(diag, DIAG_LANES - (GROUP - 1), 1, stride=1, stride_axis=0)
            bias_ref[hh] = rolled[:, :BAND]

    def scores(g, s_ref, ntiles=full):
        _attn_scores(q_ref, k_ref, s_ref, row0(g), ntiles)

    def softmax(s_ref, p_ref, l_ref, ntiles=full):
        _attn_softmax(s_ref, bias_ref, p_ref, l_ref, ntiles)

    def values(g, p_ref, l_ref, ntiles=full):
        _attn_values(p_ref, l_ref, v_ref, o_ref, row0(g), ntiles)

    bufs = ((s0, p0, l0), (s1, p1, l1))
    ntiles = lambda g: min(g + 1, full)
    for t in range(n_groups + 2):
        if t >= 2:
            _, p_ref, l_ref = bufs[t % 2]
            values(t - 2, p_ref, l_ref, ntiles(t - 2))
        if 1 <= t <= n_groups:
            softmax(*bufs[(t - 1) % 2], ntiles(t - 1))
        if t < n_groups:
            scores(t, bufs[t % 2][0], ntiles(t))


def _attention(qkv, diagonals, batch, seq):
    t = batch * seq
    return pl.pallas_call(
        _attn_kernel,
        grid=(N_PAIRS, batch),
        in_specs=[
            pl.BlockSpec((1, seq, HEAD_PAIR), lambda hp, b: (hp, b, 0)),
            pl.BlockSpec((1, seq, HEAD_PAIR), lambda hp, b: (N_PAIRS + hp, b, 0)),
            pl.BlockSpec((1, seq, HEAD_PAIR), lambda hp, b: (2 * N_PAIRS + hp, b, 0)),
            pl.BlockSpec((2, 1, DIAG_LANES), lambda hp, b: (hp, 0, 0)),
        ],
        out_specs=pl.BlockSpec((seq, HEAD_PAIR), lambda hp, b: (b, hp)),
        out_shape=jax.ShapeDtypeStruct((t, D_MODEL), BF16),
        scratch_shapes=[
            pltpu.VMEM((2, GROUP, BAND), F32),
            pltpu.VMEM((2 * GROUP, BAND), F32),
            pltpu.VMEM((2 * GROUP, BAND), F32),
            pltpu.VMEM((2 * GROUP, BAND), BF16),
            pltpu.VMEM((2 * GROUP, BAND), BF16),
            pltpu.VMEM((2 * GROUP, LANES), F32),
            pltpu.VMEM((2 * GROUP, LANES), F32),
        ],
        compiler_params=pltpu.CompilerParams(
            dimension_semantics=("parallel", "arbitrary"), vmem_limit_bytes=VMEM_LIMIT),
        name="band_attention",
    )(qkv, qkv, qkv, diagonals)


def _band_bias_diagonals(rel_table):
    j = jnp.arange(DIAG_LANES)
    dist = (GROUP - 1 - j) + (BAND - GROUP)
    w = rel_table[:, jnp.clip(dist, -MAX_REL, MAX_REL) + MAX_REL]
    return w[:, None, :]


def _gelu_tanh(x):
    k = math.sqrt(2.0 / math.pi)
    inner = x * (k + (k * 0.044715) * (x * x))
    return x * (0.5 + 0.5 * jnp.tanh(inner))


def _rglru_tile(xr, gr_ref, side_work, cw_ref, cb_ref, wr_ref, br_ref, wi_ref, bi_ref, lam_ref,
                o_ref, xs_ref, xc_ref, rp_ref, ip_ref, a_ref, b_ref, hl_ref, pp_ref, h_ref):
    tm = xr.shape[0]
    seg = tm // SUBLANES
    n_lane_tiles = D_MODEL // LANES
    tiles_per_slab = MXU_DIM // LANES
    lane_tile = lambda c: slice(c * LANES, (c + 1) * LANES)

    for c in range(n_lane_tiles):
        xs_ref[c, SUBLANES:, :] = xr[:, lane_tile(c)]

    for d in range(D_MODEL // MXU_DIM):
        for k in range(tiles_per_slab):
            c = d * tiles_per_slab + k
            for j in range(SUBLANES):
                acc = cb_ref[:, lane_tile(c)]
                for tap in range(CONV_W):
                    off = SUBLANES - (CONV_W - 1) + tap + j * seg
                    acc = acc + xs_ref[c, off:off + seg, :] * cw_ref[tap:tap + 1, lane_tile(c)]
                xc_ref[j * seg:(j + 1) * seg, lane_tile(k)] = acc
                side_work(COST_CONV)
            xs_ref[c, 0:SUBLANES, :] = xs_ref[c, tm:tm + SUBLANES, :]
        xcb = xc_ref[...].astype(BF16)
        rp_ref[...] = jnp.dot(xcb, wr_ref[d], preferred_element_type=F32)
        ip_ref[...] = jnp.dot(xcb, wi_ref[d], preferred_element_type=F32)
        for k in range(tiles_per_slab):
            c = d * tiles_per_slab + k
            nlam = -lam_ref[:, lane_tile(c)]
            softplus = jnp.maximum(nlam, 0.0) + jnp.log1p(jnp.exp(-jnp.abs(nlam)))
            log_a_scale = (-RG_C) * softplus
            for j in range(SUBLANES):
                rows = slice(j * seg, (j + 1) * seg)
                xc = xc_ref[rows, lane_tile(k)]
                r = jax.nn.sigmoid(rp_ref[rows, lane_tile(k)] + br_ref[:, lane_tile(c)])
                ig = jax.nn.sigmoid(ip_ref[rows, lane_tile(k)] + bi_ref[:, lane_tile(c)])
                log_a = r * log_a_scale
                a = jnp.exp(log_a)
                y = jnp.tanh(log_a) * (-1.0 - a * a)
                mult = jnp.where(y > 0.0, y * lax.rsqrt(y), 0.0)
                srows = slice(j * SEG_PITCH, j * SEG_PITCH + seg)
                a_ref[c, srows, :] = a
                b_ref[c, srows, :] = mult * (ig * xc)
                side_work(COST_GATE)

    hs = [jnp.zeros((SUBLANES, LANES), F32) for _ in range(n_lane_tiles)]
    ps = [jnp.ones((SUBLANES, LANES), F32) for _ in range(n_lane_tiles)]
    for t in range(seg):
        idx = pl.ds(t, SUBLANES, stride=SEG_PITCH)
        for c in range(n_lane_tiles):
            a_t = a_ref[c, idx, :]
            hs[c] = a_t * hs[c] + b_ref[c, idx, :]
            ps[c] = a_t * ps[c]
            hl_ref[c, idx, :] = hs[c]
            pp_ref[c, idx, :] = ps[c]
        side_work(COST_SCAN_STEP)
    h_end = jnp.concatenate(hs, axis=1)
    p_end = jnp.concatenate(ps, axis=1)

    h_in = h_ref[0:1, :]
    for j in range(SUBLANES):
        rows = slice(j * seg, (j + 1) * seg)
        srows = slice(j * SEG_PITCH, j * SEG_PITCH + seg)
        for c in range(n_lane_tiles):
            hj = hl_ref[c, srows, :] + pp_ref[c, srows, :] * h_in[:, lane_tile(c)]
            o_ref[rows, lane_tile(c)] = (hj * _gelu_tanh(gr_ref[rows, lane_tile(c)])).astype(BF16)
            side_work(COST_OUT)
        h_in = p_end[j:j + 1, :] * h_in + h_end[j:j + 1, :]
    h_ref[0:1, :] = h_in


def _block_diag_tiles(w):
    per = MXU_DIM // RG_BLOCK
    n_diag = D_MODEL // MXU_DIM
    w4 = w.reshape(n_diag, per, RG_BLOCK, RG_BLOCK)
    eye = jnp.eye(per, dtype=w.dtype)
    tiles = jnp.einsum('dpij,pq->dpiqj', w4, eye)
    return tiles.reshape(n_diag, MXU_DIM, MXU_DIM)


def _merge_kernel(x_ref, at_ref, hb_ref, ga_ref, gb_ref, bm_ref, wa_ref, wb_ref, wo_ref, h_ref):
    ya = jnp.dot(at_ref[...], wa_ref[...], preferred_element_type=F32)
    yb = jnp.dot(hb_ref[...], wb_ref[...], preferred_element_type=F32)
    g_a = jax.nn.sigmoid(ga_ref[...].astype(F32) + bm_ref[:, :D_MODEL])
    g_b = jax.nn.sigmoid(gb_ref[...].astype(F32) + bm_ref[:, D_MODEL:])
    mix = (g_a * ya + g_b * yb).astype(BF16)
    h_ref[...] = x_ref[...] + jnp.dot(mix, wo_ref[...], preferred_element_type=F32)


def _merge(x2, attn, hb, gates, bm, wa, wb, wo, tm):
    t = x2.shape[0]
    row = lambda i: (i, 0)
    return pl.pallas_call(
        _merge_kernel,
        grid=(t // tm,),
        in_specs=[
            pl.BlockSpec((tm, D_MODEL), row),
            pl.BlockSpec((tm, D_MODEL), row),
            pl.BlockSpec((tm, D_MODEL), row),
            pl.BlockSpec((tm, D_MODEL), lambda i: (i, 0)),
            pl.BlockSpec((tm, D_MODEL), lambda i: (i, 1)),
            _resident((1, 2 * D_MODEL)),
            _resident((D_MODEL, D_MODEL)),
            _resident((D_MODEL, D_MODEL)),
            _resident((D_MODEL, D_MODEL)),
        ],
        out_specs=pl.BlockSpec((tm, D_MODEL), row),
        out_shape=jax.ShapeDtypeStruct((t, D_MODEL), F32),
        compiler_params=pltpu.CompilerParams(
            dimension_semantics=("parallel",), vmem_limit_bytes=VMEM_LIMIT),
        name="merge",
    )(x2, attn, hb, gates, gates, bm, wa, wb, wo)


def _ffn_kernel(h_ref, gn_ref, win_ref, wout_ref, gf_ref, o_ref, *, splits):
    h = h_ref[...]
    hn = _rms(h, gn_ref[...]).astype(BF16)
    d_ff = wout_ref.shape[0]
    acc = h
    for c0, c1 in splits:
        g = jnp.dot(hn, win_ref[:, c0:c1], preferred_element_type=F32)
        up = jnp.dot(hn, win_ref[:, d_ff + c0:d_ff + c1], preferred_element_type=F32)
        act = (jax.nn.silu(g) * up).astype(BF16)
        acc = acc + jnp.dot(act, wout_ref[c0:c1, :], preferred_element_type=F32)
    o_ref[...] = _rms(acc, gf_ref[...])


def _ffn(h, gn, win, wout, gf, tm):
    t = h.shape[0]
    d_ff = wout.shape[0]
    n_tiles = d_ff // MXU_DIM
    half = (n_tiles + 1) // 2 * MXU_DIM
    splits = ((0, half), (half, d_ff))
    row = lambda i: (i, 0)
    return pl.pallas_call(
        functools.partial(_ffn_kernel, splits=splits),
        grid=(t // tm,),
        in_specs=[
            pl.BlockSpec((tm, D_MODEL), row),
            _resident((1, D_MODEL)),
            _resident((D_MODEL, 2 * d_ff)),
            _resident((d_ff, D_MODEL)),
            _resident((1, D_MODEL)),
        ],
        out_specs=pl.BlockSpec((tm, D_MODEL), row),
        out_shape=jax.ShapeDtypeStruct((t, D_MODEL), F32),
        compiler_params=pltpu.CompilerParams(
            dimension_semantics=("parallel",), vmem_limit_bytes=VMEM_LIMIT),
        name="ffn",
    )(h, gn, win, wout, gf)


def kernel(x, norm_mix_g, w_in, b_merge, rel_table, w_attn_out, conv_w, conv_b, w_rg_r, b_rg_r,
           w_rg_i, b_rg_i, rg_lambda, w_rnn_out, w_o, norm_ffn_g, w_ffn_in, w_ffn_out, final_norm_g):
    batch, seq, d = x.shape
    assert d == D_MODEL and seq % GROUP == 0 and seq // GROUP >= BAND_TILES
    assert w_in.shape[0] == 1, "single-layer block"
    x2 = x.reshape(batch * seq, d)
    row = lambda v: v.reshape(1, -1)

    assert seq % RG_TILE == 0
    qkv, gates, hb = _inproj(
        x2, row(norm_mix_g[0]), w_in[0].astype(BF16), conv_w[0], row(conv_b[0]),
        _block_diag_tiles(w_rg_r[0]).astype(BF16), row(b_rg_r[0]),
        _block_diag_tiles(w_rg_i[0]).astype(BF16), row(b_rg_i[0]), row(rg_lambda[0]), seq)
    attn = _attention(qkv, _band_bias_diagonals(rel_table[0]), batch, seq)
    h = _merge(x2, attn, hb, gates, row(b_merge[0]), w_attn_out[0], w_rnn_out[0], w_o[0], tm=512)
    out = _ffn(h, row(norm_ffn_g[0]), w_ffn_in[0], w_ffn_out[0], row(final_norm_g), tm=512)
    return out.reshape(batch, seq, d)
```

```python
import functools
import math

import jax
import jax.numpy as jnp
from jax import lax
from jax.experimental import pallas as pl
from jax.experimental.pallas import tpu as pltpu

F32 = jnp.float32
BF16 = jnp.bfloat16

D_MODEL = 1024
CHUNK = 64
LEFT_CHUNKS = 8
N_HEADS = 16
HEAD_DIM = 64
MAX_REL = 128
N_RG_BLOCKS = 16
RG_BLOCK = D_MODEL // N_RG_BLOCKS
CONV_W = 4
RG_C = 8.0
EPS = 1e-6
NEG_INF = -1e30

LANES = 128
SUBLANES = 8
MXU_DIM = 256
HEAD_PAIR = 2 * HEAD_DIM
N_PAIRS = N_HEADS // 2
GROUP = 4 * CHUNK
BAND_TILES = LEFT_CHUNKS * CHUNK // GROUP + 1
BAND = BAND_TILES * GROUP
DIAG_LANES = -(-(GROUP + BAND - 1) // LANES) * LANES
VMEM_LIMIT = 56 * 1024 * 1024
RG_TILE = 512
SEG_PITCH = RG_TILE // SUBLANES + SUBLANES // 2
COST_CONV, COST_GATE, COST_SCAN_STEP, COST_OUT = 16, 52, 23, 24


def _rms(x, g):
    var = jnp.mean(x * x, axis=-1, keepdims=True)
    return x * lax.rsqrt(var + EPS) * g


def _resident(shape):
    zeros = (0,) * len(shape)
    return pl.BlockSpec(shape, lambda *_: zeros, pipeline_mode=pl.Buffered(1))


def _inproj_kernel(*refs, tiles_per_seq, n_cast):
    (x_ref, g_ref, w_ref, cw_ref, cb_ref, wr_ref, br_ref, wi_ref, bi_ref, lam_ref), refs = refs[:10], refs[10:]
    cast_in, refs = refs[:n_cast], refs[n_cast:]
    (qkv_ref, gates_ref, hb_ref), refs = refs[:3], refs[3:]
    cast_out, refs = refs[:n_cast], refs[n_cast:]
    xn_ref, gr_ref, xs_ref, xc_ref, rp_ref, ip_ref, a_ref, b_ref, hl_ref, pp_ref, h_ref = refs

    for src_ref, dst_ref in zip(cast_in, cast_out):
        dst_ref[...] = src_ref[...].astype(BF16)

    @pl.when(pl.program_id(0) % tiles_per_seq == 0)
    def _():
        xs_ref[:, 0:SUBLANES, :] = jnp.zeros((D_MODEL // LANES, SUBLANES, LANES), F32)
        h_ref[...] = jnp.zeros_like(h_ref)

    xn_ref[...] = _rms(x_ref[...], g_ref[...]).astype(BF16)

    def u_cols(c0, width):
        return jnp.dot(xn_ref[...], w_ref[:, c0:c0 + width], preferred_element_type=F32)

    def unit(j, n):
        acc = u_cols(j * D_MODEL + n * MXU_DIM, MXU_DIM)
        if j < 3:
            for k in range(MXU_DIM // HEAD_PAIR):
                hp = n * (MXU_DIM // HEAD_PAIR) + k
                qkv_ref[j * N_PAIRS + hp] = acc[:, k * HEAD_PAIR:(k + 1) * HEAD_PAIR].astype(BF16)
        elif j == 4:
            gr_ref[:, n * MXU_DIM:(n + 1) * MXU_DIM] = acc
        else:
            c0 = (j - 5) * D_MODEL + n * MXU_DIM
            gates_ref[:, c0:c0 + MXU_DIM] = acc.astype(BF16)

    units = [(j, n) for j in (4, 0, 1, 2, 5, 6) for n in range(D_MODEL // MXU_DIM)]
    n_pieces = SUBLANES * (D_MODEL // LANES)
    total_cost = n_pieces * (COST_CONV + COST_GATE + COST_OUT) + (x_ref.shape[0] // SUBLANES) * COST_SCAN_STEP
    cost_per_unit = total_cost / len(units)
    spent = [0.0]

    def side_work(cost):
        spent[0] += cost
        while units and spent[0] >= cost_per_unit:
            spent[0] -= cost_per_unit
            unit(*units.pop(0))

    _rglru_tile(u_cols(3 * D_MODEL, D_MODEL), gr_ref, side_work,
                cw_ref, cb_ref, wr_ref, br_ref, wi_ref, bi_ref, lam_ref,
                hb_ref, xs_ref, xc_ref, rp_ref, ip_ref, a_ref, b_ref, hl_ref, pp_ref, h_ref)
    while units:
        unit(*units.pop(0))


BF16_SUBLANES = 2 * SUBLANES


def _cast_slice_spec(rows, cols, n_steps):
    repeat = next(k for k in (1, 2, 4, 8)
                  if rows * k % n_steps == 0 and (rows * k // n_steps) % BF16_SUBLANES == 0)
    return pl.BlockSpec((rows * repeat // n_steps, cols), lambda i: (i // repeat, 0))


def _inproj(x2, g, w, cw, cb, wr, br, wi, bi, lam, later_weights, seq):
    t = x2.shape[0]
    tm = RG_TILE
    n_steps = t // tm
    n_diag = D_MODEL // MXU_DIM
    scan_scratch = pltpu.VMEM((D_MODEL // LANES, SUBLANES * SEG_PITCH, LANES), F32)
    slab_scratch = pltpu.VMEM((tm, MXU_DIM), F32)
    cast_specs = [_cast_slice_spec(*wt.shape, n_steps) for wt in later_weights]
    outs = pl.pallas_call(
        functools.partial(_inproj_kernel, tiles_per_seq=seq // tm, n_cast=len(later_weights)),
        grid=(n_steps,),
        in_specs=[
            pl.BlockSpec((tm, D_MODEL), lambda i: (i, 0)),
            _resident((1, D_MODEL)),
            _resident((D_MODEL, 7 * D_MODEL)),
            _resident((CONV_W, D_MODEL)),
            _resident((1, D_MODEL)),
            _resident((n_diag, MXU_DIM, MXU_DIM)),
            _resident((1, D_MODEL)),
            _resident((n_diag, MXU_DIM, MXU_DIM)),
            _resident((1, D_MODEL)),
            _resident((1, D_MODEL)),
            *cast_specs,
        ],
        out_specs=[
            pl.BlockSpec((3 * N_PAIRS, tm, HEAD_PAIR), lambda i: (0, i, 0)),
            pl.BlockSpec((tm, 2 * D_MODEL), lambda i: (i, 0)),
            pl.BlockSpec((tm, D_MODEL), lambda i: (i, 0)),
            *cast_specs,
        ],
        out_shape=[
            jax.ShapeDtypeStruct((3 * N_PAIRS, t, HEAD_PAIR), BF16),
            jax.ShapeDtypeStruct((t, 2 * D_MODEL), BF16),
            jax.ShapeDtypeStruct((t, D_MODEL), BF16),
            *[jax.ShapeDtypeStruct(wt.shape, BF16) for wt in later_weights],
        ],
        scratch_shapes=[
            pltpu.VMEM((tm, D_MODEL), BF16),
            pltpu.VMEM((tm, D_MODEL), F32),
            pltpu.VMEM((D_MODEL // LANES, tm + SUBLANES, LANES), F32),
            slab_scratch, slab_scratch, slab_scratch,
            scan_scratch, scan_scratch, scan_scratch, scan_scratch,
            pltpu.VMEM((SUBLANES, D_MODEL), F32),
        ],
        compiler_params=pltpu.CompilerParams(
            dimension_semantics=("arbitrary",), vmem_limit_bytes=VMEM_LIMIT),
        name="inproj_rglru",
    )(x2, g, w, cw, cb, wr, br, wi, bi, lam, *later_weights)
    return outs[0], outs[1], outs[2], outs[3:]


def _band_geometry(ntiles):
    kw = ntiles * GROUP
    return kw, BAND - kw


def _attn_scores(q_ref, k_ref, s_ref, r0, ntiles):
    kw, c_off = _band_geometry(ntiles)
    lane = lax.broadcasted_iota(jnp.int32, (GROUP, HEAD_PAIR), 1)
    first_head = lane < HEAD_DIM
    qp = q_ref[0, pl.ds(r0, GROUP), :] * jnp.asarray(1.0 / math.sqrt(HEAD_DIM), BF16)
    zero = jnp.zeros_like(qp)
    qq = jnp.concatenate([jnp.where(first_head, qp, zero),
                          jnp.where(first_head, zero, qp)], axis=0)
    kk = k_ref[0, pl.ds(r0 + GROUP - kw, kw), :]
    s_ref[:, c_off:] = lax.dot_general(qq, kk, (((1,), (1,)), ((), ())),
                                       preferred_element_type=F32)


def _attn_softmax(s_ref, bias_ref, p_ref, l_ref, ntiles):
    _, c_off = _band_geometry(ntiles)
    rb = 32
    lane_rb = lax.broadcasted_iota(jnp.int32, (rb, LANES), 1)
    for hh in range(2):
        for i in range(GROUP // CHUNK):
            lo_col = max(i * CHUNK, c_off)
            hi_col = i * CHUNK + (LEFT_CHUNKS + 1) * CHUNK
            ws = (lo_col // LANES) * LANES
            we = -(-hi_col // LANES) * LANES
            nv = (we - ws) // LANES
            rows_all = slice(hh * GROUP + i * CHUNK, hh * GROUP + (i + 1) * CHUNK)
            if ws > c_off:
                p_ref[rows_all, c_off:ws] = jnp.zeros((CHUNK, ws - c_off), BF16)
            if we < BAND:
                p_ref[rows_all, we:BAND] = jnp.zeros((CHUNK, BAND - we), BF16)
            for b in range(CHUNK // rb):
                rows = slice(hh * GROUP + i * CHUNK + b * rb, hh * GROUP + i * CHUNK + (b + 1) * rb)
                brows = slice(i * CHUNK + b * rb, i * CHUNK + (b + 1) * rb)
                x = s_ref[rows, ws:we] + bias_ref[hh, brows, ws:we]
                cols = [x[:, c * LANES:(c + 1) * LANES] for c in range(nv)]
                if lo_col % LANES:
                    cols[0] = jnp.where(lane_rb >= HEAD_DIM, cols[0], NEG_INF)
                if hi_col % LANES:
                    cols[-1] = jnp.where(lane_rb < HEAD_DIM, cols[-1], NEG_INF)
                x = jnp.concatenate(cols, axis=1)
                m = jnp.max(x, axis=-1, keepdims=True)
                e = jnp.exp(x - m)
                l = jnp.sum(e, axis=-1, keepdims=True)
                p_ref[rows, ws:we] = e.astype(BF16)
                l_ref[rows, :] = jnp.broadcast_to(l, (rb, LANES))

def _attn_values(p_ref, l_ref, v_ref, o_ref, r0, ntiles):
    kw, c_off = _band_geometry(ntiles)
    lane = lax.broadcasted_iota(jnp.int32, (GROUP, HEAD_PAIR), 1)
    vv = v_ref[0, pl.ds(r0 + GROUP - kw, kw), :]
    o2 = jnp.dot(p_ref[:, c_off:], vv, preferred_element_type=F32)
    o2 = o2 / l_ref[...]
    o = jnp.where(lane < HEAD_DIM, o2[:GROUP], o2[GROUP:])
    o_ref[pl.ds(r0, GROUP), :] = o.astype(BF16)


def _attn_kernel(q_ref, k_ref, v_ref, w_ref, o_ref, bias_ref, s0, s1, p0, p1, l0, l1):
    n_groups = q_ref.shape[1] // GROUP
    full = BAND_TILES
    row0 = lambda g: pl.multiple_of(g * GROUP, GROUP)

    @pl.when(pl.program_id(1) == 0)
    def _():
        for hh in range(2):
            diag = jnp.broadcast_to(w_ref[hh], (GROUP, DIAG_LANES))
            rolled = pltpu.roll(diag, DIAG_LANES - (GROUP - 1), 1, stride=1, stride_axis=0)
            bias_ref[hh] = rolled[:, :BAND]

    def scores(g, s_ref, ntiles=full):
        _attn_scores(q_ref, k_ref, s_ref, row0(g), ntiles)

    def softmax(s_ref, p_ref, l_ref, ntiles=full):
        _attn_softmax(s_ref, bias_ref, p_ref, l_ref, ntiles)

    def values(g, p_ref, l_ref, ntiles=full):
        _attn_values(p_ref, l_ref, v_ref, o_ref, row0(g), ntiles)

    bufs = ((s0, p0, l0), (s1, p1, l1))
    ntiles = lambda g: min(g + 1, full)
    for t in range(n_groups + 2):
        if t >= 2:
            _, p_ref, l_ref = bufs[t % 2]
            values(t - 2, p_ref, l_ref, ntiles(t - 2))
        if 1 <= t <= n_groups:
            softmax(*bufs[(t - 1) % 2], ntiles(t - 1))
        if t < n_groups:
            scores(t, bufs[t % 2][0], ntiles(t))


def _attention(qkv, diagonals, batch, seq):
    t = batch * seq
    return pl.pallas_call(
        _attn_kernel,
        grid=(N_PAIRS, batch),
        in_specs=[
            pl.BlockSpec((1, seq, HEAD_PAIR), lambda hp, b: (hp, b, 0)),
            pl.BlockSpec((1, seq, HEAD_PAIR), lambda hp, b: (N_PAIRS + hp, b, 0)),
            pl.BlockSpec((1, seq, HEAD_PAIR), lambda hp, b: (2 * N_PAIRS + hp, b, 0)),
            pl.BlockSpec((2, 1, DIAG_LANES), lambda hp, b: (hp, 0, 0)),
        ],
        out_specs=pl.BlockSpec((seq, HEAD_PAIR), lambda hp, b: (b, hp)),
        out_shape=jax.ShapeDtypeStruct((t, D_MODEL), BF16),
        scratch_shapes=[
            pltpu.VMEM((2, GROUP, BAND), F32),
            pltpu.VMEM((2 * GROUP, BAND), F32),
            pltpu.VMEM((2 * GROUP, BAND), F32),
            pltpu.VMEM((2 * GROUP, BAND), BF16),
            pltpu.VMEM((2 * GROUP, BAND), BF16),
            pltpu.VMEM((2 * GROUP, LANES), F32),
            pltpu.VMEM((2 * GROUP, LANES), F32),
        ],
        compiler_params=pltpu.CompilerParams(
            dimension_semantics=("parallel", "arbitrary"), vmem_limit_bytes=VMEM_LIMIT),
        name="band_attention",
    )(qkv, qkv, qkv, diagonals)


def _band_bias_diagonals(rel_table):
    j = jnp.arange(DIAG_LANES)
    dist = (GROUP - 1 - j) + (BAND - GROUP)
    w = rel_table[:, jnp.clip(dist, -MAX_REL, MAX_REL) + MAX_REL]
    return w[:, None, :]


def _gelu_tanh(x):
    k = math.sqrt(2.0 / math.pi)
    inner = x * (k + (k * 0.044715) * (x * x))
    return x * (0.5 + 0.5 * jnp.tanh(inner))


def _rglru_tile(xr, gr_ref, side_work, cw_ref, cb_ref, wr_ref, br_ref, wi_ref, bi_ref, lam_ref,
                o_ref, xs_ref, xc_ref, rp_ref, ip_ref, a_ref, b_ref, hl_ref, pp_ref, h_ref):
    tm = xr.shape[0]
    seg = tm // SUBLANES
    n_lane_tiles = D_MODEL // LANES
    tiles_per_slab = MXU_DIM // LANES
    lane_tile = lambda c: slice(c * LANES, (c + 1) * LANES)

    for c in range(n_lane_tiles):
        xs_ref[c, SUBLANES:, :] = xr[:, lane_tile(c)]

    for d in range(D_MODEL // MXU_DIM):
        for k in range(tiles_per_slab):
            c = d * tiles_per_slab + k
            for j in range(SUBLANES):
                acc = cb_ref[:, lane_tile(c)]
                for tap in range(CONV_W):
                    off = SUBLANES - (CONV_W - 1) + tap + j * seg
                    acc = acc + xs_ref[c, off:off + seg, :] * cw_ref[tap:tap + 1, lane_tile(c)]
                xc_ref[j * seg:(j + 1) * seg, lane_tile(k)] = acc
                side_work(COST_CONV)
            xs_ref[c, 0:SUBLANES, :] = xs_ref[c, tm:tm + SUBLANES, :]
        xcb = xc_ref[...].astype(BF16)
        rp_ref[...] = jnp.dot(xcb, wr_ref[d], preferred_element_type=F32)
        ip_ref[...] = jnp.dot(xcb, wi_ref[d], preferred_element_type=F32)
        for k in range(tiles_per_slab):
            c = d * tiles_per_slab + k
            nlam = -lam_ref[:, lane_tile(c)]
            softplus = jnp.maximum(nlam, 0.0) + jnp.log1p(jnp.exp(-jnp.abs(nlam)))
            log_a_scale = (-RG_C) * softplus
            for j in range(SUBLANES):
                rows = slice(j * seg, (j + 1) * seg)
                xc = xc_ref[rows, lane_tile(k)]
                r = jax.nn.sigmoid(rp_ref[rows, lane_tile(k)] + br_ref[:, lane_tile(c)])
                ig = jax.nn.sigmoid(ip_ref[rows, lane_tile(k)] + bi_ref[:, lane_tile(c)])
                log_a = r * log_a_scale
                a = jnp.exp(log_a)
                y = jnp.tanh(log_a) * (-1.0 - a * a)
                mult = jnp.where(y > 0.0, y * lax.rsqrt(y), 0.0)
                srows = slice(j * SEG_PITCH, j * SEG_PITCH + seg)
                a_ref[c, srows, :] = a
                b_ref[c, srows, :] = mult * (ig * xc)
                side_work(COST_GATE)

    hs = [jnp.zeros((SUBLANES, LANES), F32) for _ in range(n_lane_tiles)]
    ps = [jnp.ones((SUBLANES, LANES), F32) for _ in range(n_lane_tiles)]
    for t in range(seg):
        idx = pl.ds(t, SUBLANES, stride=SEG_PITCH)
        for c in range(n_lane_tiles):
            a_t = a_ref[c, idx, :]
            hs[c] = a_t * hs[c] + b_ref[c, idx, :]
            ps[c] = a_t * ps[c]
            hl_ref[c, idx, :] = hs[c]
            pp_ref[c, idx, :] = ps[c]
        side_work(COST_SCAN_STEP)
    h_end = jnp.concatenate(hs, axis=1)
    p_end = jnp.concatenate(ps, axis=1)

    h_in = h_ref[0:1, :]
    for j in range(SUBLANES):
        rows = slice(j * seg, (j + 1) * seg)
        srows = slice(j * SEG_PITCH, j * SEG_PITCH + seg)
        for c in range(n_lane_tiles):
            hj = hl_ref[c, srows, :] + pp_ref[c, srows, :] * h_in[:, lane_tile(c)]
            o_ref[rows, lane_tile(c)] = (hj * _gelu_tanh(gr_ref[rows, lane_tile(c)])).astype(BF16)
            side_work(COST_OUT)
        h_in = p_end[j:j + 1, :] * h_in + h_end[j:j + 1, :]
    h_ref[0:1, :] = h_in


def _block_diag_tiles(w):
    per = MXU_DIM // RG_BLOCK
    n_diag = D_MODEL // MXU_DIM
    w4 = w.reshape(n_diag, per, RG_BLOCK, RG_BLOCK)
    eye = jnp.eye(per, dtype=w.dtype)
    tiles = jnp.einsum('dpij,pq->dpiqj', w4, eye)
    return tiles.reshape(n_diag, MXU_DIM, MXU_DIM)


def _mixffn_kernel(x_ref, at_ref, hb_ref, ga_ref, gb_ref, bm_ref, wa_ref, wb_ref, wo_ref,
                   gn_ref, win_ref, wout_ref, gf_ref, o_ref, *, splits):
    ya = jnp.dot(at_ref[...], wa_ref[...], preferred_element_type=F32)
    yb = jnp.dot(hb_ref[...], wb_ref[...], preferred_element_type=F32)
    g_a = jax.nn.sigmoid(ga_ref[...].astype(F32) + bm_ref[:, :D_MODEL])
    g_b = jax.nn.sigmoid(gb_ref[...].astype(F32) + bm_ref[:, D_MODEL:])
    mix = (g_a * ya + g_b * yb).astype(BF16)
    h = x_ref[...] + jnp.dot(mix, wo_ref[...], preferred_element_type=F32)
    hn = _rms(h, gn_ref[...]).astype(BF16)
    d_ff = wout_ref.shape[0]
    acc = h
    for c0, c1 in splits:
        g = jnp.dot(hn, win_ref[:, c0:c1], preferred_element_type=F32)
        up = jnp.dot(hn, win_ref[:, d_ff + c0:d_ff + c1], preferred_element_type=F32)
        act = (jax.nn.silu(g) * up).astype(BF16)
        acc = acc + jnp.dot(act, wout_ref[c0:c1, :], preferred_element_type=F32)
    o_ref[...] = _rms(acc, gf_ref[...])


def _mixffn(x2, attn, hb, gates, bm, wa, wb, wo, gn, win, wout, gf, tm):
    t = x2.shape[0]
    d_ff = wout.shape[0]
    n_tiles = d_ff // MXU_DIM
    half = (n_tiles + 1) // 2 * MXU_DIM
    splits = ((0, half), (half, d_ff))
    row = lambda i: (i, 0)
    return pl.pallas_call(
        functools.partial(_mixffn_kernel, splits=splits),
        grid=(t // tm,),
        in_specs=[
            pl.BlockSpec((tm, D_MODEL), row),
            pl.BlockSpec((tm, D_MODEL), row),
            pl.BlockSpec((tm, D_MODEL), row),
            pl.BlockSpec((tm, D_MODEL), lambda i: (i, 0)),
            pl.BlockSpec((tm, D_MODEL), lambda i: (i, 1)),
            _resident((1, 2 * D_MODEL)),
            _resident((D_MODEL, D_MODEL)),
            _resident((D_MODEL, D_MODEL)),
            _resident((D_MODEL, D_MODEL)),
            _resident((1, D_MODEL)),
            _resident((D_MODEL, 2 * d_ff)),
            _resident((d_ff, D_MODEL)),
            _resident((1, D_MODEL)),
        ],
        out_specs=pl.BlockSpec((tm, D_MODEL), row),
        out_shape=jax.ShapeDtypeStruct((t, D_MODEL), F32),
        compiler_params=pltpu.CompilerParams(
            dimension_semantics=("parallel",), vmem_limit_bytes=VMEM_LIMIT),
        name="mix_ffn",
    )(x2, attn, hb, gates, gates, bm, wa, wb, wo, gn, win, wout, gf)


def kernel(x, norm_mix_g, w_in, b_merge, rel_table, w_attn_out, conv_w, conv_b, w_rg_r, b_rg_r,
           w_rg_i, b_rg_i, rg_lambda, w_rnn_out, w_o, norm_ffn_g, w_ffn_in, w_ffn_out, final_norm_g):
    batch, seq, d = x.shape
    assert d == D_MODEL and seq % GROUP == 0 and seq // GROUP >= BAND_TILES
    assert w_in.shape[0] == 1, "single-layer block"
    x2 = x.reshape(batch * seq, d)
    row = lambda v: v.reshape(1, -1)

    assert seq % RG_TILE == 0
    later_weights = (w_attn_out[0], w_rnn_out[0], w_o[0], w_ffn_in[0], w_ffn_out[0])
    qkv, gates, hb, (wa, wb, wo, wfi, wfo) = _inproj(
        x2, row(norm_mix_g[0]), w_in[0].astype(BF16), conv_w[0], row(conv_b[0]),
        _block_diag_tiles(w_rg_r[0]).astype(BF16), row(b_rg_r[0]),
        _block_diag_tiles(w_rg_i[0]).astype(BF16), row(b_rg_i[0]), row(rg_lambda[0]),
        later_weights, seq)
    attn = _attention(qkv, _band_bias_diagonals(rel_table[0]), batch, seq)
    out = _mixffn(x2, attn, hb, gates, row(b_merge[0]), wa, wb, wo, row(norm_ffn_g[0]), wfi, wfo,
                  row(final_norm_g), tm=512)
    return out.reshape(batch, seq, d)
```

```python
import functools
import math

import jax
import jax.numpy as jnp
from jax import lax
from jax.experimental import pallas as pl
from jax.experimental.pallas import tpu as pltpu

F32 = jnp.float32
BF16 = jnp.bfloat16

D_MODEL = 1024
CHUNK = 64
LEFT_CHUNKS = 8
N_HEADS = 16
HEAD_DIM = 64
MAX_REL = 128
N_RG_BLOCKS = 16
RG_BLOCK = D_MODEL // N_RG_BLOCKS
CONV_W = 4
RG_C = 8.0
EPS = 1e-6
NEG_INF = -1e30

LANES = 128
SUBLANES = 8
MXU_DIM = 256
HEAD_PAIR = 2 * HEAD_DIM
N_PAIRS = N_HEADS // 2
GROUP = 4 * CHUNK
BAND_TILES = LEFT_CHUNKS * CHUNK // GROUP + 1
BAND = BAND_TILES * GROUP
DIAG_LANES = -(-(GROUP + BAND - 1) // LANES) * LANES
VMEM_LIMIT = 56 * 1024 * 1024
RG_TILE = 512
ATTN_SEQS_PER_STEP = 2
SEG_PITCH = RG_TILE // SUBLANES + SUBLANES // 2
COST_CONV, COST_GATE, COST_SCAN_STEP, COST_OUT = 16, 52, 23, 24


def _rms(x, g):
    var = jnp.mean(x * x, axis=-1, keepdims=True)
    return x * lax.rsqrt(var + EPS) * g


def _resident(shape):
    zeros = (0,) * len(shape)
    return pl.BlockSpec(shape, lambda *_: zeros, pipeline_mode=pl.Buffered(1))


def _inproj_kernel(*refs, tiles_per_seq, n_cast):
    (x_ref, g_ref, w_ref, cw_ref, cb_ref, wr_ref, br_ref, wi_ref, bi_ref, lam_ref), refs = refs[:10], refs[10:]
    cast_in, refs = refs[:n_cast], refs[n_cast:]
    (qkv_ref, gates_ref, hb_ref), refs = refs[:3], refs[3:]
    cast_out, refs = refs[:n_cast], refs[n_cast:]
    xn_ref, gr_ref, xs_ref, xc_ref, rp_ref, ip_ref, a_ref, b_ref, hl_ref, pp_ref, h_ref = refs

    for src_ref, dst_ref in zip(cast_in, cast_out):
        dst_ref[...] = src_ref[...].astype(BF16)

    @pl.when(pl.program_id(0) % tiles_per_seq == 0)
    def _():
        xs_ref[:, 0:SUBLANES, :] = jnp.zeros((D_MODEL // LANES, SUBLANES, LANES), F32)
        h_ref[...] = jnp.zeros_like(h_ref)

    xn_ref[...] = _rms(x_ref[...], g_ref[...]).astype(BF16)

    def u_cols(c0, width):
        return jnp.dot(xn_ref[...], w_ref[:, c0:c0 + width], preferred_element_type=F32)

    def unit(j, n):
        acc = u_cols(j * D_MODEL + n * MXU_DIM, MXU_DIM)
        if j < 3:
            for k in range(MXU_DIM // HEAD_PAIR):
                hp = n * (MXU_DIM // HEAD_PAIR) + k
                qkv_ref[j * N_PAIRS + hp] = acc[:, k * HEAD_PAIR:(k + 1) * HEAD_PAIR].astype(BF16)
        elif j == 4:
            gr_ref[:, n * MXU_DIM:(n + 1) * MXU_DIM] = acc
        else:
            c0 = (j - 5) * D_MODEL + n * MXU_DIM
            gates_ref[:, c0:c0 + MXU_DIM] = acc.astype(BF16)

    units = [(j, n) for j in (4, 0, 1, 2, 5, 6) for n in range(D_MODEL // MXU_DIM)]
    n_pieces = SUBLANES * (D_MODEL // LANES)
    total_cost = n_pieces * (COST_CONV + COST_GATE + COST_OUT) + (x_ref.shape[0] // SUBLANES) * COST_SCAN_STEP
    cost_per_unit = total_cost / len(units)
    spent = [0.0]

    def side_work(cost):
        spent[0] += cost
        while units and spent[0] >= cost_per_unit:
            spent[0] -= cost_per_unit
            unit(*units.pop(0))

    _rglru_tile(u_cols(3 * D_MODEL, D_MODEL), gr_ref, side_work,
                cw_ref, cb_ref, wr_ref, br_ref, wi_ref, bi_ref, lam_ref,
                hb_ref, xs_ref, xc_ref, rp_ref, ip_ref, a_ref, b_ref, hl_ref, pp_ref, h_ref)
    while units:
        unit(*units.pop(0))


BF16_SUBLANES = 2 * SUBLANES


def _cast_slice_spec(rows, cols, n_steps):
    repeat = next(k for k in (1, 2, 4, 8)
                  if rows * k % n_steps == 0 and (rows * k // n_steps) % BF16_SUBLANES == 0)
    return pl.BlockSpec((rows * repeat // n_steps, cols), lambda i: (i // repeat, 0))


def _inproj(x2, g, w, cw, cb, wr, br, wi, bi, lam, later_weights, seq):
    t = x2.shape[0]
    tm = RG_TILE
    n_steps = t // tm
    n_diag = D_MODEL // MXU_DIM
    scan_scratch = pltpu.VMEM((D_MODEL // LANES, SUBLANES * SEG_PITCH, LANES), F32)
    slab_scratch = pltpu.VMEM((tm, MXU_DIM), F32)
    cast_specs = [_cast_slice_spec(*wt.shape, n_steps) for wt in later_weights]
    outs = pl.pallas_call(
        functools.partial(_inproj_kernel, tiles_per_seq=seq // tm, n_cast=len(later_weights)),
        grid=(n_steps,),
        in_specs=[
            pl.BlockSpec((tm, D_MODEL), lambda i: (i, 0)),
            _resident((1, D_MODEL)),
            _resident((D_MODEL, 7 * D_MODEL)),
            _resident((CONV_W, D_MODEL)),
            _resident((1, D_MODEL)),
            _resident((n_diag, MXU_DIM, MXU_DIM)),
            _resident((1, D_MODEL)),
            _resident((n_diag, MXU_DIM, MXU_DIM)),
            _resident((1, D_MODEL)),
            _resident((1, D_MODEL)),
            *cast_specs,
        ],
        out_specs=[
            pl.BlockSpec((3 * N_PAIRS, tm, HEAD_PAIR), lambda i: (0, i, 0)),
            pl.BlockSpec((tm, 2 * D_MODEL), lambda i: (i, 0)),
            pl.BlockSpec((tm, D_MODEL), lambda i: (i, 0)),
            *cast_specs,
        ],
        out_shape=[
            jax.ShapeDtypeStruct((3 * N_PAIRS, t, HEAD_PAIR), BF16),
            jax.ShapeDtypeStruct((t, 2 * D_MODEL), BF16),
            jax.ShapeDtypeStruct((t, D_MODEL), BF16),
            *[jax.ShapeDtypeStruct(wt.shape, BF16) for wt in later_weights],
        ],
        scratch_shapes=[
            pltpu.VMEM((tm, D_MODEL), BF16),
            pltpu.VMEM((tm, D_MODEL), F32),
            pltpu.VMEM((D_MODEL // LANES, tm + SUBLANES, LANES), F32),
            slab_scratch, slab_scratch, slab_scratch,
            scan_scratch, scan_scratch, scan_scratch, scan_scratch,
            pltpu.VMEM((SUBLANES, D_MODEL), F32),
        ],
        compiler_params=pltpu.CompilerParams(
            dimension_semantics=("arbitrary",), vmem_limit_bytes=VMEM_LIMIT),
        name="inproj_rglru",
    )(x2, g, w, cw, cb, wr, br, wi, bi, lam, *later_weights)
    return outs[0], outs[1], outs[2], outs[3:]


def _band_geometry(ntiles):
    kw = ntiles * GROUP
    return kw, BAND - kw


def _attn_scores(q_ref, k_ref, s_ref, r0, ntiles):
    kw, c_off = _band_geometry(ntiles)
    lane = lax.broadcasted_iota(jnp.int32, (GROUP, HEAD_PAIR), 1)
    first_head = lane < HEAD_DIM
    qp = q_ref[0, pl.ds(r0, GROUP), :] * jnp.asarray(1.0 / math.sqrt(HEAD_DIM), BF16)
    zero = jnp.zeros_like(qp)
    qq = jnp.concatenate([jnp.where(first_head, qp, zero),
                          jnp.where(first_head, zero, qp)], axis=0)
    kk = k_ref[0, pl.ds(r0 + GROUP - kw, kw), :]
    s_ref[:, c_off:] = lax.dot_general(qq, kk, (((1,), (1,)), ((), ())),
                                       preferred_element_type=F32)


def _attn_softmax(s_ref, bias_ref, p_ref, l_ref, ntiles):
    _, c_off = _band_geometry(ntiles)
    rb = 32
    lane_rb = lax.broadcasted_iota(jnp.int32, (rb, LANES), 1)
    for hh in range(2):
        for i in range(GROUP // CHUNK):
            lo_col = max(i * CHUNK, c_off)
            hi_col = i * CHUNK + (LEFT_CHUNKS + 1) * CHUNK
            ws = (lo_col // LANES) * LANES
            we = -(-hi_col // LANES) * LANES
            nv = (we - ws) // LANES
            rows_all = slice(hh * GROUP + i * CHUNK, hh * GROUP + (i + 1) * CHUNK)
            if ws > c_off:
                p_ref[rows_all, c_off:ws] = jnp.zeros((CHUNK, ws - c_off), BF16)
            if we < BAND:
                p_ref[rows_all, we:BAND] = jnp.zeros((CHUNK, BAND - we), BF16)
            for b in range(CHUNK // rb):
                rows = slice(hh * GROUP + i * CHUNK + b * rb, hh * GROUP + i * CHUNK + (b + 1) * rb)
                brows = slice(i * CHUNK + b * rb, i * CHUNK + (b + 1) * rb)
                x = s_ref[rows, ws:we] + bias_ref[hh, brows, ws:we]
                cols = [x[:, c * LANES:(c + 1) * LANES] for c in range(nv)]
                if lo_col % LANES:
                    cols[0] = jnp.where(lane_rb >= HEAD_DIM, cols[0], NEG_INF)
                if hi_col % LANES:
                    cols[-1] = jnp.where(lane_rb < HEAD_DIM, cols[-1], NEG_INF)
                x = jnp.concatenate(cols, axis=1)
                m = jnp.max(x, axis=-1, keepdims=True)
                e = jnp.exp(x - m)
                l = jnp.sum(e, axis=-1, keepdims=True)
                p_ref[rows, ws:we] = e.astype(BF16)
                l_ref[rows, :] = jnp.broadcast_to(l, (rb, LANES))

def _attn_values(p_ref, l_ref, v_ref, o_ref, r0, ntiles):
    kw, c_off = _band_geometry(ntiles)
    lane = lax.broadcasted_iota(jnp.int32, (GROUP, HEAD_PAIR), 1)
    vv = v_ref[0, pl.ds(r0 + GROUP - kw, kw), :]
    o2 = jnp.dot(p_ref[:, c_off:], vv, preferred_element_type=F32)
    o2 = o2 / l_ref[...]
    o = jnp.where(lane < HEAD_DIM, o2[:GROUP], o2[GROUP:])
    o_ref[pl.ds(r0, GROUP), :] = o.astype(BF16)


def _attn_kernel(q_ref, k_ref, v_ref, w_ref, o_ref, bias_ref, s0, s1, p0, p1, l0, l1, *,
                 groups_per_seq):
    n_groups = q_ref.shape[1] // GROUP
    full = BAND_TILES
    row0 = lambda g: pl.multiple_of(g * GROUP, GROUP)

    @pl.when(pl.program_id(1) == 0)
    def _():
        for hh in range(2):
            diag = jnp.broadcast_to(w_ref[hh], (GROUP, DIAG_LANES))
            rolled = pltpu.roll(diag, DIAG_LANES - (GROUP - 1), 1, stride=1, stride_axis=0)
            bias_ref[hh] = rolled[:, :BAND]

    def scores(g, s_ref, ntiles=full):
        _attn_scores(q_ref, k_ref, s_ref, row0(g), ntiles)

    def softmax(s_ref, p_ref, l_ref, ntiles=full):
        _attn_softmax(s_ref, bias_ref, p_ref, l_ref, ntiles)

    def values(g, p_ref, l_ref, ntiles=full):
        _attn_values(p_ref, l_ref, v_ref, o_ref, row0(g), ntiles)

    bufs = ((s0, p0, l0), (s1, p1, l1))
    ntiles = lambda g: min(g % groups_per_seq + 1, full)
    for t in range(n_groups + 2):
        if t >= 2:
            _, p_ref, l_ref = bufs[t % 2]
            values(t - 2, p_ref, l_ref, ntiles(t - 2))
        if 1 <= t <= n_groups:
            softmax(*bufs[(t - 1) % 2], ntiles(t - 1))
        if t < n_groups:
            scores(t, bufs[t % 2][0], ntiles(t))


def _attention(qkv, diagonals, batch, seq):
    t = batch * seq
    rows = ATTN_SEQS_PER_STEP * seq
    return pl.pallas_call(
        functools.partial(_attn_kernel, groups_per_seq=seq // GROUP),
        grid=(N_PAIRS, batch // ATTN_SEQS_PER_STEP),
        in_specs=[
            pl.BlockSpec((1, rows, HEAD_PAIR), lambda hp, b: (hp, b, 0)),
            pl.BlockSpec((1, rows, HEAD_PAIR), lambda hp, b: (N_PAIRS + hp, b, 0)),
            pl.BlockSpec((1, rows, HEAD_PAIR), lambda hp, b: (2 * N_PAIRS + hp, b, 0)),
            pl.BlockSpec((2, 1, DIAG_LANES), lambda hp, b: (hp, 0, 0)),
        ],
        out_specs=pl.BlockSpec((rows, HEAD_PAIR), lambda hp, b: (b, hp)),
        out_shape=jax.ShapeDtypeStruct((t, D_MODEL), BF16),
        scratch_shapes=[
            pltpu.VMEM((2, GROUP, BAND), F32),
            pltpu.VMEM((2 * GROUP, BAND), F32),
            pltpu.VMEM((2 * GROUP, BAND), F32),
            pltpu.VMEM((2 * GROUP, BAND), BF16),
            pltpu.VMEM((2 * GROUP, BAND), BF16),
            pltpu.VMEM((2 * GROUP, LANES), F32),
            pltpu.VMEM((2 * GROUP, LANES), F32),
        ],
        compiler_params=pltpu.CompilerParams(
            dimension_semantics=("parallel", "arbitrary"), vmem_limit_bytes=VMEM_LIMIT),
        name="band_attention",
    )(qkv, qkv, qkv, diagonals)


def _band_bias_diagonals(rel_table):
    j = jnp.arange(DIAG_LANES)
    dist = (GROUP - 1 - j) + (BAND - GROUP)
    w = rel_table[:, jnp.clip(dist, -MAX_REL, MAX_REL) + MAX_REL]
    return w[:, None, :]


def _gelu_tanh(x):
    k = math.sqrt(2.0 / math.pi)
    inner = x * (k + (k * 0.044715) * (x * x))
    return x * (0.5 + 0.5 * jnp.tanh(inner))


def _rglru_tile(xr, gr_ref, side_work, cw_ref, cb_ref, wr_ref, br_ref, wi_ref, bi_ref, lam_ref,
                o_ref, xs_ref, xc_ref, rp_ref, ip_ref, a_ref, b_ref, hl_ref, pp_ref, h_ref):
    tm = xr.shape[0]
    seg = tm // SUBLANES
    n_lane_tiles = D_MODEL // LANES
    tiles_per_slab = MXU_DIM // LANES
    lane_tile = lambda c: slice(c * LANES, (c + 1) * LANES)

    for c in range(n_lane_tiles):
        xs_ref[c, SUBLANES:, :] = xr[:, lane_tile(c)]

    for d in range(D_MODEL // MXU_DIM):
        for k in range(tiles_per_slab):
            c = d * tiles_per_slab + k
            for j in range(SUBLANES):
                acc = cb_ref[:, lane_tile(c)]
                for tap in range(CONV_W):
                    off = SUBLANES - (CONV_W - 1) + tap + j * seg
                    acc = acc + xs_ref[c, off:off + seg, :] * cw_ref[tap:tap + 1, lane_tile(c)]
                xc_ref[j * seg:(j + 1) * seg, lane_tile(k)] = acc
                side_work(COST_CONV)
            xs_ref[c, 0:SUBLANES, :] = xs_ref[c, tm:tm + SUBLANES, :]
        xcb = xc_ref[...].astype(BF16)
        rp_ref[...] = jnp.dot(xcb, wr_ref[d], preferred_element_type=F32)
        ip_ref[...] = jnp.dot(xcb, wi_ref[d], preferred_element_type=F32)
        for k in range(tiles_per_slab):
            c = d * tiles_per_slab + k
            nlam = -lam_ref[:, lane_tile(c)]
            softplus = jnp.maximum(nlam, 0.0) + jnp.log1p(jnp.exp(-jnp.abs(nlam)))
            log_a_scale = (-RG_C) * softplus
            for j in range(SUBLANES):
                rows = slice(j * seg, (j + 1) * seg)
                xc = xc_ref[rows, lane_tile(k)]
                r = jax.nn.sigmoid(rp_ref[rows, lane_tile(k)] + br_ref[:, lane_tile(c)])
                ig = jax.nn.sigmoid(ip_ref[rows, lane_tile(k)] + bi_ref[:, lane_tile(c)])
                log_a = r * log_a_scale
                a = jnp.exp(log_a)
                y = jnp.tanh(log_a) * (-1.0 - a * a)
                mult = jnp.where(y > 0.0, y * lax.rsqrt(y), 0.0)
                srows = slice(j * SEG_PITCH, j * SEG_PITCH + seg)
                a_ref[c, srows, :] = a
                b_ref[c, srows, :] = mult * (ig * xc)
                side_work(COST_GATE)

    hs = [jnp.zeros((SUBLANES, LANES), F32) for _ in range(n_lane_tiles)]
    ps = [jnp.ones((SUBLANES, LANES), F32) for _ in range(n_lane_tiles)]
    for t in range(seg):
        idx = pl.ds(t, SUBLANES, stride=SEG_PITCH)
        for c in range(n_lane_tiles):
            a_t = a_ref[c, idx, :]
            hs[c] = a_t * hs[c] + b_ref[c, idx, :]
            ps[c] = a_t * ps[c]
            hl_ref[c, idx, :] = hs[c]
            pp_ref[c, idx, :] = ps[c]
        side_work(COST_SCAN_STEP)
    h_end = jnp.concatenate(hs, axis=1)
    p_end = jnp.concatenate(ps, axis=1)

    h_in = h_ref[0:1, :]
    for j in range(SUBLANES):
        rows = slice(j * seg, (j + 1) * seg)
        srows = slice(j * SEG_PITCH, j * SEG_PITCH + seg)
        for c in range(n_lane_tiles):
            hj = hl_ref[c, srows, :] + pp_ref[c, srows, :] * h_in[:, lane_tile(c)]
            o_ref[rows, lane_tile(c)] = (hj * _gelu_tanh(gr_ref[rows, lane_tile(c)])).astype(BF16)
            side_work(COST_OUT)
        h_in = p_end[j:j + 1, :] * h_in + h_end[j:j + 1, :]
    h_ref[0:1, :] = h_in


def _block_diag_tiles(w):
    per = MXU_DIM // RG_BLOCK
    n_diag = D_MODEL // MXU_DIM
    w4 = w.reshape(n_diag, per, RG_BLOCK, RG_BLOCK)
    eye = jnp.eye(per, dtype=w.dtype)
    tiles = jnp.einsum('dpij,pq->dpiqj', w4, eye)
    return tiles.reshape(n_diag, MXU_DIM, MXU_DIM)


def _mixffn_kernel(x_ref, at_ref, hb_ref, ga_ref, gb_ref, bm_ref, wa_ref, wb_ref, wo_ref,
                   gn_ref, win_ref, wout_ref, gf_ref, o_ref, *, splits):
    ya = jnp.dot(at_ref[...], wa_ref[...], preferred_element_type=F32)
    yb = jnp.dot(hb_ref[...], wb_ref[...], preferred_element_type=F32)
    g_a = jax.nn.sigmoid(ga_ref[...].astype(F32) + bm_ref[:, :D_MODEL])
    g_b = jax.nn.sigmoid(gb_ref[...].astype(F32) + bm_ref[:, D_MODEL:])
    mix = (g_a * ya + g_b * yb).astype(BF16)
    h = x_ref[...] + jnp.dot(mix, wo_ref[...], preferred_element_type=F32)
    hn = _rms(h, gn_ref[...]).astype(BF16)
    d_ff = wout_ref.shape[0]
    acc = h
    for c0, c1 in splits:
        g = jnp.dot(hn, win_ref[:, c0:c1], preferred_element_type=F32)
        up = jnp.dot(hn, win_ref[:, d_ff + c0:d_ff + c1], preferred_element_type=F32)
        act = (jax.nn.silu(g) * up).astype(BF16)
        acc = acc + jnp.dot(act, wout_ref[c0:c1, :], preferred_element_type=F32)
    o_ref[...] = _rms(acc, gf_ref[...])


def _mixffn(x2, attn, hb, gates, bm, wa, wb, wo, gn, win, wout, gf, tm):
    t = x2.shape[0]
    d_ff = wout.shape[0]
    n_tiles = d_ff // MXU_DIM
    half = (n_tiles + 1) // 2 * MXU_DIM
    splits = ((0, half), (half, d_ff))
    row = lambda i: (i, 0)
    return pl.pallas_call(
        functools.partial(_mixffn_kernel, splits=splits),
        grid=(t // tm,),
        in_specs=[
            pl.BlockSpec((tm, D_MODEL), row),
            pl.BlockSpec((tm, D_MODEL), row),
            pl.BlockSpec((tm, D_MODEL), row),
            pl.BlockSpec((tm, D_MODEL), lambda i: (i, 0)),
            pl.BlockSpec((tm, D_MODEL), lambda i: (i, 1)),
            _resident((1, 2 * D_MODEL)),
            _resident((D_MODEL, D_MODEL)),
            _resident((D_MODEL, D_MODEL)),
            _resident((D_MODEL, D_MODEL)),
            _resident((1, D_MODEL)),
            _resident((D_MODEL, 2 * d_ff)),
            _resident((d_ff, D_MODEL)),
            _resident((1, D_MODEL)),
        ],
        out_specs=pl.BlockSpec((tm, D_MODEL), row),
        out_shape=jax.ShapeDtypeStruct((t, D_MODEL), F32),
        compiler_params=pltpu.CompilerParams(
            dimension_semantics=("parallel",), vmem_limit_bytes=VMEM_LIMIT),
        name="mix_ffn",
    )(x2, attn, hb, gates, gates, bm, wa, wb, wo, gn, win, wout, gf)


def kernel(x, norm_mix_g, w_in, b_merge, rel_table, w_attn_out, conv_w, conv_b, w_rg_r, b_rg_r,
           w_rg_i, b_rg_i, rg_lambda, w_rnn_out, w_o, norm_ffn_g, w_ffn_in, w_ffn_out, final_norm_g):
    batch, seq, d = x.shape
    assert d == D_MODEL and seq % GROUP == 0 and seq // GROUP >= BAND_TILES
    assert w_in.shape[0] == 1, "single-layer block"
    x2 = x.reshape(batch * seq, d)
    row = lambda v: v.reshape(1, -1)

    assert seq % RG_TILE == 0 and batch % ATTN_SEQS_PER_STEP == 0
    later_weights = (w_attn_out[0], w_rnn_out[0], w_o[0], w_ffn_in[0], w_ffn_out[0])
    qkv, gates, hb, (wa, wb, wo, wfi, wfo) = _inproj(
        x2, row(norm_mix_g[0]), w_in[0].astype(BF16), conv_w[0], row(conv_b[0]),
        _block_diag_tiles(w_rg_r[0]).astype(BF16), row(b_rg_r[0]),
        _block_diag_tiles(w_rg_i[0]).astype(BF16), row(b_rg_i[0]), row(rg_lambda[0]),
        later_weights, seq)
    attn = _attention(qkv, _band_bias_diagonals(rel_table[0]), batch, seq)
    out = _mixffn(x2, attn, hb, gates, row(b_merge[0]), wa, wb, wo, row(norm_ffn_g[0]), wfi, wfo,
                  row(final_norm_g), tm=512)
    return out.reshape(batch, seq, d)
```

```python
import functools
import math

import jax
import jax.numpy as jnp
from jax import lax
from jax.experimental import pallas as pl
from jax.experimental.pallas import tpu as pltpu

F32 = jnp.float32
BF16 = jnp.bfloat16

D_MODEL = 1024
CHUNK = 64
LEFT_CHUNKS = 8
N_HEADS = 16
HEAD_DIM = 64
MAX_REL = 128
N_RG_BLOCKS = 16
RG_BLOCK = D_MODEL // N_RG_BLOCKS
CONV_W = 4
RG_C = 8.0
EPS = 1e-6
NEG_INF = -1e30

LANES = 128
SUBLANES = 8
MXU_DIM = 256
HEAD_PAIR = 2 * HEAD_DIM
N_PAIRS = N_HEADS // 2
GROUP = 4 * CHUNK
BAND_TILES = LEFT_CHUNKS * CHUNK // GROUP + 1
BAND = BAND_TILES * GROUP
DIAG_LANES = -(-(GROUP + BAND - 1) // LANES) * LANES
VMEM_LIMIT = 56 * 1024 * 1024
RG_TILE = 512
SEG_PITCH = RG_TILE // SUBLANES + SUBLANES // 2
COST_CONV, COST_GATE, COST_SCAN_STEP, COST_OUT = 16, 52, 23, 24


def _rms(x, g):
    var = jnp.mean(x * x, axis=-1, keepdims=True)
    return x * lax.rsqrt(var + EPS) * g


def _resident(shape):
    zeros = (0,) * len(shape)
    return pl.BlockSpec(shape, lambda *_: zeros, pipeline_mode=pl.Buffered(1))


def _inproj_kernel(*refs, tiles_per_seq, n_cast):
    (x_ref, g_ref, w_ref, cw_ref, cb_ref, wr_ref, br_ref, wi_ref, bi_ref, lam_ref), refs = refs[:10], refs[10:]
    cast_in, refs = refs[:n_cast], refs[n_cast:]
    (qkv_ref, gates_ref, hb_ref), refs = refs[:3], refs[3:]
    cast_out, refs = refs[:n_cast], refs[n_cast:]
    xn_ref, gr_ref, xs_ref, xc_ref, rp_ref, ip_ref, a_ref, b_ref, hl_ref, pp_ref, h_ref = refs

    for src_ref, dst_ref in zip(cast_in, cast_out):
        dst_ref[...] = src_ref[...].astype(BF16)

    @pl.when(pl.program_id(0) % tiles_per_seq == 0)
    def _():
        xs_ref[:, 0:SUBLANES, :] = jnp.zeros((D_MODEL // LANES, SUBLANES, LANES), F32)
        h_ref[...] = jnp.zeros_like(h_ref)

    xn_ref[...] = _rms(x_ref[...], g_ref[...]).astype(BF16)

    def u_cols(c0, width):
        return jnp.dot(xn_ref[...], w_ref[:, c0:c0 + width], preferred_element_type=F32)

    def unit(j, n):
        acc = u_cols(j * D_MODEL + n * MXU_DIM, MXU_DIM)
        if j < 3:
            for k in range(MXU_DIM // HEAD_PAIR):
                hp = n * (MXU_DIM // HEAD_PAIR) + k
                qkv_ref[j * N_PAIRS + hp] = acc[:, k * HEAD_PAIR:(k + 1) * HEAD_PAIR].astype(BF16)
        elif j == 4:
            gr_ref[:, n * MXU_DIM:(n + 1) * MXU_DIM] = acc
        else:
            c0 = (j - 5) * D_MODEL + n * MXU_DIM
            gates_ref[:, c0:c0 + MXU_DIM] = acc.astype(BF16)

    units = [(j, n) for j in (4, 0, 1, 2, 5, 6) for n in range(D_MODEL // MXU_DIM)]
    n_pieces = SUBLANES * (D_MODEL // LANES)
    total_cost = n_pieces * (COST_CONV + COST_GATE + COST_OUT) + (x_ref.shape[0] // SUBLANES) * COST_SCAN_STEP
    cost_per_unit = total_cost / len(units)
    spent = [0.0]

    def side_work(cost):
        spent[0] += cost
        while units and spent[0] >= cost_per_unit:
            spent[0] -= cost_per_unit
            unit(*units.pop(0))

    _rglru_tile(u_cols(3 * D_MODEL, D_MODEL), gr_ref, side_work,
                cw_ref, cb_ref, wr_ref, br_ref, wi_ref, bi_ref, lam_ref,
                hb_ref, xs_ref, xc_ref, rp_ref, ip_ref, a_ref, b_ref, hl_ref, pp_ref, h_ref)
    while units:
        unit(*units.pop(0))


BF16_SUBLANES = 2 * SUBLANES


def _cast_slice_spec(rows, cols, n_steps):
    repeat = next(k for k in (1, 2, 4, 8)
                  if rows * k % n_steps == 0 and (rows * k // n_steps) % BF16_SUBLANES == 0)
    return pl.BlockSpec((rows * repeat // n_steps, cols), lambda i: (i // repeat, 0))


def _inproj(x2, g, w, cw, cb, wr, br, wi, bi, lam, later_weights, seq):
    t = x2.shape[0]
    tm = RG_TILE
    n_steps = t // tm
    n_diag = D_MODEL // MXU_DIM
    scan_scratch = pltpu.VMEM((D_MODEL // LANES, SUBLANES * SEG_PITCH, LANES), F32)
    slab_scratch = pltpu.VMEM((tm, MXU_DIM), F32)
    cast_specs = [_cast_slice_spec(*wt.shape, n_steps) for wt in later_weights]
    outs = pl.pallas_call(
        functools.partial(_inproj_kernel, tiles_per_seq=seq // tm, n_cast=len(later_weights)),
        grid=(n_steps,),
        in_specs=[
            pl.BlockSpec((tm, D_MODEL), lambda i: (i, 0)),
            _resident((1, D_MODEL)),
            _resident((D_MODEL, 7 * D_MODEL)),
            _resident((CONV_W, D_MODEL)),
            _resident((1, D_MODEL)),
            _resident((n_diag, MXU_DIM, MXU_DIM)),
            _resident((1, D_MODEL)),
            _resident((n_diag, MXU_DIM, MXU_DIM)),
            _resident((1, D_MODEL)),
            _resident((1, D_MODEL)),
            *cast_specs,
        ],
        out_specs=[
            pl.BlockSpec((3 * N_PAIRS, tm, HEAD_PAIR), lambda i: (0, i, 0)),
            pl.BlockSpec((tm, 2 * D_MODEL), lambda i: (i, 0)),
            pl.BlockSpec((tm, D_MODEL), lambda i: (i, 0)),
            *cast_specs,
        ],
        out_shape=[
            jax.ShapeDtypeStruct((3 * N_PAIRS, t, HEAD_PAIR), BF16),
            jax.ShapeDtypeStruct((t, 2 * D_MODEL), BF16),
            jax.ShapeDtypeStruct((t, D_MODEL), BF16),
            *[jax.ShapeDtypeStruct(wt.shape, BF16) for wt in later_weights],
        ],
        scratch_shapes=[
            pltpu.VMEM((tm, D_MODEL), BF16),
            pltpu.VMEM((tm, D_MODEL), F32),
            pltpu.VMEM((D_MODEL // LANES, tm + SUBLANES, LANES), F32),
            slab_scratch, slab_scratch, slab_scratch,
            scan_scratch, scan_scratch, scan_scratch, scan_scratch,
            pltpu.VMEM((SUBLANES, D_MODEL), F32),
        ],
        compiler_params=pltpu.CompilerParams(
            dimension_semantics=("arbitrary",), vmem_limit_bytes=VMEM_LIMIT),
        name="inproj_rglru",
    )(x2, g, w, cw, cb, wr, br, wi, bi, lam, *later_weights)
    return outs[0], outs[1], outs[2], outs[3:]


def _band_geometry(ntiles):
    kw = ntiles * GROUP
    return kw, BAND - kw


def _attn_scores(q_ref, k_ref, s_ref, r0, ntiles):
    kw, c_off = _band_geometry(ntiles)
    lane = lax.broadcasted_iota(jnp.int32, (GROUP, HEAD_PAIR), 1)
    first_head = lane < HEAD_DIM
    qp = q_ref[0, pl.ds(r0, GROUP), :] * jnp.asarray(1.0 / math.sqrt(HEAD_DIM), BF16)
    zero = jnp.zeros_like(qp)
    qq = jnp.concatenate([jnp.where(first_head, qp, zero),
                          jnp.where(first_head, zero, qp)], axis=0)
    kk = k_ref[0, pl.ds(r0 + GROUP - kw, kw), :]
    s_ref[:, c_off:] = lax.dot_general(qq, kk, (((1,), (1,)), ((), ())),
                                       preferred_element_type=F32)


def _attn_softmax(s_ref, bias_ref, p_ref, l_ref, ntiles):
    _, c_off = _band_geometry(ntiles)
    rb = 32
    lane_rb = lax.broadcasted_iota(jnp.int32, (rb, LANES), 1)
    for hh in range(2):
        for i in range(GROUP // CHUNK):
            lo_col = max(i * CHUNK, c_off)
            hi_col = i * CHUNK + (LEFT_CHUNKS + 1) * CHUNK
            ws = (lo_col // LANES) * LANES
            we = -(-hi_col // LANES) * LANES
            nv = (we - ws) // LANES
            rows_all = slice(hh * GROUP + i * CHUNK, hh * GROUP + (i + 1) * CHUNK)
            if ws > c_off:
                p_ref[rows_all, c_off:ws] = jnp.zeros((CHUNK, ws - c_off), BF16)
            if we < BAND:
                p_ref[rows_all, we:BAND] = jnp.zeros((CHUNK, BAND - we), BF16)
            for b in range(CHUNK // rb):
                rows = slice(hh * GROUP + i * CHUNK + b * rb, hh * GROUP + i * CHUNK + (b + 1) * rb)
                brows = slice(i * CHUNK + b * rb, i * CHUNK + (b + 1) * rb)
                x = s_ref[rows, ws:we] + bias_ref[hh, brows, ws:we]
                cols = [x[:, c * LANES:(c + 1) * LANES] for c in range(nv)]
                if lo_col % LANES:
                    cols[0] = jnp.where(lane_rb >= HEAD_DIM, cols[0], NEG_INF)
                if hi_col % LANES:
                    cols[-1] = jnp.where(lane_rb < HEAD_DIM, cols[-1], NEG_INF)
                x = jnp.concatenate(cols, axis=1)
                m = jnp.max(x, axis=-1, keepdims=True)
                e = jnp.exp(x - m)
                l = jnp.sum(e, axis=-1, keepdims=True)
                p_ref[rows, ws:we] = e.astype(BF16)
                l_ref[rows, :] = jnp.broadcast_to(l, (rb, LANES))

def _attn_values(p_ref, l_ref, v_ref, o_ref, r0, ntiles):
    kw, c_off = _band_geometry(ntiles)
    lane = lax.broadcasted_iota(jnp.int32, (GROUP, HEAD_PAIR), 1)
    vv = v_ref[0, pl.ds(r0 + GROUP - kw, kw), :]
    o2 = jnp.dot(p_ref[:, c_off:], vv, preferred_element_type=F32)
    o2 = o2 / l_ref[...]
    o = jnp.where(lane < HEAD_DIM, o2[:GROUP], o2[GROUP:])
    o_ref[0, pl.ds(r0, GROUP), :] = o.astype(BF16)


def _attn_kernel(q_ref, k_ref, v_ref, w_ref, o_ref, bias_ref, s0, s1, p0, p1, l0, l1):
    n_groups = q_ref.shape[1] // GROUP
    full = BAND_TILES
    row0 = lambda g: pl.multiple_of(g * GROUP, GROUP)

    @pl.when(pl.program_id(1) == 0)
    def _():
        for hh in range(2):
            diag = jnp.broadcast_to(w_ref[hh], (GROUP, DIAG_LANES))
            rolled = pltpu.roll(diag, DIAG_LANES - (GROUP - 1), 1, stride=1, stride_axis=0)
            bias_ref[hh] = rolled[:, :BAND]

    def scores(g, s_ref, ntiles=full):
        _attn_scores(q_ref, k_ref, s_ref, row0(g), ntiles)

    def softmax(s_ref, p_ref, l_ref, ntiles=full):
        _attn_softmax(s_ref, bias_ref, p_ref, l_ref, ntiles)

    def values(g, p_ref, l_ref, ntiles=full):
        _attn_values(p_ref, l_ref, v_ref, o_ref, row0(g), ntiles)

    bufs = ((s0, p0, l0), (s1, p1, l1))
    ntiles = lambda g: min(g + 1, full)
    for t in range(n_groups + 2):
        if t >= 2:
            _, p_ref, l_ref = bufs[t % 2]
            values(t - 2, p_ref, l_ref, ntiles(t - 2))
        if 1 <= t <= n_groups:
            softmax(*bufs[(t - 1) % 2], ntiles(t - 1))
        if t < n_groups:
            scores(t, bufs[t % 2][0], ntiles(t))


def _attention(qkv, diagonals, batch, seq):
    t = batch * seq
    return pl.pallas_call(
        _attn_kernel,
        grid=(N_PAIRS, batch),
        in_specs=[
            pl.BlockSpec((1, seq, HEAD_PAIR), lambda hp, b: (hp, b, 0)),
            pl.BlockSpec((1, seq, HEAD_PAIR), lambda hp, b: (N_PAIRS + hp, b, 0)),
            pl.BlockSpec((1, seq, HEAD_PAIR), lambda hp, b: (2 * N_PAIRS + hp, b, 0)),
            pl.BlockSpec((2, 1, DIAG_LANES), lambda hp, b: (hp, 0, 0)),
        ],
        out_specs=pl.BlockSpec((1, seq, HEAD_PAIR), lambda hp, b: (hp, b, 0)),
        out_shape=jax.ShapeDtypeStruct((N_PAIRS, t, HEAD_PAIR), BF16),
        scratch_shapes=[
            pltpu.VMEM((2, GROUP, BAND), F32),
            pltpu.VMEM((2 * GROUP, BAND), F32),
            pltpu.VMEM((2 * GROUP, BAND), F32),
            pltpu.VMEM((2 * GROUP, BAND), BF16),
            pltpu.VMEM((2 * GROUP, BAND), BF16),
            pltpu.VMEM((2 * GROUP, LANES), F32),
            pltpu.VMEM((2 * GROUP, LANES), F32),
        ],
        compiler_params=pltpu.CompilerParams(
            dimension_semantics=("parallel", "arbitrary"), vmem_limit_bytes=VMEM_LIMIT),
        name="band_attention",
    )(qkv, qkv, qkv, diagonals)


def _band_bias_diagonals(rel_table):
    j = jnp.arange(DIAG_LANES)
    dist = (GROUP - 1 - j) + (BAND - GROUP)
    w = rel_table[:, jnp.clip(dist, -MAX_REL, MAX_REL) + MAX_REL]
    return w[:, None, :]


def _gelu_tanh(x):
    k = math.sqrt(2.0 / math.pi)
    inner = x * (k + (k * 0.044715) * (x * x))
    return x * (0.5 + 0.5 * jnp.tanh(inner))


def _rglru_tile(xr, gr_ref, side_work, cw_ref, cb_ref, wr_ref, br_ref, wi_ref, bi_ref, lam_ref,
                o_ref, xs_ref, xc_ref, rp_ref, ip_ref, a_ref, b_ref, hl_ref, pp_ref, h_ref):
    tm = xr.shape[0]
    seg = tm // SUBLANES
    n_lane_tiles = D_MODEL // LANES
    tiles_per_slab = MXU_DIM // LANES
    lane_tile = lambda c: slice(c * LANES, (c + 1) * LANES)

    for c in range(n_lane_tiles):
        xs_ref[c, SUBLANES:, :] = xr[:, lane_tile(c)]

    for d in range(D_MODEL // MXU_DIM):
        for k in range(tiles_per_slab):
            c = d * tiles_per_slab + k
            for j in range(SUBLANES):
                acc = cb_ref[:, lane_tile(c)]
                for tap in range(CONV_W):
                    off = SUBLANES - (CONV_W - 1) + tap + j * seg
                    acc = acc + xs_ref[c, off:off + seg, :] * cw_ref[tap:tap + 1, lane_tile(c)]
                xc_ref[j * seg:(j + 1) * seg, lane_tile(k)] = acc
                side_work(COST_CONV)
            xs_ref[c, 0:SUBLANES, :] = xs_ref[c, tm:tm + SUBLANES, :]
        xcb = xc_ref[...].astype(BF16)
        rp_ref[...] = jnp.dot(xcb, wr_ref[d], preferred_element_type=F32)
        ip_ref[...] = jnp.dot(xcb, wi_ref[d], preferred_element_type=F32)
        for k in range(tiles_per_slab):
            c = d * tiles_per_slab + k
            nlam = -lam_ref[:, lane_tile(c)]
            softplus = jnp.maximum(nlam, 0.0) + jnp.log1p(jnp.exp(-jnp.abs(nlam)))
            log_a_scale = (-RG_C) * softplus
            for j in range(SUBLANES):
                rows = slice(j * seg, (j + 1) * seg)
                xc = xc_ref[rows, lane_tile(k)]
                r = jax.nn.sigmoid(rp_ref[rows, lane_tile(k)] + br_ref[:, lane_tile(c)])
                ig = jax.nn.sigmoid(ip_ref[rows, lane_tile(k)] + bi_ref[:, lane_tile(c)])
                log_a = r * log_a_scale
                a = jnp.exp(log_a)
                y = jnp.tanh(log_a) * (-1.0 - a * a)
                mult = jnp.where(y > 0.0, y * lax.rsqrt(y), 0.0)
                srows = slice(j * SEG_PITCH, j * SEG_PITCH + seg)
                a_ref[c, srows, :] = a
                b_ref[c, srows, :] = mult * (ig * xc)
                side_work(COST_GATE)

    hs = [jnp.zeros((SUBLANES, LANES), F32) for _ in range(n_lane_tiles)]
    ps = [jnp.ones((SUBLANES, LANES), F32) for _ in range(n_lane_tiles)]
    for t in range(seg):
        idx = pl.ds(t, SUBLANES, stride=SEG_PITCH)
        for c in range(n_lane_tiles):
            a_t = a_ref[c, idx, :]
            hs[c] = a_t * hs[c] + b_ref[c, idx, :]
            ps[c] = a_t * ps[c]
            hl_ref[c, idx, :] = hs[c]
            pp_ref[c, idx, :] = ps[c]
        side_work(COST_SCAN_STEP)
    h_end = jnp.concatenate(hs, axis=1)
    p_end = jnp.concatenate(ps, axis=1)

    h_in = h_ref[0:1, :]
    for j in range(SUBLANES):
        rows = slice(j * seg, (j + 1) * seg)
        srows = slice(j * SEG_PITCH, j * SEG_PITCH + seg)
        for c in range(n_lane_tiles):
            hj = hl_ref[c, srows, :] + pp_ref[c, srows, :] * h_in[:, lane_tile(c)]
            o_ref[rows, lane_tile(c)] = (hj * _gelu_tanh(gr_ref[rows, lane_tile(c)])).astype(BF16)
            side_work(COST_OUT)
        h_in = p_end[j:j + 1, :] * h_in + h_end[j:j + 1, :]
    h_ref[0:1, :] = h_in


def _block_diag_tiles(w):
    per = MXU_DIM // RG_BLOCK
    n_diag = D_MODEL // MXU_DIM
    w4 = w.reshape(n_diag, per, RG_BLOCK, RG_BLOCK)
    eye = jnp.eye(per, dtype=w.dtype)
    tiles = jnp.einsum('dpij,pq->dpiqj', w4, eye)
    return tiles.reshape(n_diag, MXU_DIM, MXU_DIM)


def _mixffn_kernel(x_ref, at_ref, hb_ref, ga_ref, gb_ref, bm_ref, wa_ref, wb_ref, wo_ref,
                   gn_ref, win_ref, wout_ref, gf_ref, o_ref, *, splits):
    attn = jnp.concatenate([at_ref[hp] for hp in range(N_PAIRS)], axis=1)
    ya = jnp.dot(attn, wa_ref[...], preferred_element_type=F32)
    yb = jnp.dot(hb_ref[...], wb_ref[...], preferred_element_type=F32)
    g_a = jax.nn.sigmoid(ga_ref[...].astype(F32) + bm_ref[:, :D_MODEL])
    g_b = jax.nn.sigmoid(gb_ref[...].astype(F32) + bm_ref[:, D_MODEL:])
    mix = (g_a * ya + g_b * yb).astype(BF16)
    h = x_ref[...] + jnp.dot(mix, wo_ref[...], preferred_element_type=F32)
    hn = _rms(h, gn_ref[...]).astype(BF16)
    d_ff = wout_ref.shape[0]
    acc = h
    for c0, c1 in splits:
        g = jnp.dot(hn, win_ref[:, c0:c1], preferred_element_type=F32)
        up = jnp.dot(hn, win_ref[:, d_ff + c0:d_ff + c1], preferred_element_type=F32)
        act = (jax.nn.silu(g) * up).astype(BF16)
        acc = acc + jnp.dot(act, wout_ref[c0:c1, :], preferred_element_type=F32)
    o_ref[...] = _rms(acc, gf_ref[...])


def _mixffn(x2, attn, hb, gates, bm, wa, wb, wo, gn, win, wout, gf, tm):
    t = x2.shape[0]
    d_ff = wout.shape[0]
    n_tiles = d_ff // MXU_DIM
    half = (n_tiles + 1) // 2 * MXU_DIM
    splits = ((0, half), (half, d_ff))
    row = lambda i: (i, 0)
    return pl.pallas_call(
        functools.partial(_mixffn_kernel, splits=splits),
        grid=(t // tm,),
        in_specs=[
            pl.BlockSpec((tm, D_MODEL), row),
            pl.BlockSpec((N_PAIRS, tm, HEAD_PAIR), lambda i: (0, i, 0)),
            pl.BlockSpec((tm, D_MODEL), row),
            pl.BlockSpec((tm, D_MODEL), lambda i: (i, 0)),
            pl.BlockSpec((tm, D_MODEL), lambda i: (i, 1)),
            _resident((1, 2 * D_MODEL)),
            _resident((D_MODEL, D_MODEL)),
            _resident((D_MODEL, D_MODEL)),
            _resident((D_MODEL, D_MODEL)),
            _resident((1, D_MODEL)),
            _resident((D_MODEL, 2 * d_ff)),
            _resident((d_ff, D_MODEL)),
            _resident((1, D_MODEL)),
        ],
        out_specs=pl.BlockSpec((tm, D_MODEL), row),
        out_shape=jax.ShapeDtypeStruct((t, D_MODEL), F32),
        compiler_params=pltpu.CompilerParams(
            dimension_semantics=("parallel",), vmem_limit_bytes=VMEM_LIMIT),
        name="mix_ffn",
    )(x2, attn, hb, gates, gates, bm, wa, wb, wo, gn, win, wout, gf)


def kernel(x, norm_mix_g, w_in, b_merge, rel_table, w_attn_out, conv_w, conv_b, w_rg_r, b_rg_r,
           w_rg_i, b_rg_i, rg_lambda, w_rnn_out, w_o, norm_ffn_g, w_ffn_in, w_ffn_out, final_norm_g):
    batch, seq, d = x.shape
    assert d == D_MODEL and seq % GROUP == 0 and seq // GROUP >= BAND_TILES
    assert w_in.shape[0] == 1, "single-layer block"
    x2 = x.reshape(batch * seq, d)
    row = lambda v: v.reshape(1, -1)

    assert seq % RG_TILE == 0
    later_weights = (w_attn_out[0], w_rnn_out[0], w_o[0], w_ffn_in[0], w_ffn_out[0])
    qkv, gates, hb, (wa, wb, wo, wfi, wfo) = _inproj(
        x2, row(norm_mix_g[0]), w_in[0].astype(BF16), conv_w[0], row(conv_b[0]),
        _block_diag_tiles(w_rg_r[0]).astype(BF16), row(b_rg_r[0]),
        _block_diag_tiles(w_rg_i[0]).astype(BF16), row(b_rg_i[0]), row(rg_lambda[0]),
        later_weights, seq)
    attn = _attention(qkv, _band_bias_diagonals(rel_table[0]), batch, seq)
    out = _mixffn(x2, attn, hb, gates, row(b_merge[0]), wa, wb, wo, row(norm_ffn_g[0]), wfi, wfo,
                  row(final_norm_g), tm=512)
    return out.reshape(batch, seq, d)
```

```python
import functools
import math

import jax
import jax.numpy as jnp
from jax import lax
from jax.experimental import pallas as pl
from jax.experimental.pallas import tpu as pltpu

F32 = jnp.float32
BF16 = jnp.bfloat16

D_MODEL = 1024
CHUNK = 64
LEFT_CHUNKS = 8
N_HEADS = 16
HEAD_DIM = 64
MAX_REL = 128
N_RG_BLOCKS = 16
RG_BLOCK = D_MODEL // N_RG_BLOCKS
CONV_W = 4
RG_C = 8.0
EPS = 1e-6
NEG_INF = -1e30

LANES = 128
SUBLANES = 8
MXU_DIM = 256
HEAD_PAIR = 2 * HEAD_DIM
N_PAIRS = N_HEADS // 2
GROUP = 4 * CHUNK
BAND_TILES = LEFT_CHUNKS * CHUNK // GROUP + 1
BAND = BAND_TILES * GROUP
DIAG_LANES = -(-(GROUP + BAND - 1) // LANES) * LANES
VMEM_LIMIT = 56 * 1024 * 1024
RG_TILE = 512
W_STAGE_ROWS = 64
SEG_PITCH = RG_TILE // SUBLANES + SUBLANES // 2
COST_CONV, COST_GATE, COST_SCAN_STEP, COST_OUT = 16, 52, 23, 24


def _rms(x, g):
    var = jnp.mean(x * x, axis=-1, keepdims=True)
    return x * lax.rsqrt(var + EPS) * g


def _resident(shape):
    zeros = (0,) * len(shape)
    return pl.BlockSpec(shape, lambda *_: zeros, pipeline_mode=pl.Buffered(1))


def _load_rounded(src_hbm, dst_ref, stage_ref, sem):
    rows = stage_ref.shape[1]
    n_chunks = src_hbm.shape[0] // rows

    def chunk_copy(c):
        slot = c % 2
        return pltpu.make_async_copy(src_hbm.at[pl.ds(c * rows, rows), :], stage_ref.at[slot],
                                     sem.at[slot])

    chunk_copy(0).start()
    for c in range(n_chunks):
        if c + 1 < n_chunks:
            chunk_copy(c + 1).start()
        chunk_copy(c).wait()
        dst_ref[c * rows:(c + 1) * rows, :] = stage_ref[c % 2].astype(BF16)


def _inproj_kernel(*refs, tiles_per_seq, n_cast):
    (x_ref, g_ref, w_hbm, cw_ref, cb_ref, wr_ref, br_ref, wi_ref, bi_ref, lam_ref), refs = refs[:10], refs[10:]
    cast_in, refs = refs[:n_cast], refs[n_cast:]
    (qkv_ref, gates_ref, hb_ref), refs = refs[:3], refs[3:]
    cast_out, refs = refs[:n_cast], refs[n_cast:]
    (w_ref, stage_ref, stage_sem, xn_ref, gr_ref, xs_ref, xc_ref, rp_ref, ip_ref,
     a_ref, b_ref, hl_ref, pp_ref, h_ref) = refs

    @pl.when(pl.program_id(0) == 0)
    def _():
        _load_rounded(w_hbm, w_ref, stage_ref, stage_sem)

    for src_ref, dst_ref in zip(cast_in, cast_out):
        dst_ref[...] = src_ref[...].astype(BF16)

    @pl.when(pl.program_id(0) % tiles_per_seq == 0)
    def _():
        xs_ref[:, 0:SUBLANES, :] = jnp.zeros((D_MODEL // LANES, SUBLANES, LANES), F32)
        h_ref[...] = jnp.zeros_like(h_ref)

    xn_ref[...] = _rms(x_ref[...], g_ref[...]).astype(BF16)

    def u_cols(c0, width):
        return jnp.dot(xn_ref[...], w_ref[:, c0:c0 + width], preferred_element_type=F32)

    def unit(j, n):
        acc = u_cols(j * D_MODEL + n * MXU_DIM, MXU_DIM)
        if j < 3:
            for k in range(MXU_DIM // HEAD_PAIR):
                hp = n * (MXU_DIM // HEAD_PAIR) + k
                qkv_ref[j * N_PAIRS + hp] = acc[:, k * HEAD_PAIR:(k + 1) * HEAD_PAIR].astype(BF16)
        elif j == 4:
            gr_ref[:, n * MXU_DIM:(n + 1) * MXU_DIM] = acc
        else:
            c0 = (j - 5) * D_MODEL + n * MXU_DIM
            gates_ref[:, c0:c0 + MXU_DIM] = acc.astype(BF16)

    units = [(j, n) for j in (4, 0, 1, 2, 5, 6) for n in range(D_MODEL // MXU_DIM)]
    n_pieces = SUBLANES * (D_MODEL // LANES)
    total_cost = n_pieces * (COST_CONV + COST_GATE + COST_OUT) + (x_ref.shape[0] // SUBLANES) * COST_SCAN_STEP
    cost_per_unit = total_cost / len(units)
    spent = [0.0]

    def side_work(cost):
        spent[0] += cost
        while units and spent[0] >= cost_per_unit:
            spent[0] -= cost_per_unit
            unit(*units.pop(0))

    _rglru_tile(u_cols(3 * D_MODEL, D_MODEL), gr_ref, side_work,
                cw_ref, cb_ref, wr_ref, br_ref, wi_ref, bi_ref, lam_ref,
                hb_ref, xs_ref, xc_ref, rp_ref, ip_ref, a_ref, b_ref, hl_ref, pp_ref, h_ref)
    while units:
        unit(*units.pop(0))


BF16_SUBLANES = 2 * SUBLANES


def _cast_slice_spec(rows, cols, n_steps):
    repeat = next(k for k in (1, 2, 4, 8)
                  if rows * k % n_steps == 0 and (rows * k // n_steps) % BF16_SUBLANES == 0)
    return pl.BlockSpec((rows * repeat // n_steps, cols), lambda i: (i // repeat, 0))


def _inproj(x2, g, w, cw, cb, wr, br, wi, bi, lam, later_weights, seq):
    t = x2.shape[0]
    tm = RG_TILE
    n_steps = t // tm
    n_diag = D_MODEL // MXU_DIM
    scan_scratch = pltpu.VMEM((D_MODEL // LANES, SUBLANES * SEG_PITCH, LANES), F32)
    slab_scratch = pltpu.VMEM((tm, MXU_DIM), F32)
    cast_specs = [_cast_slice_spec(*wt.shape, n_steps) for wt in later_weights]
    outs = pl.pallas_call(
        functools.partial(_inproj_kernel, tiles_per_seq=seq // tm, n_cast=len(later_weights)),
        grid=(n_steps,),
        in_specs=[
            pl.BlockSpec((tm, D_MODEL), lambda i: (i, 0)),
            _resident((1, D_MODEL)),
            pl.BlockSpec(memory_space=pl.ANY),
            _resident((CONV_W, D_MODEL)),
            _resident((1, D_MODEL)),
            _resident((n_diag, MXU_DIM, MXU_DIM)),
            _resident((1, D_MODEL)),
            _resident((n_diag, MXU_DIM, MXU_DIM)),
            _resident((1, D_MODEL)),
            _resident((1, D_MODEL)),
            *cast_specs,
        ],
        out_specs=[
            pl.BlockSpec((3 * N_PAIRS, tm, HEAD_PAIR), lambda i: (0, i, 0)),
            pl.BlockSpec((tm, 2 * D_MODEL), lambda i: (i, 0)),
            pl.BlockSpec((tm, D_MODEL), lambda i: (i, 0)),
            *cast_specs,
        ],
        out_shape=[
            jax.ShapeDtypeStruct((3 * N_PAIRS, t, HEAD_PAIR), BF16),
            jax.ShapeDtypeStruct((t, 2 * D_MODEL), BF16),
            jax.ShapeDtypeStruct((t, D_MODEL), BF16),
            *[jax.ShapeDtypeStruct(wt.shape, BF16) for wt in later_weights],
        ],
        scratch_shapes=[
            pltpu.VMEM(w.shape, BF16),
            pltpu.VMEM((2, W_STAGE_ROWS, w.shape[1]), F32),
            pltpu.SemaphoreType.DMA((2,)),
            pltpu.VMEM((tm, D_MODEL), BF16),
            pltpu.VMEM((tm, D_MODEL), F32),
            pltpu.VMEM((D_MODEL // LANES, tm + SUBLANES, LANES), F32),
            slab_scratch, slab_scratch, slab_scratch,
            scan_scratch, scan_scratch, scan_scratch, scan_scratch,
            pltpu.VMEM((SUBLANES, D_MODEL), F32),
        ],
        compiler_params=pltpu.CompilerParams(
            dimension_semantics=("arbitrary",), vmem_limit_bytes=VMEM_LIMIT),
        name="inproj_rglru",
    )(x2, g, w, cw, cb, wr, br, wi, bi, lam, *later_weights)
    return outs[0], outs[1], outs[2], outs[3:]


def _band_geometry(ntiles):
    kw = ntiles * GROUP
    return kw, BAND - kw


def _attn_scores(q_ref, k_ref, s_ref, r0, ntiles):
    kw, c_off = _band_geometry(ntiles)
    lane = lax.broadcasted_iota(jnp.int32, (GROUP, HEAD_PAIR), 1)
    first_head = lane < HEAD_DIM
    qp = q_ref[0, pl.ds(r0, GROUP), :] * jnp.asarray(1.0 / math.sqrt(HEAD_DIM), BF16)
    zero = jnp.zeros_like(qp)
    qq = jnp.concatenate([jnp.where(first_head, qp, zero),
                          jnp.where(first_head, zero, qp)], axis=0)
    kk = k_ref[0, pl.ds(r0 + GROUP - kw, kw), :]
    s_ref[:, c_off:] = lax.dot_general(qq, kk, (((1,), (1,)), ((), ())),
                                       preferred_element_type=F32)


def _attn_softmax(s_ref, bias_ref, p_ref, l_ref, ntiles):
    _, c_off = _band_geometry(ntiles)
    rb = 32
    lane_rb = lax.broadcasted_iota(jnp.int32, (rb, LANES), 1)
    for hh in range(2):
        for i in range(GROUP // CHUNK):
            lo_col = max(i * CHUNK, c_off)
            hi_col = i * CHUNK + (LEFT_CHUNKS + 1) * CHUNK
            ws = (lo_col // LANES) * LANES
            we = -(-hi_col // LANES) * LANES
            nv = (we - ws) // LANES
            rows_all = slice(hh * GROUP + i * CHUNK, hh * GROUP + (i + 1) * CHUNK)
            if ws > c_off:
                p_ref[rows_all, c_off:ws] = jnp.zeros((CHUNK, ws - c_off), BF16)
            if we < BAND:
                p_ref[rows_all, we:BAND] = jnp.zeros((CHUNK, BAND - we), BF16)
            for b in range(CHUNK // rb):
                rows = slice(hh * GROUP + i * CHUNK + b * rb, hh * GROUP + i * CHUNK + (b + 1) * rb)
                brows = slice(i * CHUNK + b * rb, i * CHUNK + (b + 1) * rb)
                x = s_ref[rows, ws:we] + bias_ref[hh, brows, ws:we]
                cols = [x[:, c * LANES:(c + 1) * LANES] for c in range(nv)]
                if lo_col % LANES:
                    cols[0] = jnp.where(lane_rb >= HEAD_DIM, cols[0], NEG_INF)
                if hi_col % LANES:
                    cols[-1] = jnp.where(lane_rb < HEAD_DIM, cols[-1], NEG_INF)
                x = jnp.concatenate(cols, axis=1)
                m = jnp.max(x, axis=-1, keepdims=True)
                e = jnp.exp(x - m)
                l = jnp.sum(e, axis=-1, keepdims=True)
                p_ref[rows, ws:we] = e.astype(BF16)
                l_ref[rows, :] = jnp.broadcast_to(l, (rb, LANES))

def _attn_values(p_ref, l_ref, v_ref, o_ref, r0, ntiles):
    kw, c_off = _band_geometry(ntiles)
    lane = lax.broadcasted_iota(jnp.int32, (GROUP, HEAD_PAIR), 1)
    vv = v_ref[0, pl.ds(r0 + GROUP - kw, kw), :]
    o2 = jnp.dot(p_ref[:, c_off:], vv, preferred_element_type=F32)
    o2 = o2 / l_ref[...]
    o = jnp.where(lane < HEAD_DIM, o2[:GROUP], o2[GROUP:])
    o_ref[0, pl.ds(r0, GROUP), :] = o.astype(BF16)


def _attn_kernel(q_ref, k_ref, v_ref, w_ref, o_ref, bias_ref, s0, s1, p0, p1, l0, l1):
    n_groups = q_ref.shape[1] // GROUP
    full = BAND_TILES
    row0 = lambda g: pl.multiple_of(g * GROUP, GROUP)

    @pl.when(pl.program_id(1) == 0)
    def _():
        for hh in range(2):
            diag = jnp.broadcast_to(w_ref[hh], (GROUP, DIAG_LANES))
            rolled = pltpu.roll(diag, DIAG_LANES - (GROUP - 1), 1, stride=1, stride_axis=0)
            bias_ref[hh] = rolled[:, :BAND]

    def scores(g, s_ref, ntiles=full):
        _attn_scores(q_ref, k_ref, s_ref, row0(g), ntiles)

    def softmax(s_ref, p_ref, l_ref, ntiles=full):
        _attn_softmax(s_ref, bias_ref, p_ref, l_ref, ntiles)

    def values(g, p_ref, l_ref, ntiles=full):
        _attn_values(p_ref, l_ref, v_ref, o_ref, row0(g), ntiles)

    bufs = ((s0, p0, l0), (s1, p1, l1))
    ntiles = lambda g: min(g + 1, full)
    for t in range(n_groups + 2):
        if t >= 2:
            _, p_ref, l_ref = bufs[t % 2]
            values(t - 2, p_ref, l_ref, ntiles(t - 2))
        if 1 <= t <= n_groups:
            softmax(*bufs[(t - 1) % 2], ntiles(t - 1))
        if t < n_groups:
            scores(t, bufs[t % 2][0], ntiles(t))


def _attention(qkv, diagonals, batch, seq):
    t = batch * seq
    return pl.pallas_call(
        _attn_kernel,
        grid=(N_PAIRS, batch),
        in_specs=[
            pl.BlockSpec((1, seq, HEAD_PAIR), lambda hp, b: (hp, b, 0)),
            pl.BlockSpec((1, seq, HEAD_PAIR), lambda hp, b: (N_PAIRS + hp, b, 0)),
            pl.BlockSpec((1, seq, HEAD_PAIR), lambda hp, b: (2 * N_PAIRS + hp, b, 0)),
            pl.BlockSpec((2, 1, DIAG_LANES), lambda hp, b: (hp, 0, 0)),
        ],
        out_specs=pl.BlockSpec((1, seq, HEAD_PAIR), lambda hp, b: (hp, b, 0)),
        out_shape=jax.ShapeDtypeStruct((N_PAIRS, t, HEAD_PAIR), BF16),
        scratch_shapes=[
            pltpu.VMEM((2, GROUP, BAND), F32),
            pltpu.VMEM((2 * GROUP, BAND), F32),
            pltpu.VMEM((2 * GROUP, BAND), F32),
            pltpu.VMEM((2 * GROUP, BAND), BF16),
            pltpu.VMEM((2 * GROUP, BAND), BF16),
            pltpu.VMEM((2 * GROUP, LANES), F32),
            pltpu.VMEM((2 * GROUP, LANES), F32),
        ],
        compiler_params=pltpu.CompilerParams(
            dimension_semantics=("parallel", "arbitrary"), vmem_limit_bytes=VMEM_LIMIT),
        name="band_attention",
    )(qkv, qkv, qkv, diagonals)


def _band_bias_diagonals(rel_table):
    j = jnp.arange(DIAG_LANES)
    dist = (GROUP - 1 - j) + (BAND - GROUP)
    w = rel_table[:, jnp.clip(dist, -MAX_REL, MAX_REL) + MAX_REL]
    return w[:, None, :]


def _gelu_tanh(x):
    k = math.sqrt(2.0 / math.pi)
    inner = x * (k + (k * 0.044715) * (x * x))
    return x * (0.5 + 0.5 * jnp.tanh(inner))


def _rglru_tile(xr, gr_ref, side_work, cw_ref, cb_ref, wr_ref, br_ref, wi_ref, bi_ref, lam_ref,
                o_ref, xs_ref, xc_ref, rp_ref, ip_ref, a_ref, b_ref, hl_ref, pp_ref, h_ref):
    tm = xr.shape[0]
    seg = tm // SUBLANES
    n_lane_tiles = D_MODEL // LANES
    tiles_per_slab = MXU_DIM // LANES
    lane_tile = lambda c: slice(c * LANES, (c + 1) * LANES)

    for c in range(n_lane_tiles):
        xs_ref[c, SUBLANES:, :] = xr[:, lane_tile(c)]

    for d in range(D_MODEL // MXU_DIM):
        for k in range(tiles_per_slab):
            c = d * tiles_per_slab + k
            for j in range(SUBLANES):
                acc = cb_ref[:, lane_tile(c)]
                for tap in range(CONV_W):
                    off = SUBLANES - (CONV_W - 1) + tap + j * seg
                    acc = acc + xs_ref[c, off:off + seg, :] * cw_ref[tap:tap + 1, lane_tile(c)]
                xc_ref[j * seg:(j + 1) * seg, lane_tile(k)] = acc
                side_work(COST_CONV)
            xs_ref[c, 0:SUBLANES, :] = xs_ref[c, tm:tm + SUBLANES, :]
        xcb = xc_ref[...].astype(BF16)
        rp_ref[...] = jnp.dot(xcb, wr_ref[d], preferred_element_type=F32)
        ip_ref[...] = jnp.dot(xcb, wi_ref[d], preferred_element_type=F32)
        for k in range(tiles_per_slab):
            c = d * tiles_per_slab + k
            nlam = -lam_ref[:, lane_tile(c)]
            softplus = jnp.maximum(nlam, 0.0) + jnp.log1p(jnp.exp(-jnp.abs(nlam)))
            log_a_scale = (-RG_C) * softplus
            for j in range(SUBLANES):
                rows = slice(j * seg, (j + 1) * seg)
                xc = xc_ref[rows, lane_tile(k)]
                r = jax.nn.sigmoid(rp_ref[rows, lane_tile(k)] + br_ref[:, lane_tile(c)])
                ig = jax.nn.sigmoid(ip_ref[rows, lane_tile(k)] + bi_ref[:, lane_tile(c)])
                log_a = r * log_a_scale
                a = jnp.exp(log_a)
                y = jnp.tanh(log_a) * (-1.0 - a * a)
                mult = jnp.where(y > 0.0, y * lax.rsqrt(y), 0.0)
                srows = slice(j * SEG_PITCH, j * SEG_PITCH + seg)
                a_ref[c, srows, :] = a
                b_ref[c, srows, :] = mult * (ig * xc)
                side_work(COST_GATE)

    hs = [jnp.zeros((SUBLANES, LANES), F32) for _ in range(n_lane_tiles)]
    ps = [jnp.ones((SUBLANES, LANES), F32) for _ in range(n_lane_tiles)]
    for t in range(seg):
        idx = pl.ds(t, SUBLANES, stride=SEG_PITCH)
        for c in range(n_lane_tiles):
            a_t = a_ref[c, idx, :]
            hs[c] = a_t * hs[c] + b_ref[c, idx, :]
            ps[c] = a_t * ps[c]
            hl_ref[c, idx, :] = hs[c]
            pp_ref[c, idx, :] = ps[c]
        side_work(COST_SCAN_STEP)
    h_end = jnp.concatenate(hs, axis=1)
    p_end = jnp.concatenate(ps, axis=1)

    h_in = h_ref[0:1, :]
    for j in range(SUBLANES):
        rows = slice(j * seg, (j + 1) * seg)
        srows = slice(j * SEG_PITCH, j * SEG_PITCH + seg)
        for c in range(n_lane_tiles):
            hj = hl_ref[c, srows, :] + pp_ref[c, srows, :] * h_in[:, lane_tile(c)]
            o_ref[rows, lane_tile(c)] = (hj * _gelu_tanh(gr_ref[rows, lane_tile(c)])).astype(BF16)
            side_work(COST_OUT)
        h_in = p_end[j:j + 1, :] * h_in + h_end[j:j + 1, :]
    h_ref[0:1, :] = h_in


def _block_diag_tiles(w):
    per = MXU_DIM // RG_BLOCK
    n_diag = D_MODEL // MXU_DIM
    w4 = w.reshape(n_diag, per, RG_BLOCK, RG_BLOCK)
    eye = jnp.eye(per, dtype=w.dtype)
    tiles = jnp.einsum('dpij,pq->dpiqj', w4, eye)
    return tiles.reshape(n_diag, MXU_DIM, MXU_DIM)


def _mixffn_kernel(x_ref, at_ref, hb_ref, ga_ref, gb_ref, bm_ref, wa_ref, wb_ref, wo_ref,
                   gn_ref, win_ref, wout_ref, gf_ref, o_ref, *, splits):
    attn = jnp.concatenate([at_ref[hp] for hp in range(N_PAIRS)], axis=1)
    ya = jnp.dot(attn, wa_ref[...], preferred_element_type=F32)
    yb = jnp.dot(hb_ref[...], wb_ref[...], preferred_element_type=F32)
    g_a = jax.nn.sigmoid(ga_ref[...].astype(F32) + bm_ref[:, :D_MODEL])
    g_b = jax.nn.sigmoid(gb_ref[...].astype(F32) + bm_ref[:, D_MODEL:])
    mix = (g_a * ya + g_b * yb).astype(BF16)
    h = x_ref[...] + jnp.dot(mix, wo_ref[...], preferred_element_type=F32)
    hn = _rms(h, gn_ref[...]).astype(BF16)
    d_ff = wout_ref.shape[0]
    acc = h
    for c0, c1 in splits:
        g = jnp.dot(hn, win_ref[:, c0:c1], preferred_element_type=F32)
        up = jnp.dot(hn, win_ref[:, d_ff + c0:d_ff + c1], preferred_element_type=F32)
        act = (jax.nn.silu(g) * up).astype(BF16)
        acc = acc + jnp.dot(act, wout_ref[c0:c1, :], preferred_element_type=F32)
    o_ref[...] = _rms(acc, gf_ref[...])


def _mixffn(x2, attn, hb, gates, bm, wa, wb, wo, gn, win, wout, gf, tm):
    t = x2.shape[0]
    d_ff = wout.shape[0]
    n_tiles = d_ff // MXU_DIM
    half = (n_tiles + 1) // 2 * MXU_DIM
    splits = ((0, half), (half, d_ff))
    row = lambda i: (i, 0)
    return pl.pallas_call(
        functools.partial(_mixffn_kernel, splits=splits),
        grid=(t // tm,),
        in_specs=[
            pl.BlockSpec((tm, D_MODEL), row),
            pl.BlockSpec((N_PAIRS, tm, HEAD_PAIR), lambda i: (0, i, 0)),
            pl.BlockSpec((tm, D_MODEL), row),
            pl.BlockSpec((tm, D_MODEL), lambda i: (i, 0)),
            pl.BlockSpec((tm, D_MODEL), lambda i: (i, 1)),
            _resident((1, 2 * D_MODEL)),
            _resident((D_MODEL, D_MODEL)),
            _resident((D_MODEL, D_MODEL)),
            _resident((D_MODEL, D_MODEL)),
            _resident((1, D_MODEL)),
            _resident((D_MODEL, 2 * d_ff)),
            _resident((d_ff, D_MODEL)),
            _resident((1, D_MODEL)),
        ],
        out_specs=pl.BlockSpec((tm, D_MODEL), row),
        out_shape=jax.ShapeDtypeStruct((t, D_MODEL), F32),
        compiler_params=pltpu.CompilerParams(
            dimension_semantics=("parallel",), vmem_limit_bytes=VMEM_LIMIT),
        name="mix_ffn",
    )(x2, attn, hb, gates, gates, bm, wa, wb, wo, gn, win, wout, gf)


def kernel(x, norm_mix_g, w_in, b_merge, rel_table, w_attn_out, conv_w, conv_b, w_rg_r, b_rg_r,
           w_rg_i, b_rg_i, rg_lambda, w_rnn_out, w_o, norm_ffn_g, w_ffn_in, w_ffn_out, final_norm_g):
    batch, seq, d = x.shape
    assert d == D_MODEL and seq % GROUP == 0 and seq // GROUP >= BAND_TILES
    assert w_in.shape[0] == 1, "single-layer block"
    x2 = x.reshape(batch * seq, d)
    row = lambda v: v.reshape(1, -1)

    assert seq % RG_TILE == 0
    later_weights = (w_attn_out[0], w_rnn_out[0], w_o[0], w_ffn_in[0], w_ffn_out[0])
    qkv, gates, hb, (wa, wb, wo, wfi, wfo) = _inproj(
        x2, row(norm_mix_g[0]), w_in[0], conv_w[0], row(conv_b[0]),
        _block_diag_tiles(w_rg_r[0]).astype(BF16), row(b_rg_r[0]),
        _block_diag_tiles(w_rg_i[0]).astype(BF16), row(b_rg_i[0]), row(rg_lambda[0]),
        later_weights, seq)
    attn = _attention(qkv, _band_bias_diagonals(rel_table[0]), batch, seq)
    out = _mixffn(x2, attn, hb, gates, row(b_merge[0]), wa, wb, wo, row(norm_ffn_g[0]), wfi, wfo,
                  row(final_norm_g), tm=512)
    return out.reshape(batch, seq, d)
```

```python
import functools
import math

import jax
import jax.numpy as jnp
from jax import lax
from jax.experimental import pallas as pl
from jax.experimental.pallas import tpu as pltpu

F32 = jnp.float32
BF16 = jnp.bfloat16

D_MODEL = 1024
CHUNK = 64
LEFT_CHUNKS = 8
N_HEADS = 16
HEAD_DIM = 64
MAX_REL = 128
N_RG_BLOCKS = 16
RG_BLOCK = D_MODEL // N_RG_BLOCKS
CONV_W = 4
RG_C = 8.0
EPS = 1e-6
NEG_INF = -1e30

LANES = 128
SUBLANES = 8
MXU_DIM = 256
HEAD_PAIR = 2 * HEAD_DIM
N_PAIRS = N_HEADS // 2
GROUP = 4 * CHUNK
BAND_TILES = LEFT_CHUNKS * CHUNK // GROUP + 1
BAND = BAND_TILES * GROUP
DIAG_LANES = -(-(GROUP + BAND - 1) // LANES) * LANES
VMEM_LIMIT = 56 * 1024 * 1024
RG_TILE = 512
W_STAGE_ROWS = 64
FFN_CHUNK = 3 * MXU_DIM
SEG_PITCH = RG_TILE // SUBLANES + SUBLANES // 2
COST_CONV, COST_GATE, COST_SCAN_STEP, COST_OUT = 16, 52, 23, 24


def _rms(x, g):
    var = jnp.mean(x * x, axis=-1, keepdims=True)
    return x * lax.rsqrt(var + EPS) * g


def _resident(shape):
    zeros = (0,) * len(shape)
    return pl.BlockSpec(shape, lambda *_: zeros, pipeline_mode=pl.Buffered(1))


def _load_rounded(src_hbm, dst_ref, stage_ref, sem):
    rows = stage_ref.shape[1]
    n_chunks = src_hbm.shape[0] // rows

    def chunk_copy(c):
        slot = c % 2
        return pltpu.make_async_copy(src_hbm.at[pl.ds(c * rows, rows), :], stage_ref.at[slot],
                                     sem.at[slot])

    chunk_copy(0).start()
    for c in range(n_chunks):
        if c + 1 < n_chunks:
            chunk_copy(c + 1).start()
        chunk_copy(c).wait()
        dst_ref[c * rows:(c + 1) * rows, :] = stage_ref[c % 2].astype(BF16)


def _inproj_kernel(*refs, tiles_per_seq, n_cast):
    (x_ref, g_ref, w_hbm, cw_ref, cb_ref, wr_ref, br_ref, wi_ref, bi_ref, lam_ref), refs = refs[:10], refs[10:]
    cast_in, refs = refs[:n_cast], refs[n_cast:]
    (qkv_ref, gates_ref, hb_ref), refs = refs[:3], refs[3:]
    cast_out, refs = refs[:n_cast], refs[n_cast:]
    (w_ref, stage_ref, stage_sem, xn_ref, gr_ref, xs_ref, xc_ref, rp_ref, ip_ref,
     a_ref, b_ref, hl_ref, pp_ref, h_ref) = refs

    @pl.when(pl.program_id(0) == 0)
    def _():
        _load_rounded(w_hbm, w_ref, stage_ref, stage_sem)

    for src_ref, dst_ref in zip(cast_in, cast_out):
        dst_ref[...] = src_ref[...].astype(BF16)

    @pl.when(pl.program_id(0) % tiles_per_seq == 0)
    def _():
        xs_ref[:, 0:SUBLANES, :] = jnp.zeros((D_MODEL // LANES, SUBLANES, LANES), F32)
        h_ref[...] = jnp.zeros_like(h_ref)

    xn_ref[...] = _rms(x_ref[...], g_ref[...]).astype(BF16)

    def u_cols(c0, width):
        return jnp.dot(xn_ref[...], w_ref[:, c0:c0 + width], preferred_element_type=F32)

    def unit(j, n):
        acc = u_cols(j * D_MODEL + n * MXU_DIM, MXU_DIM)
        if j < 3:
            for k in range(MXU_DIM // HEAD_PAIR):
                hp = n * (MXU_DIM // HEAD_PAIR) + k
                qkv_ref[j * N_PAIRS + hp] = acc[:, k * HEAD_PAIR:(k + 1) * HEAD_PAIR].astype(BF16)
        elif j == 4:
            gr_ref[:, n * MXU_DIM:(n + 1) * MXU_DIM] = acc
        else:
            c0 = (j - 5) * D_MODEL + n * MXU_DIM
            gates_ref[:, c0:c0 + MXU_DIM] = acc.astype(BF16)

    units = [(j, n) for j in (4, 0, 1, 2, 5, 6) for n in range(D_MODEL // MXU_DIM)]
    n_pieces = SUBLANES * (D_MODEL // LANES)
    total_cost = n_pieces * (COST_CONV + COST_GATE + COST_OUT) + (x_ref.shape[0] // SUBLANES) * COST_SCAN_STEP
    cost_per_unit = total_cost / len(units)
    spent = [0.0]

    def side_work(cost):
        spent[0] += cost
        while units and spent[0] >= cost_per_unit:
            spent[0] -= cost_per_unit
            unit(*units.pop(0))

    _rglru_tile(u_cols(3 * D_MODEL, D_MODEL), gr_ref, side_work,
                cw_ref, cb_ref, wr_ref, br_ref, wi_ref, bi_ref, lam_ref,
                hb_ref, xs_ref, xc_ref, rp_ref, ip_ref, a_ref, b_ref, hl_ref, pp_ref, h_ref)
    while units:
        unit(*units.pop(0))


BF16_SUBLANES = 2 * SUBLANES


def _cast_slice_spec(rows, cols, n_steps):
    repeat = next(k for k in (1, 2, 4, 8)
                  if rows * k % n_steps == 0 and (rows * k // n_steps) % BF16_SUBLANES == 0)
    return pl.BlockSpec((rows * repeat // n_steps, cols), lambda i: (i // repeat, 0))


def _inproj(x2, g, w, cw, cb, wr, br, wi, bi, lam, later_weights, seq):
    t = x2.shape[0]
    tm = RG_TILE
    n_steps = t // tm
    n_diag = D_MODEL // MXU_DIM
    scan_scratch = pltpu.VMEM((D_MODEL // LANES, SUBLANES * SEG_PITCH, LANES), F32)
    slab_scratch = pltpu.VMEM((tm, MXU_DIM), F32)
    cast_specs = [_cast_slice_spec(*wt.shape, n_steps) for wt in later_weights]
    outs = pl.pallas_call(
        functools.partial(_inproj_kernel, tiles_per_seq=seq // tm, n_cast=len(later_weights)),
        grid=(n_steps,),
        in_specs=[
            pl.BlockSpec((tm, D_MODEL), lambda i: (i, 0)),
            _resident((1, D_MODEL)),
            pl.BlockSpec(memory_space=pl.ANY),
            _resident((CONV_W, D_MODEL)),
            _resident((1, D_MODEL)),
            _resident((n_diag, MXU_DIM, MXU_DIM)),
            _resident((1, D_MODEL)),
            _resident((n_diag, MXU_DIM, MXU_DIM)),
            _resident((1, D_MODEL)),
            _resident((1, D_MODEL)),
            *cast_specs,
        ],
        out_specs=[
            pl.BlockSpec((3 * N_PAIRS, tm, HEAD_PAIR), lambda i: (0, i, 0)),
            pl.BlockSpec((tm, 2 * D_MODEL), lambda i: (i, 0)),
            pl.BlockSpec((tm, D_MODEL), lambda i: (i, 0)),
            *cast_specs,
        ],
        out_shape=[
            jax.ShapeDtypeStruct((3 * N_PAIRS, t, HEAD_PAIR), BF16),
            jax.ShapeDtypeStruct((t, 2 * D_MODEL), BF16),
            jax.ShapeDtypeStruct((t, D_MODEL), BF16),
            *[jax.ShapeDtypeStruct(wt.shape, BF16) for wt in later_weights],
        ],
        scratch_shapes=[
            pltpu.VMEM(w.shape, BF16),
            pltpu.VMEM((2, W_STAGE_ROWS, w.shape[1]), F32),
            pltpu.SemaphoreType.DMA((2,)),
            pltpu.VMEM((tm, D_MODEL), BF16),
            pltpu.VMEM((tm, D_MODEL), F32),
            pltpu.VMEM((D_MODEL // LANES, tm + SUBLANES, LANES), F32),
            slab_scratch, slab_scratch, slab_scratch,
            scan_scratch, scan_scratch, scan_scratch, scan_scratch,
            pltpu.VMEM((SUBLANES, D_MODEL), F32),
        ],
        compiler_params=pltpu.CompilerParams(
            dimension_semantics=("arbitrary",), vmem_limit_bytes=VMEM_LIMIT),
        name="inproj_rglru",
    )(x2, g, w, cw, cb, wr, br, wi, bi, lam, *later_weights)
    return outs[0], outs[1], outs[2], outs[3:]


def _band_geometry(ntiles):
    kw = ntiles * GROUP
    return kw, BAND - kw


def _attn_scores(q_ref, k_ref, s_ref, r0, ntiles):
    kw, c_off = _band_geometry(ntiles)
    lane = lax.broadcasted_iota(jnp.int32, (GROUP, HEAD_PAIR), 1)
    first_head = lane < HEAD_DIM
    qp = q_ref[0, pl.ds(r0, GROUP), :] * jnp.asarray(1.0 / math.sqrt(HEAD_DIM), BF16)
    zero = jnp.zeros_like(qp)
    qq = jnp.concatenate([jnp.where(first_head, qp, zero),
                          jnp.where(first_head, zero, qp)], axis=0)
    kk = k_ref[0, pl.ds(r0 + GROUP - kw, kw), :]
    s_ref[:, c_off:] = lax.dot_general(qq, kk, (((1,), (1,)), ((), ())),
                                       preferred_element_type=F32)


def _attn_softmax(s_ref, bias_ref, p_ref, l_ref, ntiles):
    _, c_off = _band_geometry(ntiles)
    rb = 32
    lane_rb = lax.broadcasted_iota(jnp.int32, (rb, LANES), 1)
    for hh in range(2):
        for i in range(GROUP // CHUNK):
            lo_col = max(i * CHUNK, c_off)
            hi_col = i * CHUNK + (LEFT_CHUNKS + 1) * CHUNK
            ws = (lo_col // LANES) * LANES
            we = -(-hi_col // LANES) * LANES
            nv = (we - ws) // LANES
            rows_all = slice(hh * GROUP + i * CHUNK, hh * GROUP + (i + 1) * CHUNK)
            if ws > c_off:
                p_ref[rows_all, c_off:ws] = jnp.zeros((CHUNK, ws - c_off), BF16)
            if we < BAND:
                p_ref[rows_all, we:BAND] = jnp.zeros((CHUNK, BAND - we), BF16)
            for b in range(CHUNK // rb):
                rows = slice(hh * GROUP + i * CHUNK + b * rb, hh * GROUP + i * CHUNK + (b + 1) * rb)
                brows = slice(i * CHUNK + b * rb, i * CHUNK + (b + 1) * rb)
                x = s_ref[rows, ws:we] + bias_ref[hh, brows, ws:we]
                cols = [x[:, c * LANES:(c + 1) * LANES] for c in range(nv)]
                if lo_col % LANES:
                    cols[0] = jnp.where(lane_rb >= HEAD_DIM, cols[0], NEG_INF)
                if hi_col % LANES:
                    cols[-1] = jnp.where(lane_rb < HEAD_DIM, cols[-1], NEG_INF)
                x = jnp.concatenate(cols, axis=1)
                m = jnp.max(x, axis=-1, keepdims=True)
                e = jnp.exp(x - m)
                l = jnp.sum(e, axis=-1, keepdims=True)
                p_ref[rows, ws:we] = e.astype(BF16)
                l_ref[rows, :] = jnp.broadcast_to(l, (rb, LANES))

def _attn_values(p_ref, l_ref, v_ref, o_ref, r0, ntiles):
    kw, c_off = _band_geometry(ntiles)
    lane = lax.broadcasted_iota(jnp.int32, (GROUP, HEAD_PAIR), 1)
    vv = v_ref[0, pl.ds(r0 + GROUP - kw, kw), :]
    o2 = jnp.dot(p_ref[:, c_off:], vv, preferred_element_type=F32)
    o2 = o2 / l_ref[...]
    o = jnp.where(lane < HEAD_DIM, o2[:GROUP], o2[GROUP:])
    o_ref[0, pl.ds(r0, GROUP), :] = o.astype(BF16)


def _attn_kernel(q_ref, k_ref, v_ref, w_ref, o_ref, bias_ref, s0, s1, p0, p1, l0, l1):
    n_groups = q_ref.shape[1] // GROUP
    full = BAND_TILES
    row0 = lambda g: pl.multiple_of(g * GROUP, GROUP)

    @pl.when(pl.program_id(1) == 0)
    def _():
        for hh in range(2):
            diag = jnp.broadcast_to(w_ref[hh], (GROUP, DIAG_LANES))
            rolled = pltpu.roll(diag, DIAG_LANES - (GROUP - 1), 1, stride=1, stride_axis=0)
            bias_ref[hh] = rolled[:, :BAND]

    def scores(g, s_ref, ntiles=full):
        _attn_scores(q_ref, k_ref, s_ref, row0(g), ntiles)

    def softmax(s_ref, p_ref, l_ref, ntiles=full):
        _attn_softmax(s_ref, bias_ref, p_ref, l_ref, ntiles)

    def values(g, p_ref, l_ref, ntiles=full):
        _attn_values(p_ref, l_ref, v_ref, o_ref, row0(g), ntiles)

    bufs = ((s0, p0, l0), (s1, p1, l1))
    ntiles = lambda g: min(g + 1, full)
    for t in range(n_groups + 2):
        if t >= 2:
            _, p_ref, l_ref = bufs[t % 2]
            values(t - 2, p_ref, l_ref, ntiles(t - 2))
        if 1 <= t <= n_groups:
            softmax(*bufs[(t - 1) % 2], ntiles(t - 1))
        if t < n_groups:
            scores(t, bufs[t % 2][0], ntiles(t))


def _attention(qkv, diagonals, batch, seq):
    t = batch * seq
    return pl.pallas_call(
        _attn_kernel,
        grid=(N_PAIRS, batch),
        in_specs=[
            pl.BlockSpec((1, seq, HEAD_PAIR), lambda hp, b: (hp, b, 0)),
            pl.BlockSpec((1, seq, HEAD_PAIR), lambda hp, b: (N_PAIRS + hp, b, 0)),
            pl.BlockSpec((1, seq, HEAD_PAIR), lambda hp, b: (2 * N_PAIRS + hp, b, 0)),
            pl.BlockSpec((2, 1, DIAG_LANES), lambda hp, b: (hp, 0, 0)),
        ],
        out_specs=pl.BlockSpec((1, seq, HEAD_PAIR), lambda hp, b: (hp, b, 0)),
        out_shape=jax.ShapeDtypeStruct((N_PAIRS, t, HEAD_PAIR), BF16),
        scratch_shapes=[
            pltpu.VMEM((2, GROUP, BAND), F32),
            pltpu.VMEM((2 * GROUP, BAND), F32),
            pltpu.VMEM((2 * GROUP, BAND), F32),
            pltpu.VMEM((2 * GROUP, BAND), BF16),
            pltpu.VMEM((2 * GROUP, BAND), BF16),
            pltpu.VMEM((2 * GROUP, LANES), F32),
            pltpu.VMEM((2 * GROUP, LANES), F32),
        ],
        compiler_params=pltpu.CompilerParams(
            dimension_semantics=("parallel", "arbitrary"), vmem_limit_bytes=VMEM_LIMIT),
        name="band_attention",
    )(qkv, qkv, qkv, diagonals)


def _band_bias_diagonals(rel_table):
    j = jnp.arange(DIAG_LANES)
    dist = (GROUP - 1 - j) + (BAND - GROUP)
    w = rel_table[:, jnp.clip(dist, -MAX_REL, MAX_REL) + MAX_REL]
    return w[:, None, :]


def _gelu_tanh(x):
    k = math.sqrt(2.0 / math.pi)
    inner = x * (k + (k * 0.044715) * (x * x))
    return x * (0.5 + 0.5 * jnp.tanh(inner))


def _rglru_tile(xr, gr_ref, side_work, cw_ref, cb_ref, wr_ref, br_ref, wi_ref, bi_ref, lam_ref,
                o_ref, xs_ref, xc_ref, rp_ref, ip_ref, a_ref, b_ref, hl_ref, pp_ref, h_ref):
    tm = xr.shape[0]
    seg = tm // SUBLANES
    n_lane_tiles = D_MODEL // LANES
    tiles_per_slab = MXU_DIM // LANES
    lane_tile = lambda c: slice(c * LANES, (c + 1) * LANES)

    for c in range(n_lane_tiles):
        xs_ref[c, SUBLANES:, :] = xr[:, lane_tile(c)]

    for d in range(D_MODEL // MXU_DIM):
        for k in range(tiles_per_slab):
            c = d * tiles_per_slab + k
            for j in range(SUBLANES):
                acc = cb_ref[:, lane_tile(c)]
                for tap in range(CONV_W):
                    off = SUBLANES - (CONV_W - 1) + tap + j * seg
                    acc = acc + xs_ref[c, off:off + seg, :] * cw_ref[tap:tap + 1, lane_tile(c)]
                xc_ref[j * seg:(j + 1) * seg, lane_tile(k)] = acc
                side_work(COST_CONV)
            xs_ref[c, 0:SUBLANES, :] = xs_ref[c, tm:tm + SUBLANES, :]
        xcb = xc_ref[...].astype(BF16)
        rp_ref[...] = jnp.dot(xcb, wr_ref[d], preferred_element_type=F32)
        ip_ref[...] = jnp.dot(xcb, wi_ref[d], preferred_element_type=F32)
        for k in range(tiles_per_slab):
            c = d * tiles_per_slab + k
            nlam = -lam_ref[:, lane_tile(c)]
            softplus = jnp.maximum(nlam, 0.0) + jnp.log1p(jnp.exp(-jnp.abs(nlam)))
            log_a_scale = (-RG_C) * softplus
            for j in range(SUBLANES):
                rows = slice(j * seg, (j + 1) * seg)
                xc = xc_ref[rows, lane_tile(k)]
                r = jax.nn.sigmoid(rp_ref[rows, lane_tile(k)] + br_ref[:, lane_tile(c)])
                ig = jax.nn.sigmoid(ip_ref[rows, lane_tile(k)] + bi_ref[:, lane_tile(c)])
                log_a = r * log_a_scale
                a = jnp.exp(log_a)
                y = jnp.tanh(log_a) * (-1.0 - a * a)
                mult = jnp.where(y > 0.0, y * lax.rsqrt(y), 0.0)
                srows = slice(j * SEG_PITCH, j * SEG_PITCH + seg)
                a_ref[c, srows, :] = a
                b_ref[c, srows, :] = mult * (ig * xc)
                side_work(COST_GATE)

    hs = [jnp.zeros((SUBLANES, LANES), F32) for _ in range(n_lane_tiles)]
    ps = [jnp.ones((SUBLANES, LANES), F32) for _ in range(n_lane_tiles)]
    for t in range(seg):
        idx = pl.ds(t, SUBLANES, stride=SEG_PITCH)
        for c in range(n_lane_tiles):
            a_t = a_ref[c, idx, :]
            hs[c] = a_t * hs[c] + b_ref[c, idx, :]
            ps[c] = a_t * ps[c]
            hl_ref[c, idx, :] = hs[c]
            pp_ref[c, idx, :] = ps[c]
        side_work(COST_SCAN_STEP)
    h_end = jnp.concatenate(hs, axis=1)
    p_end = jnp.concatenate(ps, axis=1)

    h_in = h_ref[0:1, :]
    for j in range(SUBLANES):
        rows = slice(j * seg, (j + 1) * seg)
        srows = slice(j * SEG_PITCH, j * SEG_PITCH + seg)
        for c in range(n_lane_tiles):
            hj = hl_ref[c, srows, :] + pp_ref[c, srows, :] * h_in[:, lane_tile(c)]
            o_ref[rows, lane_tile(c)] = (hj * _gelu_tanh(gr_ref[rows, lane_tile(c)])).astype(BF16)
            side_work(COST_OUT)
        h_in = p_end[j:j + 1, :] * h_in + h_end[j:j + 1, :]
    h_ref[0:1, :] = h_in


def _block_diag_tiles(w):
    per = MXU_DIM // RG_BLOCK
    n_diag = D_MODEL // MXU_DIM
    w4 = w.reshape(n_diag, per, RG_BLOCK, RG_BLOCK)
    eye = jnp.eye(per, dtype=w.dtype)
    tiles = jnp.einsum('dpij,pq->dpiqj', w4, eye)
    return tiles.reshape(n_diag, MXU_DIM, MXU_DIM)


def _mixffn_kernel(x_ref, at_ref, hb_ref, ga_ref, gb_ref, bm_ref, wa_ref, wb_ref, wo_ref,
                   gn_ref, win_ref, wout_ref, gf_ref, o_ref, *, splits):
    attn = jnp.concatenate([at_ref[hp] for hp in range(N_PAIRS)], axis=1)
    ya = jnp.dot(attn, wa_ref[...], preferred_element_type=F32)
    yb = jnp.dot(hb_ref[...], wb_ref[...], preferred_element_type=F32)
    g_a = jax.nn.sigmoid(ga_ref[...].astype(F32) + bm_ref[:, :D_MODEL])
    g_b = jax.nn.sigmoid(gb_ref[...].astype(F32) + bm_ref[:, D_MODEL:])
    mix = (g_a * ya + g_b * yb).astype(BF16)
    h = x_ref[...] + jnp.dot(mix, wo_ref[...], preferred_element_type=F32)
    hn = _rms(h, gn_ref[...]).astype(BF16)
    d_ff = wout_ref.shape[0]
    acc = h
    for c0, c1 in splits:
        g = jnp.dot(hn, win_ref[:, c0:c1], preferred_element_type=F32)
        up = jnp.dot(hn, win_ref[:, d_ff + c0:d_ff + c1], preferred_element_type=F32)
        act = (jax.nn.silu(g) * up).astype(BF16)
        acc = acc + jnp.dot(act, wout_ref[c0:c1, :], preferred_element_type=F32)
    o_ref[...] = _rms(acc, gf_ref[...])


def _mixffn(x2, attn, hb, gates, bm, wa, wb, wo, gn, win, wout, gf, tm):
    t = x2.shape[0]
    d_ff = wout.shape[0]
    edges = list(range(0, d_ff, FFN_CHUNK)) + [d_ff]
    splits = tuple(zip(edges[:-1], edges[1:]))
    row = lambda i: (i, 0)
    return pl.pallas_call(
        functools.partial(_mixffn_kernel, splits=splits),
        grid=(t // tm,),
        in_specs=[
            pl.BlockSpec((tm, D_MODEL), row),
            pl.BlockSpec((N_PAIRS, tm, HEAD_PAIR), lambda i: (0, i, 0)),
            pl.BlockSpec((tm, D_MODEL), row),
            pl.BlockSpec((tm, D_MODEL), lambda i: (i, 0)),
            pl.BlockSpec((tm, D_MODEL), lambda i: (i, 1)),
            _resident((1, 2 * D_MODEL)),
            _resident((D_MODEL, D_MODEL)),
            _resident((D_MODEL, D_MODEL)),
            _resident((D_MODEL, D_MODEL)),
            _resident((1, D_MODEL)),
            _resident((D_MODEL, 2 * d_ff)),
            _resident((d_ff, D_MODEL)),
            _resident((1, D_MODEL)),
        ],
        out_specs=pl.BlockSpec((tm, D_MODEL), row),
        out_shape=jax.ShapeDtypeStruct((t, D_MODEL), F32),
        compiler_params=pltpu.CompilerParams(
            dimension_semantics=("parallel",), vmem_limit_bytes=VMEM_LIMIT),
        name="mix_ffn",
    )(x2, attn, hb, gates, gates, bm, wa, wb, wo, gn, win, wout, gf)


def kernel(x, norm_mix_g, w_in, b_merge, rel_table, w_attn_out, conv_w, conv_b, w_rg_r, b_rg_r,
           w_rg_i, b_rg_i, rg_lambda, w_rnn_out, w_o, norm_ffn_g, w_ffn_in, w_ffn_out, final_norm_g):
    batch, seq, d = x.shape
    assert d == D_MODEL and seq % GROUP == 0 and seq // GROUP >= BAND_TILES
    assert w_in.shape[0] == 1, "single-layer block"
    x2 = x.reshape(batch * seq, d)
    row = lambda v: v.reshape(1, -1)

    assert seq % RG_TILE == 0
    later_weights = (w_attn_out[0], w_rnn_out[0], w_o[0], w_ffn_in[0], w_ffn_out[0])
    qkv, gates, hb, (wa, wb, wo, wfi, wfo) = _inproj(
        x2, row(norm_mix_g[0]), w_in[0], conv_w[0], row(conv_b[0]),
        _block_diag_tiles(w_rg_r[0]).astype(BF16), row(b_rg_r[0]),
        _block_diag_tiles(w_rg_i[0]).astype(BF16), row(b_rg_i[0]), row(rg_lambda[0]),
        later_weights, seq)
    attn = _attention(qkv, _band_bias_diagonals(rel_table[0]), batch, seq)
    out = _mixffn(x2, attn, hb, gates, row(b_merge[0]), wa, wb, wo, row(norm_ffn_g[0]), wfi, wfo,
                  row(final_norm_g), tm=512)
    return out.reshape(batch, seq, d)
```

```python
import functools
import math

import jax
import jax.numpy as jnp
from jax import lax
from jax.experimental import pallas as pl
from jax.experimental.pallas import tpu as pltpu

F32 = jnp.float32
BF16 = jnp.bfloat16

D_MODEL = 1024
CHUNK = 64
LEFT_CHUNKS = 8
N_HEADS = 16
HEAD_DIM = 64
MAX_REL = 128
N_RG_BLOCKS = 16
RG_BLOCK = D_MODEL // N_RG_BLOCKS
CONV_W = 4
RG_C = 8.0
EPS = 1e-6
NEG_INF = -1e30

LANES = 128
SUBLANES = 8
MXU_DIM = 256
HEAD_PAIR = 2 * HEAD_DIM
N_PAIRS = N_HEADS // 2
GROUP = 4 * CHUNK
BAND_TILES = LEFT_CHUNKS * CHUNK // GROUP + 1
BAND = BAND_TILES * GROUP
DIAG_LANES = -(-(GROUP + BAND - 1) // LANES) * LANES
VMEM_LIMIT = 56 * 1024 * 1024
RG_TILE = 512
W_STAGE_ROWS = 64
FFN_CHUNK = 3 * MXU_DIM
MIX_CHUNK = 2 * MXU_DIM
SEG_PITCH = RG_TILE // SUBLANES + SUBLANES // 2
COST_CONV, COST_GATE, COST_SCAN_STEP, COST_OUT = 16, 52, 23, 24


def _rms(x, g):
    var = jnp.mean(x * x, axis=-1, keepdims=True)
    return x * lax.rsqrt(var + EPS) * g


def _resident(shape):
    zeros = (0,) * len(shape)
    return pl.BlockSpec(shape, lambda *_: zeros, pipeline_mode=pl.Buffered(1))


def _load_rounded(src_hbm, dst_ref, stage_ref, sem):
    rows = stage_ref.shape[1]
    n_chunks = src_hbm.shape[0] // rows

    def chunk_copy(c):
        slot = c % 2
        return pltpu.make_async_copy(src_hbm.at[pl.ds(c * rows, rows), :], stage_ref.at[slot],
                                     sem.at[slot])

    chunk_copy(0).start()
    for c in range(n_chunks):
        if c + 1 < n_chunks:
            chunk_copy(c + 1).start()
        chunk_copy(c).wait()
        dst_ref[c * rows:(c + 1) * rows, :] = stage_ref[c % 2].astype(BF16)


def _inproj_kernel(*refs, tiles_per_seq, n_cast):
    (x_ref, g_ref, w_hbm, cw_ref, cb_ref, wr_ref, br_ref, wi_ref, bi_ref, lam_ref), refs = refs[:10], refs[10:]
    cast_in, refs = refs[:n_cast], refs[n_cast:]
    (qkv_ref, gates_ref, hb_ref), refs = refs[:3], refs[3:]
    cast_out, refs = refs[:n_cast], refs[n_cast:]
    (w_ref, stage_ref, stage_sem, xn_ref, gr_ref, xs_ref, xc_ref, rp_ref, ip_ref,
     a_ref, b_ref, hl_ref, pp_ref, h_ref) = refs

    @pl.when(pl.program_id(0) == 0)
    def _():
        _load_rounded(w_hbm, w_ref, stage_ref, stage_sem)

    for src_ref, dst_ref in zip(cast_in, cast_out):
        dst_ref[...] = src_ref[...].astype(BF16)

    @pl.when(pl.program_id(0) % tiles_per_seq == 0)
    def _():
        xs_ref[:, 0:SUBLANES, :] = jnp.zeros((D_MODEL // LANES, SUBLANES, LANES), F32)
        h_ref[...] = jnp.zeros_like(h_ref)

    xn_ref[...] = _rms(x_ref[...], g_ref[...]).astype(BF16)

    def u_cols(c0, width):
        return jnp.dot(xn_ref[...], w_ref[:, c0:c0 + width], preferred_element_type=F32)

    def unit(j, n):
        acc = u_cols(j * D_MODEL + n * MXU_DIM, MXU_DIM)
        if j < 3:
            for k in range(MXU_DIM // HEAD_PAIR):
                hp = n * (MXU_DIM // HEAD_PAIR) + k
                qkv_ref[j * N_PAIRS + hp] = acc[:, k * HEAD_PAIR:(k + 1) * HEAD_PAIR].astype(BF16)
        elif j == 4:
            gr_ref[:, n * MXU_DIM:(n + 1) * MXU_DIM] = acc
        else:
            c0 = (j - 5) * D_MODEL + n * MXU_DIM
            gates_ref[:, c0:c0 + MXU_DIM] = acc.astype(BF16)

    units = [(j, n) for j in (4, 0, 1, 2, 5, 6) for n in range(D_MODEL // MXU_DIM)]
    n_pieces = SUBLANES * (D_MODEL // LANES)
    total_cost = n_pieces * (COST_CONV + COST_GATE + COST_OUT) + (x_ref.shape[0] // SUBLANES) * COST_SCAN_STEP
    cost_per_unit = total_cost / len(units)
    spent = [0.0]

    def side_work(cost):
        spent[0] += cost
        while units and spent[0] >= cost_per_unit:
            spent[0] -= cost_per_unit
            unit(*units.pop(0))

    _rglru_tile(u_cols(3 * D_MODEL, D_MODEL), gr_ref, side_work,
                cw_ref, cb_ref, wr_ref, br_ref, wi_ref, bi_ref, lam_ref,
                hb_ref, xs_ref, xc_ref, rp_ref, ip_ref, a_ref, b_ref, hl_ref, pp_ref, h_ref)
    while units:
        unit(*units.pop(0))


BF16_SUBLANES = 2 * SUBLANES


def _cast_slice_spec(rows, cols, n_steps):
    repeat = next(k for k in (1, 2, 4, 8)
                  if rows * k % n_steps == 0 and (rows * k // n_steps) % BF16_SUBLANES == 0)
    return pl.BlockSpec((rows * repeat // n_steps, cols), lambda i: (i // repeat, 0))


def _inproj(x2, g, w, cw, cb, wr, br, wi, bi, lam, later_weights, seq):
    t = x2.shape[0]
    tm = RG_TILE
    n_steps = t // tm
    n_diag = D_MODEL // MXU_DIM
    scan_scratch = pltpu.VMEM((D_MODEL // LANES, SUBLANES * SEG_PITCH, LANES), F32)
    slab_scratch = pltpu.VMEM((tm, MXU_DIM), F32)
    cast_specs = [_cast_slice_spec(*wt.shape, n_steps) for wt in later_weights]
    outs = pl.pallas_call(
        functools.partial(_inproj_kernel, tiles_per_seq=seq // tm, n_cast=len(later_weights)),
        grid=(n_steps,),
        in_specs=[
            pl.BlockSpec((tm, D_MODEL), lambda i: (i, 0)),
            _resident((1, D_MODEL)),
            pl.BlockSpec(memory_space=pl.ANY),
            _resident((CONV_W, D_MODEL)),
            _resident((1, D_MODEL)),
            _resident((n_diag, MXU_DIM, MXU_DIM)),
            _resident((1, D_MODEL)),
            _resident((n_diag, MXU_DIM, MXU_DIM)),
            _resident((1, D_MODEL)),
            _resident((1, D_MODEL)),
            *cast_specs,
        ],
        out_specs=[
            pl.BlockSpec((3 * N_PAIRS, tm, HEAD_PAIR), lambda i: (0, i, 0)),
            pl.BlockSpec((tm, 2 * D_MODEL), lambda i: (i, 0)),
            pl.BlockSpec((tm, D_MODEL), lambda i: (i, 0)),
            *cast_specs,
        ],
        out_shape=[
            jax.ShapeDtypeStruct((3 * N_PAIRS, t, HEAD_PAIR), BF16),
            jax.ShapeDtypeStruct((t, 2 * D_MODEL), BF16),
            jax.ShapeDtypeStruct((t, D_MODEL), BF16),
            *[jax.ShapeDtypeStruct(wt.shape, BF16) for wt in later_weights],
        ],
        scratch_shapes=[
            pltpu.VMEM(w.shape, BF16),
            pltpu.VMEM((2, W_STAGE_ROWS, w.shape[1]), F32),
            pltpu.SemaphoreType.DMA((2,)),
            pltpu.VMEM((tm, D_MODEL), BF16),
            pltpu.VMEM((tm, D_MODEL), F32),
            pltpu.VMEM((D_MODEL // LANES, tm + SUBLANES, LANES), F32),
            slab_scratch, slab_scratch, slab_scratch,
            scan_scratch, scan_scratch, scan_scratch, scan_scratch,
            pltpu.VMEM((SUBLANES, D_MODEL), F32),
        ],
        compiler_params=pltpu.CompilerParams(
            dimension_semantics=("arbitrary",), vmem_limit_bytes=VMEM_LIMIT),
        name="inproj_rglru",
    )(x2, g, w, cw, cb, wr, br, wi, bi, lam, *later_weights)
    return outs[0], outs[1], outs[2], outs[3:]


def _band_geometry(ntiles):
    kw = ntiles * GROUP
    return kw, BAND - kw


def _attn_scores(q_ref, k_ref, s_ref, r0, ntiles):
    kw, c_off = _band_geometry(ntiles)
    lane = lax.broadcasted_iota(jnp.int32, (GROUP, HEAD_PAIR), 1)
    first_head = lane < HEAD_DIM
    qp = q_ref[0, pl.ds(r0, GROUP), :] * jnp.asarray(1.0 / math.sqrt(HEAD_DIM), BF16)
    zero = jnp.zeros_like(qp)
    qq = jnp.concatenate([jnp.where(first_head, qp, zero),
                          jnp.where(first_head, zero, qp)], axis=0)
    kk = k_ref[0, pl.ds(r0 + GROUP - kw, kw), :]
    s_ref[:, c_off:] = lax.dot_general(qq, kk, (((1,), (1,)), ((), ())),
                                       preferred_element_type=F32)


def _attn_softmax(s_ref, bias_ref, p_ref, l_ref, ntiles):
    _, c_off = _band_geometry(ntiles)
    rb = 32
    lane_rb = lax.broadcasted_iota(jnp.int32, (rb, LANES), 1)
    for hh in range(2):
        for i in range(GROUP // CHUNK):
            lo_col = max(i * CHUNK, c_off)
            hi_col = i * CHUNK + (LEFT_CHUNKS + 1) * CHUNK
            ws = (lo_col // LANES) * LANES
            we = -(-hi_col // LANES) * LANES
            nv = (we - ws) // LANES
            rows_all = slice(hh * GROUP + i * CHUNK, hh * GROUP + (i + 1) * CHUNK)
            if ws > c_off:
                p_ref[rows_all, c_off:ws] = jnp.zeros((CHUNK, ws - c_off), BF16)
            if we < BAND:
                p_ref[rows_all, we:BAND] = jnp.zeros((CHUNK, BAND - we), BF16)
            for b in range(CHUNK // rb):
                rows = slice(hh * GROUP + i * CHUNK + b * rb, hh * GROUP + i * CHUNK + (b + 1) * rb)
                brows = slice(i * CHUNK + b * rb, i * CHUNK + (b + 1) * rb)
                x = s_ref[rows, ws:we] + bias_ref[hh, brows, ws:we]
                cols = [x[:, c * LANES:(c + 1) * LANES] for c in range(nv)]
                if lo_col % LANES:
                    cols[0] = jnp.where(lane_rb >= HEAD_DIM, cols[0], NEG_INF)
                if hi_col % LANES:
                    cols[-1] = jnp.where(lane_rb < HEAD_DIM, cols[-1], NEG_INF)
                x = jnp.concatenate(cols, axis=1)
                m = jnp.max(x, axis=-1, keepdims=True)
                e = jnp.exp(x - m)
                l = jnp.sum(e, axis=-1, keepdims=True)
                p_ref[rows, ws:we] = e.astype(BF16)
                l_ref[rows, :] = jnp.broadcast_to(l, (rb, LANES))

def _attn_values(p_ref, l_ref, v_ref, o_ref, r0, ntiles):
    kw, c_off = _band_geometry(ntiles)
    lane = lax.broadcasted_iota(jnp.int32, (GROUP, HEAD_PAIR), 1)
    vv = v_ref[0, pl.ds(r0 + GROUP - kw, kw), :]
    o2 = jnp.dot(p_ref[:, c_off:], vv, preferred_element_type=F32)
    o2 = o2 / l_ref[...]
    o = jnp.where(lane < HEAD_DIM, o2[:GROUP], o2[GROUP:])
    o_ref[0, pl.ds(r0, GROUP), :] = o.astype(BF16)


def _attn_kernel(q_ref, k_ref, v_ref, w_ref, o_ref, bias_ref, s0, s1, p0, p1, l0, l1):
    n_groups = q_ref.shape[1] // GROUP
    full = BAND_TILES
    row0 = lambda g: pl.multiple_of(g * GROUP, GROUP)

    @pl.when(pl.program_id(1) == 0)
    def _():
        for hh in range(2):
            diag = jnp.broadcast_to(w_ref[hh], (GROUP, DIAG_LANES))
            rolled = pltpu.roll(diag, DIAG_LANES - (GROUP - 1), 1, stride=1, stride_axis=0)
            bias_ref[hh] = rolled[:, :BAND]

    def scores(g, s_ref, ntiles=full):
        _attn_scores(q_ref, k_ref, s_ref, row0(g), ntiles)

    def softmax(s_ref, p_ref, l_ref, ntiles=full):
        _attn_softmax(s_ref, bias_ref, p_ref, l_ref, ntiles)

    def values(g, p_ref, l_ref, ntiles=full):
        _attn_values(p_ref, l_ref, v_ref, o_ref, row0(g), ntiles)

    bufs = ((s0, p0, l0), (s1, p1, l1))
    ntiles = lambda g: min(g + 1, full)
    for t in range(n_groups + 2):
        if t >= 2:
            _, p_ref, l_ref = bufs[t % 2]
            values(t - 2, p_ref, l_ref, ntiles(t - 2))
        if 1 <= t <= n_groups:
            softmax(*bufs[(t - 1) % 2], ntiles(t - 1))
        if t < n_groups:
            scores(t, bufs[t % 2][0], ntiles(t))


def _attention(qkv, diagonals, batch, seq):
    t = batch * seq
    return pl.pallas_call(
        _attn_kernel,
        grid=(N_PAIRS, batch),
        in_specs=[
            pl.BlockSpec((1, seq, HEAD_PAIR), lambda hp, b: (hp, b, 0)),
            pl.BlockSpec((1, seq, HEAD_PAIR), lambda hp, b: (N_PAIRS + hp, b, 0)),
            pl.BlockSpec((1, seq, HEAD_PAIR), lambda hp, b: (2 * N_PAIRS + hp, b, 0)),
            pl.BlockSpec((2, 1, DIAG_LANES), lambda hp, b: (hp, 0, 0)),
        ],
        out_specs=pl.BlockSpec((1, seq, HEAD_PAIR), lambda hp, b: (hp, b, 0)),
        out_shape=jax.ShapeDtypeStruct((N_PAIRS, t, HEAD_PAIR), BF16),
        scratch_shapes=[
            pltpu.VMEM((2, GROUP, BAND), F32),
            pltpu.VMEM((2 * GROUP, BAND), F32),
            pltpu.VMEM((2 * GROUP, BAND), F32),
            pltpu.VMEM((2 * GROUP, BAND), BF16),
            pltpu.VMEM((2 * GROUP, BAND), BF16),
            pltpu.VMEM((2 * GROUP, LANES), F32),
            pltpu.VMEM((2 * GROUP, LANES), F32),
        ],
        compiler_params=pltpu.CompilerParams(
            dimension_semantics=("parallel", "arbitrary"), vmem_limit_bytes=VMEM_LIMIT),
        name="band_attention",
    )(qkv, qkv, qkv, diagonals)


def _band_bias_diagonals(rel_table):
    j = jnp.arange(DIAG_LANES)
    dist = (GROUP - 1 - j) + (BAND - GROUP)
    w = rel_table[:, jnp.clip(dist, -MAX_REL, MAX_REL) + MAX_REL]
    return w[:, None, :]


def _gelu_tanh(x):
    k = math.sqrt(2.0 / math.pi)
    inner = x * (k + (k * 0.044715) * (x * x))
    return x * (0.5 + 0.5 * jnp.tanh(inner))


def _rglru_tile(xr, gr_ref, side_work, cw_ref, cb_ref, wr_ref, br_ref, wi_ref, bi_ref, lam_ref,
                o_ref, xs_ref, xc_ref, rp_ref, ip_ref, a_ref, b_ref, hl_ref, pp_ref, h_ref):
    tm = xr.shape[0]
    seg = tm // SUBLANES
    n_lane_tiles = D_MODEL // LANES
    tiles_per_slab = MXU_DIM // LANES
    lane_tile = lambda c: slice(c * LANES, (c + 1) * LANES)

    for c in range(n_lane_tiles):
        xs_ref[c, SUBLANES:, :] = xr[:, lane_tile(c)]

    for d in range(D_MODEL // MXU_DIM):
        for k in range(tiles_per_slab):
            c = d * tiles_per_slab + k
            for j in range(SUBLANES):
                acc = cb_ref[:, lane_tile(c)]
                for tap in range(CONV_W):
                    off = SUBLANES - (CONV_W - 1) + tap + j * seg
                    acc = acc + xs_ref[c, off:off + seg, :] * cw_ref[tap:tap + 1, lane_tile(c)]
                xc_ref[j * seg:(j + 1) * seg, lane_tile(k)] = acc
                side_work(COST_CONV)
            xs_ref[c, 0:SUBLANES, :] = xs_ref[c, tm:tm + SUBLANES, :]
        xcb = xc_ref[...].astype(BF16)
        rp_ref[...] = jnp.dot(xcb, wr_ref[d], preferred_element_type=F32)
        ip_ref[...] = jnp.dot(xcb, wi_ref[d], preferred_element_type=F32)
        for k in range(tiles_per_slab):
            c = d * tiles_per_slab + k
            nlam = -lam_ref[:, lane_tile(c)]
            softplus = jnp.maximum(nlam, 0.0) + jnp.log1p(jnp.exp(-jnp.abs(nlam)))
            log_a_scale = (-RG_C) * softplus
            for j in range(SUBLANES):
                rows = slice(j * seg, (j + 1) * seg)
                xc = xc_ref[rows, lane_tile(k)]
                r = jax.nn.sigmoid(rp_ref[rows, lane_tile(k)] + br_ref[:, lane_tile(c)])
                ig = jax.nn.sigmoid(ip_ref[rows, lane_tile(k)] + bi_ref[:, lane_tile(c)])
                log_a = r * log_a_scale
                a = jnp.exp(log_a)
                y = jnp.tanh(log_a) * (-1.0 - a * a)
                mult = jnp.where(y > 0.0, y * lax.rsqrt(y), 0.0)
                srows = slice(j * SEG_PITCH, j * SEG_PITCH + seg)
                a_ref[c, srows, :] = a
                b_ref[c, srows, :] = mult * (ig * xc)
                side_work(COST_GATE)

    hs = [jnp.zeros((SUBLANES, LANES), F32) for _ in range(n_lane_tiles)]
    ps = [jnp.ones((SUBLANES, LANES), F32) for _ in range(n_lane_tiles)]
    for t in range(seg):
        idx = pl.ds(t, SUBLANES, stride=SEG_PITCH)
        for c in range(n_lane_tiles):
            a_t = a_ref[c, idx, :]
            hs[c] = a_t * hs[c] + b_ref[c, idx, :]
            ps[c] = a_t * ps[c]
            hl_ref[c, idx, :] = hs[c]
            pp_ref[c, idx, :] = ps[c]
        side_work(COST_SCAN_STEP)
    h_end = jnp.concatenate(hs, axis=1)
    p_end = jnp.concatenate(ps, axis=1)

    h_in = h_ref[0:1, :]
    for j in range(SUBLANES):
        rows = slice(j * seg, (j + 1) * seg)
        srows = slice(j * SEG_PITCH, j * SEG_PITCH + seg)
        for c in range(n_lane_tiles):
            hj = hl_ref[c, srows, :] + pp_ref[c, srows, :] * h_in[:, lane_tile(c)]
            o_ref[rows, lane_tile(c)] = (hj * _gelu_tanh(gr_ref[rows, lane_tile(c)])).astype(BF16)
            side_work(COST_OUT)
        h_in = p_end[j:j + 1, :] * h_in + h_end[j:j + 1, :]
    h_ref[0:1, :] = h_in


def _block_diag_tiles(w):
    per = MXU_DIM // RG_BLOCK
    n_diag = D_MODEL // MXU_DIM
    w4 = w.reshape(n_diag, per, RG_BLOCK, RG_BLOCK)
    eye = jnp.eye(per, dtype=w.dtype)
    tiles = jnp.einsum('dpij,pq->dpiqj', w4, eye)
    return tiles.reshape(n_diag, MXU_DIM, MXU_DIM)


def _mixffn_kernel(x_ref, at_ref, hb_ref, ga_ref, gb_ref, bm_ref, wa_ref, wb_ref, wo_ref,
                   gn_ref, win_ref, wout_ref, gf_ref, o_ref, *, splits):
    attn = jnp.concatenate([at_ref[hp] for hp in range(N_PAIRS)], axis=1)
    hb = hb_ref[...]
    mix = []
    for c0 in range(0, D_MODEL, MIX_CHUNK):
        cols = slice(c0, c0 + MIX_CHUNK)
        ya = jnp.dot(attn, wa_ref[:, cols], preferred_element_type=F32)
        yb = jnp.dot(hb, wb_ref[:, cols], preferred_element_type=F32)
        g_a = jax.nn.sigmoid(ga_ref[:, cols].astype(F32) + bm_ref[:, cols])
        g_b = jax.nn.sigmoid(gb_ref[:, cols].astype(F32) + bm_ref[:, D_MODEL + c0:D_MODEL + c0 + MIX_CHUNK])
        mix.append((g_a * ya + g_b * yb).astype(BF16))
    mix = jnp.concatenate(mix, axis=1)
    h = x_ref[...] + jnp.dot(mix, wo_ref[...], preferred_element_type=F32)
    hn = _rms(h, gn_ref[...]).astype(BF16)
    d_ff = wout_ref.shape[0]
    acc = h
    for c0, c1 in splits:
        g = jnp.dot(hn, win_ref[:, c0:c1], preferred_element_type=F32)
        up = jnp.dot(hn, win_ref[:, d_ff + c0:d_ff + c1], preferred_element_type=F32)
        act = (jax.nn.silu(g) * up).astype(BF16)
        acc = acc + jnp.dot(act, wout_ref[c0:c1, :], preferred_element_type=F32)
    o_ref[...] = _rms(acc, gf_ref[...])


def _mixffn(x2, attn, hb, gates, bm, wa, wb, wo, gn, win, wout, gf, tm):
    t = x2.shape[0]
    d_ff = wout.shape[0]
    edges = list(range(0, d_ff, FFN_CHUNK)) + [d_ff]
    splits = tuple(zip(edges[:-1], edges[1:]))
    row = lambda i: (i, 0)
    return pl.pallas_call(
        functools.partial(_mixffn_kernel, splits=splits),
        grid=(t // tm,),
        in_specs=[
            pl.BlockSpec((tm, D_MODEL), row),
            pl.BlockSpec((N_PAIRS, tm, HEAD_PAIR), lambda i: (0, i, 0)),
            pl.BlockSpec((tm, D_MODEL), row),
            pl.BlockSpec((tm, D_MODEL), lambda i: (i, 0)),
            pl.BlockSpec((tm, D_MODEL), lambda i: (i, 1)),
            _resident((1, 2 * D_MODEL)),
            _resident((D_MODEL, D_MODEL)),
            _resident((D_MODEL, D_MODEL)),
            _resident((D_MODEL, D_MODEL)),
            _resident((1, D_MODEL)),
            _resident((D_MODEL, 2 * d_ff)),
            _resident((d_ff, D_MODEL)),
            _resident((1, D_MODEL)),
        ],
        out_specs=pl.BlockSpec((tm, D_MODEL), row),
        out_shape=jax.ShapeDtypeStruct((t, D_MODEL), F32),
        compiler_params=pltpu.CompilerParams(
            dimension_semantics=("parallel",), vmem_limit_bytes=VMEM_LIMIT),
        name="mix_ffn",
    )(x2, attn, hb, gates, gates, bm, wa, wb, wo, gn, win, wout, gf)


def kernel(x, norm_mix_g, w_in, b_merge, rel_table, w_attn_out, conv_w, conv_b, w_rg_r, b_rg_r,
           w_rg_i, b_rg_i, rg_lambda, w_rnn_out, w_o, norm_ffn_g, w_ffn_in, w_ffn_out, final_norm_g):
    batch, seq, d = x.shape
    assert d == D_MODEL and seq % GROUP == 0 and seq // GROUP >= BAND_TILES
    assert w_in.shape[0] == 1, "single-layer block"
    x2 = x.reshape(batch * seq, d)
    row = lambda v: v.reshape(1, -1)

    assert seq % RG_TILE == 0
    later_weights = (w_attn_out[0], w_rnn_out[0], w_o[0], w_ffn_in[0], w_ffn_out[0])
    qkv, gates, hb, (wa, wb, wo, wfi, wfo) = _inproj(
        x2, row(norm_mix_g[0]), w_in[0], conv_w[0], row(conv_b[0]),
        _block_diag_tiles(w_rg_r[0]).astype(BF16), row(b_rg_r[0]),
        _block_diag_tiles(w_rg_i[0]).astype(BF16), row(b_rg_i[0]), row(rg_lambda[0]),
        later_weights, seq)
    attn = _attention(qkv, _band_bias_diagonals(rel_table[0]), batch, seq)
    out = _mixffn(x2, attn, hb, gates, row(b_merge[0]), wa, wb, wo, row(norm_ffn_g[0]), wfi, wfo,
                  row(final_norm_g), tm=512)
    return out.reshape(batch, seq, d)
```

```python
import functools
import math

import jax
import jax.numpy as jnp
from jax import lax
from jax.experimental import pallas as pl
from jax.experimental.pallas import tpu as pltpu

F32 = jnp.float32
BF16 = jnp.bfloat16

D_MODEL = 1024
CHUNK = 64
LEFT_CHUNKS = 8
N_HEADS = 16
HEAD_DIM = 64
MAX_REL = 128
N_RG_BLOCKS = 16
RG_BLOCK = D_MODEL // N_RG_BLOCKS
CONV_W = 4
RG_C = 8.0
EPS = 1e-6
NEG_INF = -1e30

LANES = 128
SUBLANES = 8
MXU_DIM = 256
HEAD_PAIR = 2 * HEAD_DIM
N_PAIRS = N_HEADS // 2
GROUP = 4 * CHUNK
BAND_TILES = LEFT_CHUNKS * CHUNK // GROUP + 1
BAND = BAND_TILES * GROUP
DIAG_LANES = -(-(GROUP + BAND - 1) // LANES) * LANES
VMEM_LIMIT = 56 * 1024 * 1024
RG_TILE = 512
W_STAGE_ROWS = 64
FFN_CHUNK = 3 * MXU_DIM
SEG_PITCH = RG_TILE // SUBLANES + SUBLANES // 2
COST_CONV, COST_GATE, COST_SCAN_STEP, COST_OUT = 16, 52, 23, 24


def _rms(x, g):
    var = jnp.mean(x * x, axis=-1, keepdims=True)
    return x * lax.rsqrt(var + EPS) * g


def _resident(shape):
    zeros = (0,) * len(shape)
    return pl.BlockSpec(shape, lambda *_: zeros, pipeline_mode=pl.Buffered(1))


def _load_rounded(src_hbm, dst_ref, stage_ref, sem):
    rows = stage_ref.shape[1]
    n_chunks = src_hbm.shape[0] // rows

    def chunk_copy(c):
        slot = c % 2
        return pltpu.make_async_copy(src_hbm.at[pl.ds(c * rows, rows), :], stage_ref.at[slot],
                                     sem.at[slot])

    chunk_copy(0).start()
    for c in range(n_chunks):
        if c + 1 < n_chunks:
            chunk_copy(c + 1).start()
        chunk_copy(c).wait()
        dst_ref[c * rows:(c + 1) * rows, :] = stage_ref[c % 2].astype(BF16)


def _inproj_kernel(*refs, tiles_per_seq, n_cast):
    (x_ref, g_ref, w_hbm, cw_ref, cb_ref, wr_ref, br_ref, wi_ref, bi_ref, lam_ref), refs = refs[:10], refs[10:]
    cast_in, refs = refs[:n_cast], refs[n_cast:]
    (qkv_ref, gates_ref, hb_ref), refs = refs[:3], refs[3:]
    cast_out, refs = refs[:n_cast], refs[n_cast:]
    (w_ref, stage_ref, stage_sem, xn_ref, gr_ref, xs_ref, xc_ref, rp_ref, ip_ref,
     a_ref, b_ref, hl_ref, pp_ref, h_ref) = refs

    @pl.when(pl.program_id(0) == 0)
    def _():
        _load_rounded(w_hbm, w_ref, stage_ref, stage_sem)

    for src_ref, dst_ref in zip(cast_in, cast_out):
        dst_ref[...] = src_ref[...].astype(BF16)

    @pl.when(pl.program_id(0) % tiles_per_seq == 0)
    def _():
        xs_ref[:, 0:SUBLANES, :] = jnp.zeros((D_MODEL // LANES, SUBLANES, LANES), F32)
        h_ref[...] = jnp.zeros_like(h_ref)

    xn_ref[...] = _rms(x_ref[...], g_ref[...]).astype(BF16)

    def u_cols(c0, width):
        return jnp.dot(xn_ref[...], w_ref[:, c0:c0 + width], preferred_element_type=F32)

    def unit(j, n):
        acc = u_cols(j * D_MODEL + n * MXU_DIM, MXU_DIM)
        if j < 3:
            for k in range(MXU_DIM // HEAD_PAIR):
                hp = n * (MXU_DIM // HEAD_PAIR) + k
                qkv_ref[j * N_PAIRS + hp] = acc[:, k * HEAD_PAIR:(k + 1) * HEAD_PAIR].astype(BF16)
        elif j == 4:
            gr_ref[:, n * MXU_DIM:(n + 1) * MXU_DIM] = acc
        else:
            c0 = (j - 5) * D_MODEL + n * MXU_DIM
            gates_ref[:, c0:c0 + MXU_DIM] = acc.astype(BF16)

    units = [(j, n) for j in (4, 0, 1, 2, 5, 6) for n in range(D_MODEL // MXU_DIM)]
    n_pieces = SUBLANES * (D_MODEL // LANES)
    total_cost = n_pieces * (COST_CONV + COST_GATE + COST_OUT) + (x_ref.shape[0] // SUBLANES) * COST_SCAN_STEP
    cost_per_unit = total_cost / len(units)
    spent = [0.0]

    def side_work(cost):
        spent[0] += cost
        while units and spent[0] >= cost_per_unit:
            spent[0] -= cost_per_unit
            unit(*units.pop(0))

    _rglru_tile(u_cols(3 * D_MODEL, D_MODEL), gr_ref, side_work,
                cw_ref, cb_ref, wr_ref, br_ref, wi_ref, bi_ref, lam_ref,
                hb_ref, xs_ref, xc_ref, rp_ref, ip_ref, a_ref, b_ref, hl_ref, pp_ref, h_ref)
    while units:
        unit(*units.pop(0))


BF16_SUBLANES = 2 * SUBLANES


def _cast_slice_spec(rows, cols, n_steps):
    repeat = next(k for k in (1, 2, 4, 8)
                  if rows * k % n_steps == 0 and (rows * k // n_steps) % BF16_SUBLANES == 0)
    return pl.BlockSpec((rows * repeat // n_steps, cols), lambda i: (i // repeat, 0))


def _inproj(x2, g, w, cw, cb, wr, br, wi, bi, lam, later_weights, seq):
    t = x2.shape[0]
    tm = RG_TILE
    n_steps = t // tm
    n_diag = D_MODEL // MXU_DIM
    scan_scratch = pltpu.VMEM((D_MODEL // LANES, SUBLANES * SEG_PITCH, LANES), F32)
    slab_scratch = pltpu.VMEM((tm, MXU_DIM), F32)
    cast_specs = [_cast_slice_spec(*wt.shape, n_steps) for wt in later_weights]
    outs = pl.pallas_call(
        functools.partial(_inproj_kernel, tiles_per_seq=seq // tm, n_cast=len(later_weights)),
        grid=(n_steps,),
        in_specs=[
            pl.BlockSpec((tm, D_MODEL), lambda i: (i, 0)),
            _resident((1, D_MODEL)),
            pl.BlockSpec(memory_space=pl.ANY),
            _resident((CONV_W, D_MODEL)),
            _resident((1, D_MODEL)),
            _resident((n_diag, MXU_DIM, MXU_DIM)),
            _resident((1, D_MODEL)),
            _resident((n_diag, MXU_DIM, MXU_DIM)),
            _resident((1, D_MODEL)),
            _resident((1, D_MODEL)),
            *cast_specs,
        ],
        out_specs=[
            pl.BlockSpec((3 * N_PAIRS, tm, HEAD_PAIR), lambda i: (0, i, 0)),
            pl.BlockSpec((tm, 2 * D_MODEL), lambda i: (i, 0)),
            pl.BlockSpec((tm, D_MODEL), lambda i: (i, 0)),
            *cast_specs,
        ],
        out_shape=[
            jax.ShapeDtypeStruct((3 * N_PAIRS, t, HEAD_PAIR), BF16),
            jax.ShapeDtypeStruct((t, 2 * D_MODEL), BF16),
            jax.ShapeDtypeStruct((t, D_MODEL), BF16),
            *[jax.ShapeDtypeStruct(wt.shape, BF16) for wt in later_weights],
        ],
        scratch_shapes=[
            pltpu.VMEM(w.shape, BF16),
            pltpu.VMEM((2, W_STAGE_ROWS, w.shape[1]), F32),
            pltpu.SemaphoreType.DMA((2,)),
            pltpu.VMEM((tm, D_MODEL), BF16),
            pltpu.VMEM((tm, D_MODEL), F32),
            pltpu.VMEM((D_MODEL // LANES, tm + SUBLANES, LANES), F32),
            slab_scratch, slab_scratch, slab_scratch,
            scan_scratch, scan_scratch, scan_scratch, scan_scratch,
            pltpu.VMEM((SUBLANES, D_MODEL), F32),
        ],
        compiler_params=pltpu.CompilerParams(
            dimension_semantics=("arbitrary",), vmem_limit_bytes=VMEM_LIMIT),
        name="inproj_rglru",
    )(x2, g, w, cw, cb, wr, br, wi, bi, lam, *later_weights)
    return outs[0], outs[1], outs[2], outs[3:]


def _band_geometry(ntiles):
    kw = ntiles * GROUP
    return kw, BAND - kw


def _attn_scores(q_ref, k_ref, s_ref, r0, ntiles):
    kw, c_off = _band_geometry(ntiles)
    lane = lax.broadcasted_iota(jnp.int32, (GROUP, HEAD_PAIR), 1)
    first_head = lane < HEAD_DIM
    qp = q_ref[0, pl.ds(r0, GROUP), :] * jnp.asarray(1.0 / math.sqrt(HEAD_DIM), BF16)
    zero = jnp.zeros_like(qp)
    qq = jnp.concatenate([jnp.where(first_head, qp, zero),
                          jnp.where(first_head, zero, qp)], axis=0)
    kk = k_ref[0, pl.ds(r0 + GROUP - kw, kw), :]
    s_ref[:, c_off:] = lax.dot_general(qq, kk, (((1,), (1,)), ((), ())),
                                       preferred_element_type=F32)


def _attn_softmax(s_ref, bias_ref, p_ref, l_ref, ntiles):
    _, c_off = _band_geometry(ntiles)
    rb = 16
    lane_rb = lax.broadcasted_iota(jnp.int32, (rb, LANES), 1)
    for hh in range(2):
        for i in range(GROUP // CHUNK):
            lo_col = max(i * CHUNK, c_off)
            hi_col = i * CHUNK + (LEFT_CHUNKS + 1) * CHUNK
            ws = (lo_col // LANES) * LANES
            we = -(-hi_col // LANES) * LANES
            nv = (we - ws) // LANES
            rows_all = slice(hh * GROUP + i * CHUNK, hh * GROUP + (i + 1) * CHUNK)
            if ws > c_off:
                p_ref[rows_all, c_off:ws] = jnp.zeros((CHUNK, ws - c_off), BF16)
            if we < BAND:
                p_ref[rows_all, we:BAND] = jnp.zeros((CHUNK, BAND - we), BF16)
            for b in range(CHUNK // rb):
                rows = slice(hh * GROUP + i * CHUNK + b * rb, hh * GROUP + i * CHUNK + (b + 1) * rb)
                brows = slice(i * CHUNK + b * rb, i * CHUNK + (b + 1) * rb)
                x = s_ref[rows, ws:we] + bias_ref[hh, brows, ws:we]
                cols = [x[:, c * LANES:(c + 1) * LANES] for c in range(nv)]
                if lo_col % LANES:
                    cols[0] = jnp.where(lane_rb >= HEAD_DIM, cols[0], NEG_INF)
                if hi_col % LANES:
                    cols[-1] = jnp.where(lane_rb < HEAD_DIM, cols[-1], NEG_INF)
                x = jnp.concatenate(cols, axis=1)
                m = jnp.max(x, axis=-1, keepdims=True)
                e = jnp.exp(x - m)
                l = jnp.sum(e, axis=-1, keepdims=True)
                p_ref[rows, ws:we] = e.astype(BF16)
                l_ref[rows, :] = jnp.broadcast_to(l, (rb, LANES))

def _attn_values(p_ref, l_ref, v_ref, o_ref, r0, ntiles):
    kw, c_off = _band_geometry(ntiles)
    lane = lax.broadcasted_iota(jnp.int32, (GROUP, HEAD_PAIR), 1)
    vv = v_ref[0, pl.ds(r0 + GROUP - kw, kw), :]
    o2 = jnp.dot(p_ref[:, c_off:], vv, preferred_element_type=F32)
    o2 = o2 / l_ref[...]
    o = jnp.where(lane < HEAD_DIM, o2[:GROUP], o2[GROUP:])
    o_ref[0, pl.ds(r0, GROUP), :] = o.astype(BF16)


def _attn_kernel(q_ref, k_ref, v_ref, w_ref, o_ref, bias_ref, s0, s1, p0, p1, l0, l1):
    n_groups = q_ref.shape[1] // GROUP
    full = BAND_TILES
    row0 = lambda g: pl.multiple_of(g * GROUP, GROUP)

    @pl.when(pl.program_id(1) == 0)
    def _():
        for hh in range(2):
            diag = jnp.broadcast_to(w_ref[hh], (GROUP, DIAG_LANES))
            rolled = pltpu.roll(diag, DIAG_LANES - (GROUP - 1), 1, stride=1, stride_axis=0)
            bias_ref[hh] = rolled[:, :BAND]

    def scores(g, s_ref, ntiles=full):
        _attn_scores(q_ref, k_ref, s_ref, row0(g), ntiles)

    def softmax(s_ref, p_ref, l_ref, ntiles=full):
        _attn_softmax(s_ref, bias_ref, p_ref, l_ref, ntiles)

    def values(g, p_ref, l_ref, ntiles=full):
        _attn_values(p_ref, l_ref, v_ref, o_ref, row0(g), ntiles)

    bufs = ((s0, p0, l0), (s1, p1, l1))
    ntiles = lambda g: min(g + 1, full)
    for t in range(n_groups + 2):
        if t >= 2:
            _, p_ref, l_ref = bufs[t % 2]
            values(t - 2, p_ref, l_ref, ntiles(t - 2))
        if 1 <= t <= n_groups:
            softmax(*bufs[(t - 1) % 2], ntiles(t - 1))
        if t < n_groups:
            scores(t, bufs[t % 2][0], ntiles(t))


def _attention(qkv, diagonals, batch, seq):
    t = batch * seq
    return pl.pallas_call(
        _attn_kernel,
        grid=(N_PAIRS, batch),
        in_specs=[
            pl.BlockSpec((1, seq, HEAD_PAIR), lambda hp, b: (hp, b, 0)),
            pl.BlockSpec((1, seq, HEAD_PAIR), lambda hp, b: (N_PAIRS + hp, b, 0)),
            pl.BlockSpec((1, seq, HEAD_PAIR), lambda hp, b: (2 * N_PAIRS + hp, b, 0)),
            pl.BlockSpec((2, 1, DIAG_LANES), lambda hp, b: (hp, 0, 0)),
        ],
        out_specs=pl.BlockSpec((1, seq, HEAD_PAIR), lambda hp, b: (hp, b, 0)),
        out_shape=jax.ShapeDtypeStruct((N_PAIRS, t, HEAD_PAIR), BF16),
        scratch_shapes=[
            pltpu.VMEM((2, GROUP, BAND), F32),
            pltpu.VMEM((2 * GROUP, BAND), F32),
            pltpu.VMEM((2 * GROUP, BAND), F32),
            pltpu.VMEM((2 * GROUP, BAND), BF16),
            pltpu.VMEM((2 * GROUP, BAND), BF16),
            pltpu.VMEM((2 * GROUP, LANES), F32),
            pltpu.VMEM((2 * GROUP, LANES), F32),
        ],
        compiler_params=pltpu.CompilerParams(
            dimension_semantics=("parallel", "arbitrary"), vmem_limit_bytes=VMEM_LIMIT),
        name="band_attention",
    )(qkv, qkv, qkv, diagonals)


def _band_bias_diagonals(rel_table):
    j = jnp.arange(DIAG_LANES)
    dist = (GROUP - 1 - j) + (BAND - GROUP)
    w = rel_table[:, jnp.clip(dist, -MAX_REL, MAX_REL) + MAX_REL]
    return w[:, None, :]


def _gelu_tanh(x):
    k = math.sqrt(2.0 / math.pi)
    inner = x * (k + (k * 0.044715) * (x * x))
    return x * (0.5 + 0.5 * jnp.tanh(inner))


def _rglru_tile(xr, gr_ref, side_work, cw_ref, cb_ref, wr_ref, br_ref, wi_ref, bi_ref, lam_ref,
                o_ref, xs_ref, xc_ref, rp_ref, ip_ref, a_ref, b_ref, hl_ref, pp_ref, h_ref):
    tm = xr.shape[0]
    seg = tm // SUBLANES
    n_lane_tiles = D_MODEL // LANES
    tiles_per_slab = MXU_DIM // LANES
    lane_tile = lambda c: slice(c * LANES, (c + 1) * LANES)

    for c in range(n_lane_tiles):
        xs_ref[c, SUBLANES:, :] = xr[:, lane_tile(c)]

    for d in range(D_MODEL // MXU_DIM):
        for k in range(tiles_per_slab):
            c = d * tiles_per_slab + k
            for j in range(SUBLANES):
                acc = cb_ref[:, lane_tile(c)]
                for tap in range(CONV_W):
                    off = SUBLANES - (CONV_W - 1) + tap + j * seg
                    acc = acc + xs_ref[c, off:off + seg, :] * cw_ref[tap:tap + 1, lane_tile(c)]
                xc_ref[j * seg:(j + 1) * seg, lane_tile(k)] = acc
                side_work(COST_CONV)
            xs_ref[c, 0:SUBLANES, :] = xs_ref[c, tm:tm + SUBLANES, :]
        xcb = xc_ref[...].astype(BF16)
        rp_ref[...] = jnp.dot(xcb, wr_ref[d], preferred_element_type=F32)
        ip_ref[...] = jnp.dot(xcb, wi_ref[d], preferred_element_type=F32)
        for k in range(tiles_per_slab):
            c = d * tiles_per_slab + k
            nlam = -lam_ref[:, lane_tile(c)]
            softplus = jnp.maximum(nlam, 0.0) + jnp.log1p(jnp.exp(-jnp.abs(nlam)))
            log_a_scale = (-RG_C) * softplus
            for j in range(SUBLANES):
                rows = slice(j * seg, (j + 1) * seg)
                xc = xc_ref[rows, lane_tile(k)]
                r = jax.nn.sigmoid(rp_ref[rows, lane_tile(k)] + br_ref[:, lane_tile(c)])
                ig = jax.nn.sigmoid(ip_ref[rows, lane_tile(k)] + bi_ref[:, lane_tile(c)])
                log_a = r * log_a_scale
                a = jnp.exp(log_a)
                y = jnp.tanh(log_a) * (-1.0 - a * a)
                mult = jnp.where(y > 0.0, y * lax.rsqrt(y), 0.0)
                srows = slice(j * SEG_PITCH, j * SEG_PITCH + seg)
                a_ref[c, srows, :] = a
                b_ref[c, srows, :] = mult * (ig * xc)
                side_work(COST_GATE)

    hs = [jnp.zeros((SUBLANES, LANES), F32) for _ in range(n_lane_tiles)]
    ps = [jnp.ones((SUBLANES, LANES), F32) for _ in range(n_lane_tiles)]
    for t in range(seg):
        idx = pl.ds(t, SUBLANES, stride=SEG_PITCH)
        for c in range(n_lane_tiles):
            a_t = a_ref[c, idx, :]
            hs[c] = a_t * hs[c] + b_ref[c, idx, :]
            ps[c] = a_t * ps[c]
            hl_ref[c, idx, :] = hs[c]
            pp_ref[c, idx, :] = ps[c]
        side_work(COST_SCAN_STEP)
    h_end = jnp.concatenate(hs, axis=1)
    p_end = jnp.concatenate(ps, axis=1)

    h_in = h_ref[0:1, :]
    for j in range(SUBLANES):
        rows = slice(j * seg, (j + 1) * seg)
        srows = slice(j * SEG_PITCH, j * SEG_PITCH + seg)
        for c in range(n_lane_tiles):
            hj = hl_ref[c, srows, :] + pp_ref[c, srows, :] * h_in[:, lane_tile(c)]
            o_ref[rows, lane_tile(c)] = (hj * _gelu_tanh(gr_ref[rows, lane_tile(c)])).astype(BF16)
            side_work(COST_OUT)
        h_in = p_end[j:j + 1, :] * h_in + h_end[j:j + 1, :]
    h_ref[0:1, :] = h_in


def _block_diag_tiles(w):
    per = MXU_DIM // RG_BLOCK
    n_diag = D_MODEL // MXU_DIM
    w4 = w.reshape(n_diag, per, RG_BLOCK, RG_BLOCK)
    eye = jnp.eye(per, dtype=w.dtype)
    tiles = jnp.einsum('dpij,pq->dpiqj', w4, eye)
    return tiles.reshape(n_diag, MXU_DIM, MXU_DIM)


def _mixffn_kernel(x_ref, at_ref, hb_ref, ga_ref, gb_ref, bm_ref, wa_ref, wb_ref, wo_ref,
                   gn_ref, win_ref, wout_ref, gf_ref, o_ref, *, splits):
    attn = jnp.concatenate([at_ref[hp] for hp in range(N_PAIRS)], axis=1)
    ya = jnp.dot(attn, wa_ref[...], preferred_element_type=F32)
    yb = jnp.dot(hb_ref[...], wb_ref[...], preferred_element_type=F32)
    g_a = jax.nn.sigmoid(ga_ref[...].astype(F32) + bm_ref[:, :D_MODEL])
    g_b = jax.nn.sigmoid(gb_ref[...].astype(F32) + bm_ref[:, D_MODEL:])
    mix = (g_a * ya + g_b * yb).astype(BF16)
    h = x_ref[...] + jnp.dot(mix, wo_ref[...], preferred_element_type=F32)
    hn = _rms(h, gn_ref[...]).astype(BF16)
    d_ff = wout_ref.shape[0]
    acc = h
    for c0, c1 in splits:
        g = jnp.dot(hn, win_ref[:, c0:c1], preferred_element_type=F32)
        up = jnp.dot(hn, win_ref[:, d_ff + c0:d_ff + c1], preferred_element_type=F32)
        act = (jax.nn.silu(g) * up).astype(BF16)
        acc = acc + jnp.dot(act, wout_ref[c0:c1, :], preferred_element_type=F32)
    o_ref[...] = _rms(acc, gf_ref[...])


def _mixffn(x2, attn, hb, gates, bm, wa, wb, wo, gn, win, wout, gf, tm):
    t = x2.shape[0]
    d_ff = wout.shape[0]
    edges = list(range(0, d_ff, FFN_CHUNK)) + [d_ff]
    splits = tuple(zip(edges[:-1], edges[1:]))
    row = lambda i: (i, 0)
    return pl.pallas_call(
        functools.partial(_mixffn_kernel, splits=splits),
        grid=(t // tm,),
        in_specs=[
            pl.BlockSpec((tm, D_MODEL), row),
            pl.BlockSpec((N_PAIRS, tm, HEAD_PAIR), lambda i: (0, i, 0)),
            pl.BlockSpec((tm, D_MODEL), row),
            pl.BlockSpec((tm, D_MODEL), lambda i: (i, 0)),
            pl.BlockSpec((tm, D_MODEL), lambda i: (i, 1)),
            _resident((1, 2 * D_MODEL)),
            _resident((D_MODEL, D_MODEL)),
            _resident((D_MODEL, D_MODEL)),
            _resident((D_MODEL, D_MODEL)),
            _resident((1, D_MODEL)),
            _resident((D_MODEL, 2 * d_ff)),
            _resident((d_ff, D_MODEL)),
            _resident((1, D_MODEL)),
        ],
        out_specs=pl.BlockSpec((tm, D_MODEL), row),
        out_shape=jax.ShapeDtypeStruct((t, D_MODEL), F32),
        compiler_params=pltpu.CompilerParams(
            dimension_semantics=("parallel",), vmem_limit_bytes=VMEM_LIMIT),
        name="mix_ffn",
    )(x2, attn, hb, gates, gates, bm, wa, wb, wo, gn, win, wout, gf)


def kernel(x, norm_mix_g, w_in, b_merge, rel_table, w_attn_out, conv_w, conv_b, w_rg_r, b_rg_r,
           w_rg_i, b_rg_i, rg_lambda, w_rnn_out, w_o, norm_ffn_g, w_ffn_in, w_ffn_out, final_norm_g):
    batch, seq, d = x.shape
    assert d == D_MODEL and seq % GROUP == 0 and seq // GROUP >= BAND_TILES
    assert w_in.shape[0] == 1, "single-layer block"
    x2 = x.reshape(batch * seq, d)
    row = lambda v: v.reshape(1, -1)

    assert seq % RG_TILE == 0
    later_weights = (w_attn_out[0], w_rnn_out[0], w_o[0], w_ffn_in[0], w_ffn_out[0])
    qkv, gates, hb, (wa, wb, wo, wfi, wfo) = _inproj(
        x2, row(norm_mix_g[0]), w_in[0], conv_w[0], row(conv_b[0]),
        _block_diag_tiles(w_rg_r[0]).astype(BF16), row(b_rg_r[0]),
        _block_diag_tiles(w_rg_i[0]).astype(BF16), row(b_rg_i[0]), row(rg_lambda[0]),
        later_weights, seq)
    attn = _attention(qkv, _band_bias_diagonals(rel_table[0]), batch, seq)
    out = _mixffn(x2, attn, hb, gates, row(b_merge[0]), wa, wb, wo, row(norm_ffn_g[0]), wfi, wfo,
                  row(final_norm_g), tm=512)
    return out.reshape(batch, seq, d)
```

```python
import functools
import math

import jax
import jax.numpy as jnp
from jax import lax
from jax.experimental import pallas as pl
from jax.experimental.pallas import tpu as pltpu

F32 = jnp.float32
BF16 = jnp.bfloat16

D_MODEL = 1024
CHUNK = 64
LEFT_CHUNKS = 8
N_HEADS = 16
HEAD_DIM = 64
MAX_REL = 128
N_RG_BLOCKS = 16
RG_BLOCK = D_MODEL // N_RG_BLOCKS
CONV_W = 4
RG_C = 8.0
EPS = 1e-6
NEG_INF = -1e30

LANES = 128
SUBLANES = 8
MXU_DIM = 256
HEAD_PAIR = 2 * HEAD_DIM
N_PAIRS = N_HEADS // 2
GROUP = 4 * CHUNK
BAND_TILES = LEFT_CHUNKS * CHUNK // GROUP + 1
BAND = BAND_TILES * GROUP
DIAG_LANES = -(-(GROUP + BAND - 1) // LANES) * LANES
VMEM_LIMIT = 56 * 1024 * 1024
RG_TILE = 512
W_STAGE_ROWS = 64
FFN_CHUNK = 3 * MXU_DIM
SEG_PITCH = RG_TILE // SUBLANES + SUBLANES // 2
COST_CONV, COST_GATE, COST_SCAN_STEP, COST_OUT = 16, 52, 23, 24


def _rms(x, g):
    var = jnp.mean(x * x, axis=-1, keepdims=True)
    return x * lax.rsqrt(var + EPS) * g


def _resident(shape):
    zeros = (0,) * len(shape)
    return pl.BlockSpec(shape, lambda *_: zeros, pipeline_mode=pl.Buffered(1))


def _load_rounded(src_hbm, dst_ref, stage_ref, sem):
    rows = stage_ref.shape[1]
    n_chunks = src_hbm.shape[0] // rows

    def chunk_copy(c):
        slot = c % 2
        return pltpu.make_async_copy(src_hbm.at[pl.ds(c * rows, rows), :], stage_ref.at[slot],
                                     sem.at[slot])

    chunk_copy(0).start()
    for c in range(n_chunks):
        if c + 1 < n_chunks:
            chunk_copy(c + 1).start()
        chunk_copy(c).wait()
        dst_ref[c * rows:(c + 1) * rows, :] = stage_ref[c % 2].astype(BF16)


def _inproj_kernel(*refs, tiles_per_seq, n_cast):
    (x_ref, g_ref, w_hbm, cw_ref, cb_ref, wr_ref, br_ref, wi_ref, bi_ref, lam_ref), refs = refs[:10], refs[10:]
    cast_in, refs = refs[:n_cast], refs[n_cast:]
    (qkv_ref, gates_ref, hb_ref), refs = refs[:3], refs[3:]
    cast_out, refs = refs[:n_cast], refs[n_cast:]
    (w_ref, stage_ref, stage_sem, xn_ref, gr_ref, xs_ref, xc_ref, rp_ref, ip_ref,
     a_ref, b_ref, hl_ref, pp_ref, h_ref) = refs

    @pl.when(pl.program_id(0) == 0)
    def _():
        _load_rounded(w_hbm, w_ref, stage_ref, stage_sem)

    for src_ref, dst_ref in zip(cast_in, cast_out):
        dst_ref[...] = src_ref[...].astype(BF16)

    @pl.when(pl.program_id(0) % tiles_per_seq == 0)
    def _():
        xs_ref[:, 0:SUBLANES, :] = jnp.zeros((D_MODEL // LANES, SUBLANES, LANES), F32)
        h_ref[...] = jnp.zeros_like(h_ref)

    xn_ref[...] = _rms(x_ref[...], g_ref[...]).astype(BF16)

    def u_cols(c0, width):
        return jnp.dot(xn_ref[...], w_ref[:, c0:c0 + width], preferred_element_type=F32)

    def unit(j, n):
        acc = u_cols(j * D_MODEL + n * MXU_DIM, MXU_DIM)
        if j < 3:
            for k in range(MXU_DIM // HEAD_PAIR):
                hp = n * (MXU_DIM // HEAD_PAIR) + k
                qkv_ref[j * N_PAIRS + hp] = acc[:, k * HEAD_PAIR:(k + 1) * HEAD_PAIR].astype(BF16)
        elif j == 4:
            gr_ref[:, n * MXU_DIM:(n + 1) * MXU_DIM] = acc
        else:
            c0 = (j - 5) * D_MODEL + n * MXU_DIM
            gates_ref[:, c0:c0 + MXU_DIM] = acc.astype(BF16)

    units = [(j, n) for j in (4, 0, 1, 2, 5, 6) for n in range(D_MODEL // MXU_DIM)]
    n_pieces = SUBLANES * (D_MODEL // LANES)
    total_cost = n_pieces * (COST_CONV + COST_GATE + COST_OUT) + (x_ref.shape[0] // SUBLANES) * COST_SCAN_STEP
    cost_per_unit = total_cost / len(units)
    spent = [0.0]

    def side_work(cost):
        spent[0] += cost
        while units and spent[0] >= cost_per_unit:
            spent[0] -= cost_per_unit
            unit(*units.pop(0))

    _rglru_tile(u_cols(3 * D_MODEL, D_MODEL), gr_ref, side_work,
                cw_ref, cb_ref, wr_ref, br_ref, wi_ref, bi_ref, lam_ref,
                hb_ref, xs_ref, xc_ref, rp_ref, ip_ref, a_ref, b_ref, hl_ref, pp_ref, h_ref)
    while units:
        unit(*units.pop(0))


BF16_SUBLANES = 2 * SUBLANES


def _cast_slice_spec(rows, cols, n_steps):
    repeat = next(k for k in (1, 2, 4, 8)
                  if rows * k % n_steps == 0 and (rows * k // n_steps) % BF16_SUBLANES == 0)
    return pl.BlockSpec((rows * repeat // n_steps, cols), lambda i: (i // repeat, 0))


def _inproj(x2, g, w, cw, cb, wr, br, wi, bi, lam, later_weights, seq):
    t = x2.shape[0]
    tm = RG_TILE
    n_steps = t // tm
    n_diag = D_MODEL // MXU_DIM
    scan_scratch = pltpu.VMEM((D_MODEL // LANES, SUBLANES * SEG_PITCH, LANES), F32)
    slab_scratch = pltpu.VMEM((tm, MXU_DIM), F32)
    cast_specs = [_cast_slice_spec(*wt.shape, n_steps) for wt in later_weights]
    outs = pl.pallas_call(
        functools.partial(_inproj_kernel, tiles_per_seq=seq // tm, n_cast=len(later_weights)),
        grid=(n_steps,),
        in_specs=[
            pl.BlockSpec((tm, D_MODEL), lambda i: (i, 0)),
            _resident((1, D_MODEL)),
            pl.BlockSpec(memory_space=pl.ANY),
            _resident((CONV_W, D_MODEL)),
            _resident((1, D_MODEL)),
            _resident((n_diag, MXU_DIM, MXU_DIM)),
            _resident((1, D_MODEL)),
            _resident((n_diag, MXU_DIM, MXU_DIM)),
            _resident((1, D_MODEL)),
            _resident((1, D_MODEL)),
            *cast_specs,
        ],
        out_specs=[
            pl.BlockSpec((3 * N_PAIRS, tm, HEAD_PAIR), lambda i: (0, i, 0)),
            pl.BlockSpec((tm, 2 * D_MODEL), lambda i: (i, 0)),
            pl.BlockSpec((tm, D_MODEL), lambda i: (i, 0)),
            *cast_specs,
        ],
        out_shape=[
            jax.ShapeDtypeStruct((3 * N_PAIRS, t, HEAD_PAIR), BF16),
            jax.ShapeDtypeStruct((t, 2 * D_MODEL), BF16),
            jax.ShapeDtypeStruct((t, D_MODEL), BF16),
            *[jax.ShapeDtypeStruct(wt.shape, BF16) for wt in later_weights],
        ],
        scratch_shapes=[
            pltpu.VMEM(w.shape, BF16),
            pltpu.VMEM((2, W_STAGE_ROWS, w.shape[1]), F32),
            pltpu.SemaphoreType.DMA((2,)),
            pltpu.VMEM((tm, D_MODEL), BF16),
            pltpu.VMEM((tm, D_MODEL), F32),
            pltpu.VMEM((D_MODEL // LANES, tm + SUBLANES, LANES), F32),
            slab_scratch, slab_scratch, slab_scratch,
            scan_scratch, scan_scratch, scan_scratch, scan_scratch,
            pltpu.VMEM((SUBLANES, D_MODEL), F32),
        ],
        compiler_params=pltpu.CompilerParams(
            dimension_semantics=("arbitrary",), vmem_limit_bytes=VMEM_LIMIT),
        name="inproj_rglru",
    )(x2, g, w, cw, cb, wr, br, wi, bi, lam, *later_weights)
    return outs[0], outs[1], outs[2], outs[3:]


def _band_geometry(ntiles):
    kw = ntiles * GROUP
    return kw, BAND - kw


def _attn_scores(q_ref, k_ref, s_ref, r0, ntiles):
    kw, c_off = _band_geometry(ntiles)
    lane = lax.broadcasted_iota(jnp.int32, (GROUP, HEAD_PAIR), 1)
    first_head = lane < HEAD_DIM
    qp = q_ref[0, pl.ds(r0, GROUP), :] * jnp.asarray(1.0 / math.sqrt(HEAD_DIM), BF16)
    zero = jnp.zeros_like(qp)
    qq = jnp.concatenate([jnp.where(first_head, qp, zero),
                          jnp.where(first_head, zero, qp)], axis=0)
    kk = k_ref[0, pl.ds(r0 + GROUP - kw, kw), :]
    s_ref[:, c_off:] = lax.dot_general(qq, kk, (((1,), (1,)), ((), ())),
                                       preferred_element_type=F32)


def _attn_softmax(s_ref, bias_ref, p_ref, l_ref, ntiles):
    _, c_off = _band_geometry(ntiles)
    rb = 32
    lane_rb = lax.broadcasted_iota(jnp.int32, (rb, LANES), 1)
    for hh in range(2):
        for i in range(GROUP // CHUNK):
            lo_col = max(i * CHUNK, c_off)
            hi_col = i * CHUNK + (LEFT_CHUNKS + 1) * CHUNK
            ws = (lo_col // LANES) * LANES
            we = -(-hi_col // LANES) * LANES
            nv = (we - ws) // LANES
            rows_all = slice(hh * GROUP + i * CHUNK, hh * GROUP + (i + 1) * CHUNK)
            if ws > c_off:
                p_ref[rows_all, c_off:ws] = jnp.zeros((CHUNK, ws - c_off), BF16)
            if we < BAND:
                p_ref[rows_all, we:BAND] = jnp.zeros((CHUNK, BAND - we), BF16)
            for b in range(CHUNK // rb):
                rows = slice(hh * GROUP + i * CHUNK + b * rb, hh * GROUP + i * CHUNK + (b + 1) * rb)
                brows = slice(i * CHUNK + b * rb, i * CHUNK + (b + 1) * rb)
                x = s_ref[rows, ws:we] + bias_ref[hh, brows, ws:we]
                cols = [x[:, c * LANES:(c + 1) * LANES] for c in range(nv)]
                if lo_col % LANES:
                    cols[0] = jnp.where(lane_rb >= HEAD_DIM, cols[0], NEG_INF)
                if hi_col % LANES:
                    cols[-1] = jnp.where(lane_rb < HEAD_DIM, cols[-1], NEG_INF)
                x = jnp.concatenate(cols, axis=1)
                m = jnp.max(x, axis=-1, keepdims=True)
                e = jnp.exp(x - m)
                l = jnp.sum(e, axis=-1, keepdims=True)
                p_ref[rows, ws:we] = e.astype(BF16)
                l_ref[rows, :] = jnp.broadcast_to(l, (rb, LANES))

def _attn_values(p_ref, l_ref, v_ref, o_ref, r0, ntiles):
    kw, c_off = _band_geometry(ntiles)
    lane = lax.broadcasted_iota(jnp.int32, (GROUP, HEAD_PAIR), 1)
    vv = v_ref[0, pl.ds(r0 + GROUP - kw, kw), :]
    o2 = jnp.dot(p_ref[:, c_off:], vv, preferred_element_type=F32)
    o2 = o2 / l_ref[...]
    o = jnp.where(lane < HEAD_DIM, o2[:GROUP], o2[GROUP:])
    o_ref[0, pl.ds(r0, GROUP), :] = o.astype(BF16)


def _attn_kernel(q_ref, k_ref, v_ref, w_ref, o_ref, bias_ref, s0, s1, p0, p1, l0, l1):
    n_groups = q_ref.shape[1] // GROUP
    full = BAND_TILES
    row0 = lambda g: pl.multiple_of(g * GROUP, GROUP)

    @pl.when(pl.program_id(1) == 0)
    def _():
        for hh in range(2):
            diag = jnp.broadcast_to(w_ref[hh], (GROUP, DIAG_LANES))
            rolled = pltpu.roll(diag, DIAG_LANES - (GROUP - 1), 1, stride=1, stride_axis=0)
            bias_ref[hh] = rolled[:, :BAND]

    def scores(g, s_ref, ntiles=full):
        _attn_scores(q_ref, k_ref, s_ref, row0(g), ntiles)

    def softmax(s_ref, p_ref, l_ref, ntiles=full):
        _attn_softmax(s_ref, bias_ref, p_ref, l_ref, ntiles)

    def values(g, p_ref, l_ref, ntiles=full):
        _attn_values(p_ref, l_ref, v_ref, o_ref, row0(g), ntiles)

    bufs = ((s0, p0, l0), (s1, p1, l1))
    ntiles = lambda g: min(g + 1, full)
    for t in range(n_groups + 2):
        if t >= 2:
            _, p_ref, l_ref = bufs[t % 2]
            values(t - 2, p_ref, l_ref, ntiles(t - 2))
        if 1 <= t <= n_groups:
            softmax(*bufs[(t - 1) % 2], ntiles(t - 1))
        if t < n_groups:
            scores(t, bufs[t % 2][0], ntiles(t))


def _attention(qkv, diagonals, batch, seq):
    t = batch * seq
    return pl.pallas_call(
        _attn_kernel,
        grid=(N_PAIRS, batch),
        in_specs=[
            pl.BlockSpec((1, seq, HEAD_PAIR), lambda hp, b: (hp, b, 0)),
            pl.BlockSpec((1, seq, HEAD_PAIR), lambda hp, b: (N_PAIRS + hp, b, 0)),
            pl.BlockSpec((1, seq, HEAD_PAIR), lambda hp, b: (2 * N_PAIRS + hp, b, 0)),
            pl.BlockSpec((2, 1, DIAG_LANES), lambda hp, b: (hp, 0, 0)),
        ],
        out_specs=pl.BlockSpec((1, seq, HEAD_PAIR), lambda hp, b: (hp, b, 0)),
        out_shape=jax.ShapeDtypeStruct((N_PAIRS, t, HEAD_PAIR), BF16),
        scratch_shapes=[
            pltpu.VMEM((2, GROUP, BAND), F32),
            pltpu.VMEM((2 * GROUP, BAND), F32),
            pltpu.VMEM((2 * GROUP, BAND), F32),
            pltpu.VMEM((2 * GROUP, BAND), BF16),
            pltpu.VMEM((2 * GROUP, BAND), BF16),
            pltpu.VMEM((2 * GROUP, LANES), F32),
            pltpu.VMEM((2 * GROUP, LANES), F32),
        ],
        compiler_params=pltpu.CompilerParams(
            dimension_semantics=("parallel", "arbitrary"), vmem_limit_bytes=VMEM_LIMIT),
        name="band_attention",
    )(qkv, qkv, qkv, diagonals)


def _band_bias_diagonals(rel_table):
    j = jnp.arange(DIAG_LANES)
    dist = (GROUP - 1 - j) + (BAND - GROUP)
    w = rel_table[:, jnp.clip(dist, -MAX_REL, MAX_REL) + MAX_REL]
    return w[:, None, :]


def _gelu_tanh(x):
    k = math.sqrt(2.0 / math.pi)
    inner = x * (k + (k * 0.044715) * (x * x))
    return x * (0.5 + 0.5 * jnp.tanh(inner))


def _rglru_tile(xr, gr_ref, side_work, cw_ref, cb_ref, wr_ref, br_ref, wi_ref, bi_ref, lam_ref,
                o_ref, xs_ref, xc_ref, rp_ref, ip_ref, a_ref, b_ref, hl_ref, pp_ref, h_ref):
    tm = xr.shape[0]
    seg = tm // SUBLANES
    n_lane_tiles = D_MODEL // LANES
    tiles_per_slab = MXU_DIM // LANES
    lane_tile = lambda c: slice(c * LANES, (c + 1) * LANES)

    for c in range(n_lane_tiles):
        xs_ref[c, SUBLANES:, :] = xr[:, lane_tile(c)]

    for d in range(D_MODEL // MXU_DIM):
        for k in range(tiles_per_slab):
            c = d * tiles_per_slab + k
            for j in range(SUBLANES):
                acc = cb_ref[:, lane_tile(c)]
                for tap in range(CONV_W):
                    off = SUBLANES - (CONV_W - 1) + tap + j * seg
                    acc = acc + xs_ref[c, off:off + seg, :] * cw_ref[tap:tap + 1, lane_tile(c)]
                xc_ref[j * seg:(j + 1) * seg, lane_tile(k)] = acc
                side_work(COST_CONV)
            xs_ref[c, 0:SUBLANES, :] = xs_ref[c, tm:tm + SUBLANES, :]
        xcb = xc_ref[...].astype(BF16)
        rp_ref[...] = jnp.dot(xcb, wr_ref[d], preferred_element_type=F32)
        ip_ref[...] = jnp.dot(xcb, wi_ref[d], preferred_element_type=F32)
        for k in range(tiles_per_slab):
            c = d * tiles_per_slab + k
            nlam = -lam_ref[:, lane_tile(c)]
            softplus = jnp.maximum(nlam, 0.0) + jnp.log1p(jnp.exp(-jnp.abs(nlam)))
            log_a_scale = (-RG_C) * softplus
            for j in range(SUBLANES):
                rows = slice(j * seg, (j + 1) * seg)
                xc = xc_ref[rows, lane_tile(k)]
                r = jax.nn.sigmoid(rp_ref[rows, lane_tile(k)] + br_ref[:, lane_tile(c)])
                ig = jax.nn.sigmoid(ip_ref[rows, lane_tile(k)] + bi_ref[:, lane_tile(c)])
                log_a = r * log_a_scale
                a = jnp.exp(log_a)
                y = jnp.tanh(log_a) * (-1.0 - a * a)
                mult = jnp.where(y > 0.0, y * lax.rsqrt(y), 0.0)
                srows = slice(j * SEG_PITCH, j * SEG_PITCH + seg)
                a_ref[c, srows, :] = a
                b_ref[c, srows, :] = mult * (ig * xc)
                side_work(COST_GATE)

    hs = [jnp.zeros((SUBLANES, LANES), F32) for _ in range(n_lane_tiles)]
    ps = [jnp.ones((SUBLANES, LANES), F32) for _ in range(n_lane_tiles)]
    for t in range(seg):
        idx = pl.ds(t, SUBLANES, stride=SEG_PITCH)
        for c in range(n_lane_tiles):
            a_t = a_ref[c, idx, :]
            hs[c] = a_t * hs[c] + b_ref[c, idx, :]
            ps[c] = a_t * ps[c]
            hl_ref[c, idx, :] = hs[c]
            pp_ref[c, idx, :] = ps[c]
        side_work(COST_SCAN_STEP)
    h_end = jnp.concatenate(hs, axis=1)
    p_end = jnp.concatenate(ps, axis=1)

    h_in = h_ref[0:1, :]
    for j in range(SUBLANES):
        rows = slice(j * seg, (j + 1) * seg)
        srows = slice(j * SEG_PITCH, j * SEG_PITCH + seg)
        for c in range(n_lane_tiles):
            hj = hl_ref[c, srows, :] + pp_ref[c, srows, :] * h_in[:, lane_tile(c)]
            o_ref[rows, lane_tile(c)] = (hj * _gelu_tanh(gr_ref[rows, lane_tile(c)])).astype(BF16)
            side_work(COST_OUT)
        h_in = p_end[j:j + 1, :] * h_in + h_end[j:j + 1, :]
    h_ref[0:1, :] = h_in


def _block_diag_tiles(w):
    per = MXU_DIM // RG_BLOCK
    n_diag = D_MODEL // MXU_DIM
    tiled = jnp.tile(w.reshape(n_diag, MXU_DIM, RG_BLOCK), (1, 1, per))
    row_block = lax.broadcasted_iota(jnp.int32, (MXU_DIM, MXU_DIM), 0) // RG_BLOCK
    col_block = lax.broadcasted_iota(jnp.int32, (MXU_DIM, MXU_DIM), 1) // RG_BLOCK
    return jnp.where(row_block == col_block, tiled, 0.0)


def _mixffn_kernel(x_ref, at_ref, hb_ref, ga_ref, gb_ref, bm_ref, wa_ref, wb_ref, wo_ref,
                   gn_ref, win_ref, wout_ref, gf_ref, o_ref, *, splits):
    attn = jnp.concatenate([at_ref[hp] for hp in range(N_PAIRS)], axis=1)
    ya = jnp.dot(attn, wa_ref[...], preferred_element_type=F32)
    yb = jnp.dot(hb_ref[...], wb_ref[...], preferred_element_type=F32)
    g_a = jax.nn.sigmoid(ga_ref[...].astype(F32) + bm_ref[:, :D_MODEL])
    g_b = jax.nn.sigmoid(gb_ref[...].astype(F32) + bm_ref[:, D_MODEL:])
    mix = (g_a * ya + g_b * yb).astype(BF16)
    h = x_ref[...] + jnp.dot(mix, wo_ref[...], preferred_element_type=F32)
    hn = _rms(h, gn_ref[...]).astype(BF16)
    d_ff = wout_ref.shape[0]
    acc = h
    for c0, c1 in splits:
        g = jnp.dot(hn, win_ref[:, c0:c1], preferred_element_type=F32)
        up = jnp.dot(hn, win_ref[:, d_ff + c0:d_ff + c1], preferred_element_type=F32)
        act = (jax.nn.silu(g) * up).astype(BF16)
        acc = acc + jnp.dot(act, wout_ref[c0:c1, :], preferred_element_type=F32)
    o_ref[...] = _rms(acc, gf_ref[...])


def _mixffn(x2, attn, hb, gates, bm, wa, wb, wo, gn, win, wout, gf, tm):
    t = x2.shape[0]
    d_ff = wout.shape[0]
    edges = list(range(0, d_ff, FFN_CHUNK)) + [d_ff]
    splits = tuple(zip(edges[:-1], edges[1:]))
    row = lambda i: (i, 0)
    return pl.pallas_call(
        functools.partial(_mixffn_kernel, splits=splits),
        grid=(t // tm,),
        in_specs=[
            pl.BlockSpec((tm, D_MODEL), row),
            pl.BlockSpec((N_PAIRS, tm, HEAD_PAIR), lambda i: (0, i, 0)),
            pl.BlockSpec((tm, D_MODEL), row),
            pl.BlockSpec((tm, D_MODEL), lambda i: (i, 0)),
            pl.BlockSpec((tm, D_MODEL), lambda i: (i, 1)),
            _resident((1, 2 * D_MODEL)),
            _resident((D_MODEL, D_MODEL)),
            _resident((D_MODEL, D_MODEL)),
            _resident((D_MODEL, D_MODEL)),
            _resident((1, D_MODEL)),
            _resident((D_MODEL, 2 * d_ff)),
            _resident((d_ff, D_MODEL)),
            _resident((1, D_MODEL)),
        ],
        out_specs=pl.BlockSpec((tm, D_MODEL), row),
        out_shape=jax.ShapeDtypeStruct((t, D_MODEL), F32),
        compiler_params=pltpu.CompilerParams(
            dimension_semantics=("parallel",), vmem_limit_bytes=VMEM_LIMIT),
        name="mix_ffn",
    )(x2, attn, hb, gates, gates, bm, wa, wb, wo, gn, win, wout, gf)


def kernel(x, norm_mix_g, w_in, b_merge, rel_table, w_attn_out, conv_w, conv_b, w_rg_r, b_rg_r,
           w_rg_i, b_rg_i, rg_lambda, w_rnn_out, w_o, norm_ffn_g, w_ffn_in, w_ffn_out, final_norm_g):
    batch, seq, d = x.shape
    assert d == D_MODEL and seq % GROUP == 0 and seq // GROUP >= BAND_TILES
    assert w_in.shape[0] == 1, "single-layer block"
    x2 = x.reshape(batch * seq, d)
    row = lambda v: v.reshape(1, -1)

    assert seq % RG_TILE == 0
    later_weights = (w_attn_out[0], w_rnn_out[0], w_o[0], w_ffn_in[0], w_ffn_out[0])
    qkv, gates, hb, (wa, wb, wo, wfi, wfo) = _inproj(
        x2, row(norm_mix_g[0]), w_in[0], conv_w[0], row(conv_b[0]),
        _block_diag_tiles(w_rg_r[0]).astype(BF16), row(b_rg_r[0]),
        _block_diag_tiles(w_rg_i[0]).astype(BF16), row(b_rg_i[0]), row(rg_lambda[0]),
        later_weights, seq)
    attn = _attention(qkv, _band_bias_diagonals(rel_table[0]), batch, seq)
    out = _mixffn(x2, attn, hb, gates, row(b_merge[0]), wa, wb, wo, row(norm_ffn_g[0]), wfi, wfo,
                  row(final_norm_g), tm=512)
    return out.reshape(batch, seq, d)
```

```python
import functools
import math

import jax
import jax.numpy as jnp
from jax import lax
from jax.experimental import pallas as pl
from jax.experimental.pallas import tpu as pltpu

F32 = jnp.float32
BF16 = jnp.bfloat16

D_MODEL = 1024
CHUNK = 64
LEFT_CHUNKS = 8
N_HEADS = 16
HEAD_DIM = 64
MAX_REL = 128
N_RG_BLOCKS = 16
RG_BLOCK = D_MODEL // N_RG_BLOCKS
CONV_W = 4
RG_C = 8.0
EPS = 1e-6
NEG_INF = -1e30

LANES = 128
SUBLANES = 8
MXU_DIM = 256
HEAD_PAIR = 2 * HEAD_DIM
N_PAIRS = N_HEADS // 2
GROUP = 4 * CHUNK
BAND_TILES = LEFT_CHUNKS * CHUNK // GROUP + 1
BAND = BAND_TILES * GROUP
DIAG_LANES = -(-(GROUP + BAND - 1) // LANES) * LANES
VMEM_LIMIT = 56 * 1024 * 1024
RG_TILE = 512
W_STAGE_ROWS = 64
FFN_CHUNK = 3 * MXU_DIM
SEG_PITCH = RG_TILE // SUBLANES + SUBLANES // 2
COST_CONV, COST_GATE, COST_SCAN_STEP, COST_OUT = 16, 52, 23, 24


def _rms(x, g):
    var = jnp.mean(x * x, axis=-1, keepdims=True)
    return x * lax.rsqrt(var + EPS) * g


def _resident(shape):
    zeros = (0,) * len(shape)
    return pl.BlockSpec(shape, lambda *_: zeros, pipeline_mode=pl.Buffered(1))


def _load_rounded(src_hbm, dst_ref, stage_ref, sem):
    rows = stage_ref.shape[1]
    n_chunks = src_hbm.shape[0] // rows

    def chunk_copy(c):
        slot = c % 2
        return pltpu.make_async_copy(src_hbm.at[pl.ds(c * rows, rows), :], stage_ref.at[slot],
                                     sem.at[slot])

    chunk_copy(0).start()
    for c in range(n_chunks):
        if c + 1 < n_chunks:
            chunk_copy(c + 1).start()
        chunk_copy(c).wait()
        dst_ref[c * rows:(c + 1) * rows, :] = stage_ref[c % 2].astype(BF16)


def _inproj_kernel(*refs, tiles_per_seq, n_cast):
    (x_ref, g_ref, w_hbm, cw_ref, cb_ref, wr_ref, br_ref, wi_ref, bi_ref, lam_ref), refs = refs[:10], refs[10:]
    cast_in, refs = refs[:n_cast], refs[n_cast:]
    (qkv_ref, gates_ref, hb_ref), refs = refs[:3], refs[3:]
    cast_out, refs = refs[:n_cast], refs[n_cast:]
    (w_ref, stage_ref, stage_sem, xn_ref, gr_ref, xs_ref, xc_ref, rp_ref, ip_ref,
     a_ref, b_ref, hl_ref, pp_ref, h_ref) = refs

    @pl.when(pl.program_id(0) == 0)
    def _():
        _load_rounded(w_hbm, w_ref, stage_ref, stage_sem)

    for src_ref, dst_ref in zip(cast_in, cast_out):
        dst_ref[...] = src_ref[...].astype(BF16)

    @pl.when(pl.program_id(0) % tiles_per_seq == 0)
    def _():
        xs_ref[:, 0:SUBLANES, :] = jnp.zeros((D_MODEL // LANES, SUBLANES, LANES), F32)
        h_ref[...] = jnp.zeros_like(h_ref)

    xn_ref[...] = _rms(x_ref[...], g_ref[...]).astype(BF16)

    def u_cols(c0, width):
        return jnp.dot(xn_ref[...], w_ref[:, c0:c0 + width], preferred_element_type=F32)

    def unit(j, n):
        acc = u_cols(j * D_MODEL + n * MXU_DIM, MXU_DIM)
        if j < 3:
            for k in range(MXU_DIM // HEAD_PAIR):
                hp = n * (MXU_DIM // HEAD_PAIR) + k
                qkv_ref[j * N_PAIRS + hp] = acc[:, k * HEAD_PAIR:(k + 1) * HEAD_PAIR].astype(BF16)
        elif j == 4:
            gr_ref[:, n * MXU_DIM:(n + 1) * MXU_DIM] = acc
        else:
            c0 = (j - 5) * D_MODEL + n * MXU_DIM
            gates_ref[:, c0:c0 + MXU_DIM] = acc.astype(BF16)

    units = [(j, n) for j in (4, 0, 1, 2, 5, 6) for n in range(D_MODEL // MXU_DIM)]
    n_pieces = SUBLANES * (D_MODEL // LANES)
    total_cost = n_pieces * (COST_CONV + COST_GATE + COST_OUT) + (x_ref.shape[0] // SUBLANES) * COST_SCAN_STEP
    cost_per_unit = total_cost / len(units)
    spent = [0.0]

    def side_work(cost):
        spent[0] += cost
        while units and spent[0] >= cost_per_unit:
            spent[0] -= cost_per_unit
            unit(*units.pop(0))

    _rglru_tile(u_cols(3 * D_MODEL, D_MODEL), gr_ref, side_work,
                cw_ref, cb_ref, wr_ref, br_ref, wi_ref, bi_ref, lam_ref,
                hb_ref, xs_ref, xc_ref, rp_ref, ip_ref, a_ref, b_ref, hl_ref, pp_ref, h_ref)
    while units:
        unit(*units.pop(0))


BF16_SUBLANES = 2 * SUBLANES


def _cast_slice_spec(rows, cols, n_steps):
    repeat = next(k for k in (1, 2, 4, 8)
                  if rows * k % n_steps == 0 and (rows * k // n_steps) % BF16_SUBLANES == 0)
    return pl.BlockSpec((rows * repeat // n_steps, cols), lambda i: (i // repeat, 0))


def _inproj(x2, g, w, cw, cb, wr, br, wi, bi, lam, later_weights, seq):
    t = x2.shape[0]
    tm = RG_TILE
    n_steps = t // tm
    n_diag = D_MODEL // MXU_DIM
    scan_scratch = pltpu.VMEM((D_MODEL // LANES, SUBLANES * SEG_PITCH, LANES), F32)
    slab_scratch = pltpu.VMEM((tm, MXU_DIM), F32)
    cast_specs = [_cast_slice_spec(*wt.shape, n_steps) for wt in later_weights]
    outs = pl.pallas_call(
        functools.partial(_inproj_kernel, tiles_per_seq=seq // tm, n_cast=len(later_weights)),
        grid=(n_steps,),
        in_specs=[
            pl.BlockSpec((tm, D_MODEL), lambda i: (i, 0)),
            _resident((1, D_MODEL)),
            pl.BlockSpec(memory_space=pl.ANY),
            _resident((CONV_W, D_MODEL)),
            _resident((1, D_MODEL)),
            _resident((n_diag, MXU_DIM, MXU_DIM)),
            _resident((1, D_MODEL)),
            _resident((n_diag, MXU_DIM, MXU_DIM)),
            _resident((1, D_MODEL)),
            _resident((1, D_MODEL)),
            *cast_specs,
        ],
        out_specs=[
            pl.BlockSpec((3 * N_PAIRS, tm, HEAD_PAIR), lambda i: (0, i, 0)),
            pl.BlockSpec((tm, 2 * D_MODEL), lambda i: (i, 0)),
            pl.BlockSpec((tm, D_MODEL), lambda i: (i, 0)),
            *cast_specs,
        ],
        out_shape=[
            jax.ShapeDtypeStruct((3 * N_PAIRS, t, HEAD_PAIR), BF16),
            jax.ShapeDtypeStruct((t, 2 * D_MODEL), BF16),
            jax.ShapeDtypeStruct((t, D_MODEL), BF16),
            *[jax.ShapeDtypeStruct(wt.shape, BF16) for wt in later_weights],
        ],
        scratch_shapes=[
            pltpu.VMEM(w.shape, BF16),
            pltpu.VMEM((2, W_STAGE_ROWS, w.shape[1]), F32),
            pltpu.SemaphoreType.DMA((2,)),
            pltpu.VMEM((tm, D_MODEL), BF16),
            pltpu.VMEM((tm, D_MODEL), F32),
            pltpu.VMEM((D_MODEL // LANES, tm + SUBLANES, LANES), F32),
            slab_scratch, slab_scratch, slab_scratch,
            scan_scratch, scan_scratch, scan_scratch, scan_scratch,
            pltpu.VMEM((SUBLANES, D_MODEL), F32),
        ],
        compiler_params=pltpu.CompilerParams(
            dimension_semantics=("arbitrary",), vmem_limit_bytes=VMEM_LIMIT),
        name="inproj_rglru",
    )(x2, g, w, cw, cb, wr, br, wi, bi, lam, *later_weights)
    return outs[0], outs[1], outs[2], outs[3:]


def _band_geometry(ntiles):
    kw = ntiles * GROUP
    return kw, BAND - kw


def _attn_scores(q_ref, k_ref, s_ref, r0, ntiles):
    kw, c_off = _band_geometry(ntiles)
    lane = lax.broadcasted_iota(jnp.int32, (GROUP, HEAD_PAIR), 1)
    first_head = lane < HEAD_DIM
    qp = q_ref[0, pl.ds(r0, GROUP), :] * jnp.asarray(1.0 / math.sqrt(HEAD_DIM), BF16)
    zero = jnp.zeros_like(qp)
    qq = jnp.concatenate([jnp.where(first_head, qp, zero),
                          jnp.where(first_head, zero, qp)], axis=0)
    kk = k_ref[0, pl.ds(r0 + GROUP - kw, kw), :]
    s_ref[:, c_off:] = lax.dot_general(qq, kk, (((1,), (1,)), ((), ())),
                                       preferred_element_type=F32)


def _attn_softmax(s_ref, bias_ref, p_ref, l_ref, ntiles):
    _, c_off = _band_geometry(ntiles)
    rb = 32
    lane_rb = lax.broadcasted_iota(jnp.int32, (rb, LANES), 1)
    for hh in range(2):
        for i in range(GROUP // CHUNK):
            lo_col = max(i * CHUNK, c_off)
            hi_col = i * CHUNK + (LEFT_CHUNKS + 1) * CHUNK
            ws = (lo_col // LANES) * LANES
            we = -(-hi_col // LANES) * LANES
            nv = (we - ws) // LANES
            rows_all = slice(hh * GROUP + i * CHUNK, hh * GROUP + (i + 1) * CHUNK)
            if ws > c_off:
                p_ref[rows_all, c_off:ws] = jnp.zeros((CHUNK, ws - c_off), BF16)
            if we < BAND:
                p_ref[rows_all, we:BAND] = jnp.zeros((CHUNK, BAND - we), BF16)
            for b in range(CHUNK // rb):
                rows = slice(hh * GROUP + i * CHUNK + b * rb, hh * GROUP + i * CHUNK + (b + 1) * rb)
                brows = slice(i * CHUNK + b * rb, i * CHUNK + (b + 1) * rb)
                first_biased = brows.start + (BAND - GROUP) - MAX_REL + 1
                cols = []
                for c in range(nv):
                    tile = slice(ws + c * LANES, ws + (c + 1) * LANES)
                    col = s_ref[rows, tile]
                    if tile.stop > first_biased:
                        col = col + bias_ref[hh, brows, tile]
                    cols.append(col)
                if lo_col % LANES:
                    cols[0] = jnp.where(lane_rb >= HEAD_DIM, cols[0], NEG_INF)
                if hi_col % LANES:
                    cols[-1] = jnp.where(lane_rb < HEAD_DIM, cols[-1], NEG_INF)
                x = jnp.concatenate(cols, axis=1)
                m = jnp.max(x, axis=-1, keepdims=True)
                e = jnp.exp(x - m)
                l = jnp.sum(e, axis=-1, keepdims=True)
                p_ref[rows, ws:we] = e.astype(BF16)
                l_ref[rows, :] = jnp.broadcast_to(l, (rb, LANES))

def _attn_values(p_ref, l_ref, v_ref, o_ref, r0, ntiles):
    kw, c_off = _band_geometry(ntiles)
    lane = lax.broadcasted_iota(jnp.int32, (GROUP, HEAD_PAIR), 1)
    vv = v_ref[0, pl.ds(r0 + GROUP - kw, kw), :]
    o2 = jnp.dot(p_ref[:, c_off:], vv, preferred_element_type=F32)
    o2 = o2 / l_ref[...]
    o = jnp.where(lane < HEAD_DIM, o2[:GROUP], o2[GROUP:])
    o_ref[0, pl.ds(r0, GROUP), :] = o.astype(BF16)


def _attn_kernel(q_ref, k_ref, v_ref, w_ref, o_ref, bias_ref, s0, s1, p0, p1, l0, l1):
    n_groups = q_ref.shape[1] // GROUP
    full = BAND_TILES
    row0 = lambda g: pl.multiple_of(g * GROUP, GROUP)

    @pl.when(pl.program_id(1) == 0)
    def _():
        for hh in range(2):
            diag = jnp.broadcast_to(w_ref[hh], (GROUP, DIAG_LANES))
            rolled = pltpu.roll(diag, DIAG_LANES - (GROUP - 1), 1, stride=1, stride_axis=0)
            bias_ref[hh] = rolled[:, :BAND]

    def scores(g, s_ref, ntiles=full):
        _attn_scores(q_ref, k_ref, s_ref, row0(g), ntiles)

    def softmax(s_ref, p_ref, l_ref, ntiles=full):
        _attn_softmax(s_ref, bias_ref, p_ref, l_ref, ntiles)

    def values(g, p_ref, l_ref, ntiles=full):
        _attn_values(p_ref, l_ref, v_ref, o_ref, row0(g), ntiles)

    bufs = ((s0, p0, l0), (s1, p1, l1))
    ntiles = lambda g: min(g + 1, full)
    for t in range(n_groups + 2):
        if t >= 2:
            _, p_ref, l_ref = bufs[t % 2]
            values(t - 2, p_ref, l_ref, ntiles(t - 2))
        if 1 <= t <= n_groups:
            softmax(*bufs[(t - 1) % 2], ntiles(t - 1))
        if t < n_groups:
            scores(t, bufs[t % 2][0], ntiles(t))


def _attention(qkv, diagonals, batch, seq):
    t = batch * seq
    return pl.pallas_call(
        _attn_kernel,
        grid=(N_PAIRS, batch),
        in_specs=[
            pl.BlockSpec((1, seq, HEAD_PAIR), lambda hp, b: (hp, b, 0)),
            pl.BlockSpec((1, seq, HEAD_PAIR), lambda hp, b: (N_PAIRS + hp, b, 0)),
            pl.BlockSpec((1, seq, HEAD_PAIR), lambda hp, b: (2 * N_PAIRS + hp, b, 0)),
            pl.BlockSpec((2, 1, DIAG_LANES), lambda hp, b: (hp, 0, 0)),
        ],
        out_specs=pl.BlockSpec((1, seq, HEAD_PAIR), lambda hp, b: (hp, b, 0)),
        out_shape=jax.ShapeDtypeStruct((N_PAIRS, t, HEAD_PAIR), BF16),
        scratch_shapes=[
            pltpu.VMEM((2, GROUP, BAND), F32),
            pltpu.VMEM((2 * GROUP, BAND), F32),
            pltpu.VMEM((2 * GROUP, BAND), F32),
            pltpu.VMEM((2 * GROUP, BAND), BF16),
            pltpu.VMEM((2 * GROUP, BAND), BF16),
            pltpu.VMEM((2 * GROUP, LANES), F32),
            pltpu.VMEM((2 * GROUP, LANES), F32),
        ],
        compiler_params=pltpu.CompilerParams(
            dimension_semantics=("parallel", "arbitrary"), vmem_limit_bytes=VMEM_LIMIT),
        name="band_attention",
    )(qkv, qkv, qkv, diagonals)


def _band_bias_diagonals(rel_table):
    j = jnp.arange(DIAG_LANES)
    dist = (GROUP - 1 - j) + (BAND - GROUP)
    w = rel_table[:, jnp.clip(dist, -MAX_REL, MAX_REL) + MAX_REL] - rel_table[:, 2 * MAX_REL:]
    return w[:, None, :]


def _gelu_tanh(x):
    k = math.sqrt(2.0 / math.pi)
    inner = x * (k + (k * 0.044715) * (x * x))
    return x * (0.5 + 0.5 * jnp.tanh(inner))


def _rglru_tile(xr, gr_ref, side_work, cw_ref, cb_ref, wr_ref, br_ref, wi_ref, bi_ref, lam_ref,
                o_ref, xs_ref, xc_ref, rp_ref, ip_ref, a_ref, b_ref, hl_ref, pp_ref, h_ref):
    tm = xr.shape[0]
    seg = tm // SUBLANES
    n_lane_tiles = D_MODEL // LANES
    tiles_per_slab = MXU_DIM // LANES
    lane_tile = lambda c: slice(c * LANES, (c + 1) * LANES)

    for c in range(n_lane_tiles):
        xs_ref[c, SUBLANES:, :] = xr[:, lane_tile(c)]

    for d in range(D_MODEL // MXU_DIM):
        for k in range(tiles_per_slab):
            c = d * tiles_per_slab + k
            for j in range(SUBLANES):
                acc = cb_ref[:, lane_tile(c)]
                for tap in range(CONV_W):
                    off = SUBLANES - (CONV_W - 1) + tap + j * seg
                    acc = acc + xs_ref[c, off:off + seg, :] * cw_ref[tap:tap + 1, lane_tile(c)]
                xc_ref[j * seg:(j + 1) * seg, lane_tile(k)] = acc
                side_work(COST_CONV)
            xs_ref[c, 0:SUBLANES, :] = xs_ref[c, tm:tm + SUBLANES, :]
        xcb = xc_ref[...].astype(BF16)
        rp_ref[...] = jnp.dot(xcb, wr_ref[d], preferred_element_type=F32)
        ip_ref[...] = jnp.dot(xcb, wi_ref[d], preferred_element_type=F32)
        for k in range(tiles_per_slab):
            c = d * tiles_per_slab + k
            nlam = -lam_ref[:, lane_tile(c)]
            softplus = jnp.maximum(nlam, 0.0) + jnp.log1p(jnp.exp(-jnp.abs(nlam)))
            log_a_scale = (-RG_C) * softplus
            for j in range(SUBLANES):
                rows = slice(j * seg, (j + 1) * seg)
                xc = xc_ref[rows, lane_tile(k)]
                r = jax.nn.sigmoid(rp_ref[rows, lane_tile(k)] + br_ref[:, lane_tile(c)])
                ig = jax.nn.sigmoid(ip_ref[rows, lane_tile(k)] + bi_ref[:, lane_tile(c)])
                log_a = r * log_a_scale
                a = jnp.exp(log_a)
                y = jnp.tanh(log_a) * (-1.0 - a * a)
                mult = jnp.where(y > 0.0, y * lax.rsqrt(y), 0.0)
                srows = slice(j * SEG_PITCH, j * SEG_PITCH + seg)
                a_ref[c, srows, :] = a
                b_ref[c, srows, :] = mult * (ig * xc)
                side_work(COST_GATE)

    hs = [jnp.zeros((SUBLANES, LANES), F32) for _ in range(n_lane_tiles)]
    ps = [jnp.ones((SUBLANES, LANES), F32) for _ in range(n_lane_tiles)]
    for t in range(seg):
        idx = pl.ds(t, SUBLANES, stride=SEG_PITCH)
        for c in range(n_lane_tiles):
            a_t = a_ref[c, idx, :]
            hs[c] = a_t * hs[c] + b_ref[c, idx, :]
            ps[c] = a_t * ps[c]
            hl_ref[c, idx, :] = hs[c]
            pp_ref[c, idx, :] = ps[c]
        side_work(COST_SCAN_STEP)
    h_end = jnp.concatenate(hs, axis=1)
    p_end = jnp.concatenate(ps, axis=1)

    h_in = h_ref[0:1, :]
    for j in range(SUBLANES):
        rows = slice(j * seg, (j + 1) * seg)
        srows = slice(j * SEG_PITCH, j * SEG_PITCH + seg)
        for c in range(n_lane_tiles):
            hj = hl_ref[c, srows, :] + pp_ref[c, srows, :] * h_in[:, lane_tile(c)]
            o_ref[rows, lane_tile(c)] = (hj * _gelu_tanh(gr_ref[rows, lane_tile(c)])).astype(BF16)
            side_work(COST_OUT)
        h_in = p_end[j:j + 1, :] * h_in + h_end[j:j + 1, :]
    h_ref[0:1, :] = h_in


def _block_diag_tiles(w):
    per = MXU_DIM // RG_BLOCK
    n_diag = D_MODEL // MXU_DIM
    w4 = w.reshape(n_diag, per, RG_BLOCK, RG_BLOCK)
    eye = jnp.eye(per, dtype=w.dtype)
    tiles = jnp.einsum('dpij,pq->dpiqj', w4, eye)
    return tiles.reshape(n_diag, MXU_DIM, MXU_DIM)


def _mixffn_kernel(x_ref, at_ref, hb_ref, ga_ref, gb_ref, bm_ref, wa_ref, wb_ref, wo_ref,
                   gn_ref, win_ref, wout_ref, gf_ref, o_ref, *, splits):
    attn = jnp.concatenate([at_ref[hp] for hp in range(N_PAIRS)], axis=1)
    ya = jnp.dot(attn, wa_ref[...], preferred_element_type=F32)
    yb = jnp.dot(hb_ref[...], wb_ref[...], preferred_element_type=F32)
    g_a = jax.nn.sigmoid(ga_ref[...].astype(F32) + bm_ref[:, :D_MODEL])
    g_b = jax.nn.sigmoid(gb_ref[...].astype(F32) + bm_ref[:, D_MODEL:])
    mix = (g_a * ya + g_b * yb).astype(BF16)
    h = x_ref[...] + jnp.dot(mix, wo_ref[...], preferred_element_type=F32)
    hn = _rms(h, gn_ref[...]).astype(BF16)
    d_ff = wout_ref.shape[0]
    acc = h
    for c0, c1 in splits:
        g = jnp.dot(hn, win_ref[:, c0:c1], preferred_element_type=F32)
        up = jnp.dot(hn, win_ref[:, d_ff + c0:d_ff + c1], preferred_element_type=F32)
        act = (jax.nn.silu(g) * up).astype(BF16)
        acc = acc + jnp.dot(act, wout_ref[c0:c1, :], preferred_element_type=F32)
    o_ref[...] = _rms(acc, gf_ref[...])


def _mixffn(x2, attn, hb, gates, bm, wa, wb, wo, gn, win, wout, gf, tm):
    t = x2.shape[0]
    d_ff = wout.shape[0]
    edges = list(range(0, d_ff, FFN_CHUNK)) + [d_ff]
    splits = tuple(zip(edges[:-1], edges[1:]))
    row = lambda i: (i, 0)
    return pl.pallas_call(
        functools.partial(_mixffn_kernel, splits=splits),
        grid=(t // tm,),
        in_specs=[
            pl.BlockSpec((tm, D_MODEL), row),
            pl.BlockSpec((N_PAIRS, tm, HEAD_PAIR), lambda i: (0, i, 0)),
            pl.BlockSpec((tm, D_MODEL), row),
            pl.BlockSpec((tm, D_MODEL), lambda i: (i, 0)),
            pl.BlockSpec((tm, D_MODEL), lambda i: (i, 1)),
            _resident((1, 2 * D_MODEL)),
            _resident((D_MODEL, D_MODEL)),
            _resident((D_MODEL, D_MODEL)),
            _resident((D_MODEL, D_MODEL)),
            _resident((1, D_MODEL)),
            _resident((D_MODEL, 2 * d_ff)),
            _resident((d_ff, D_MODEL)),
            _resident((1, D_MODEL)),
        ],
        out_specs=pl.BlockSpec((tm, D_MODEL), row),
        out_shape=jax.ShapeDtypeStruct((t, D_MODEL), F32),
        compiler_params=pltpu.CompilerParams(
            dimension_semantics=("parallel",), vmem_limit_bytes=VMEM_LIMIT),
        name="mix_ffn",
    )(x2, attn, hb, gates, gates, bm, wa, wb, wo, gn, win, wout, gf)


def kernel(x, norm_mix_g, w_in, b_merge, rel_table, w_attn_out, conv_w, conv_b, w_rg_r, b_rg_r,
           w_rg_i, b_rg_i, rg_lambda, w_rnn_out, w_o, norm_ffn_g, w_ffn_in, w_ffn_out, final_norm_g):
    batch, seq, d = x.shape
    assert d == D_MODEL and seq % GROUP == 0 and seq // GROUP >= BAND_TILES
    assert w_in.shape[0] == 1, "single-layer block"
    x2 = x.reshape(batch * seq, d)
    row = lambda v: v.reshape(1, -1)

    assert seq % RG_TILE == 0
    later_weights = (w_attn_out[0], w_rnn_out[0], w_o[0], w_ffn_in[0], w_ffn_out[0])
    qkv, gates, hb, (wa, wb, wo, wfi, wfo) = _inproj(
        x2, row(norm_mix_g[0]), w_in[0], conv_w[0], row(conv_b[0]),
        _block_diag_tiles(w_rg_r[0]).astype(BF16), row(b_rg_r[0]),
        _block_diag_tiles(w_rg_i[0]).astype(BF16), row(b_rg_i[0]), row(rg_lambda[0]),
        later_weights, seq)
    attn = _attention(qkv, _band_bias_diagonals(rel_table[0]), batch, seq)
    out = _mixffn(x2, attn, hb, gates, row(b_merge[0]), wa, wb, wo, row(norm_ffn_g[0]), wfi, wfo,
                  row(final_norm_g), tm=512)
    return out.reshape(batch, seq, d)
```

```python
import functools
import math

import jax
import jax.numpy as jnp
from jax import lax
from jax.experimental import pallas as pl
from jax.experimental.pallas import tpu as pltpu

F32 = jnp.float32
BF16 = jnp.bfloat16

D_MODEL = 1024
CHUNK = 64
LEFT_CHUNKS = 8
N_HEADS = 16
HEAD_DIM = 64
MAX_REL = 128
N_RG_BLOCKS = 16
RG_BLOCK = D_MODEL // N_RG_BLOCKS
CONV_W = 4
RG_C = 8.0
EPS = 1e-6
NEG_INF = -1e30

LANES = 128
SUBLANES = 8
MXU_DIM = 256
HEAD_PAIR = 2 * HEAD_DIM
N_PAIRS = N_HEADS // 2
GROUP = 4 * CHUNK
BAND_TILES = LEFT_CHUNKS * CHUNK // GROUP + 1
BAND = BAND_TILES * GROUP
DIAG_LANES = -(-(GROUP + BAND - 1) // LANES) * LANES
VMEM_LIMIT = 56 * 1024 * 1024
RG_TILE = 512
W_STAGE_ROWS = 64
FFN_CHUNK = 3 * MXU_DIM
UNIT_COLS = 2 * MXU_DIM
SEG_PITCH = RG_TILE // SUBLANES + SUBLANES // 2
COST_CONV, COST_GATE, COST_SCAN_STEP, COST_OUT = 16, 52, 23, 24


def _rms(x, g):
    var = jnp.mean(x * x, axis=-1, keepdims=True)
    return x * lax.rsqrt(var + EPS) * g


def _resident(shape):
    zeros = (0,) * len(shape)
    return pl.BlockSpec(shape, lambda *_: zeros, pipeline_mode=pl.Buffered(1))


def _load_rounded(src_hbm, dst_ref, stage_ref, sem):
    rows = stage_ref.shape[1]
    n_chunks = src_hbm.shape[0] // rows

    def chunk_copy(c):
        slot = c % 2
        return pltpu.make_async_copy(src_hbm.at[pl.ds(c * rows, rows), :], stage_ref.at[slot],
                                     sem.at[slot])

    chunk_copy(0).start()
    for c in range(n_chunks):
        if c + 1 < n_chunks:
            chunk_copy(c + 1).start()
        chunk_copy(c).wait()
        dst_ref[c * rows:(c + 1) * rows, :] = stage_ref[c % 2].astype(BF16)


def _inproj_kernel(*refs, tiles_per_seq, n_cast):
    (x_ref, g_ref, w_hbm, cw_ref, cb_ref, wr_ref, br_ref, wi_ref, bi_ref, lam_ref), refs = refs[:10], refs[10:]
    cast_in, refs = refs[:n_cast], refs[n_cast:]
    (qkv_ref, gates_ref, hb_ref), refs = refs[:3], refs[3:]
    cast_out, refs = refs[:n_cast], refs[n_cast:]
    (w_ref, stage_ref, stage_sem, xn_ref, gr_ref, xs_ref, xc_ref, rp_ref, ip_ref,
     a_ref, b_ref, hl_ref, pp_ref, h_ref) = refs

    @pl.when(pl.program_id(0) == 0)
    def _():
        _load_rounded(w_hbm, w_ref, stage_ref, stage_sem)

    for src_ref, dst_ref in zip(cast_in, cast_out):
        dst_ref[...] = src_ref[...].astype(BF16)

    @pl.when(pl.program_id(0) % tiles_per_seq == 0)
    def _():
        xs_ref[:, 0:SUBLANES, :] = jnp.zeros((D_MODEL // LANES, SUBLANES, LANES), F32)
        h_ref[...] = jnp.zeros_like(h_ref)

    xn_ref[...] = _rms(x_ref[...], g_ref[...]).astype(BF16)

    def u_cols(c0, width):
        return jnp.dot(xn_ref[...], w_ref[:, c0:c0 + width], preferred_element_type=F32)

    def unit(j, n):
        acc = u_cols(j * D_MODEL + n * UNIT_COLS, UNIT_COLS)
        cols = slice(n * UNIT_COLS, (n + 1) * UNIT_COLS)
        if j < 3:
            for k in range(UNIT_COLS // HEAD_PAIR):
                hp = n * (UNIT_COLS // HEAD_PAIR) + k
                qkv_ref[j * N_PAIRS + hp] = acc[:, k * HEAD_PAIR:(k + 1) * HEAD_PAIR].astype(BF16)
        elif j == 4:
            gr_ref[:, cols] = acc
        else:
            gates_ref[:, (j - 5) * D_MODEL + cols.start:(j - 5) * D_MODEL + cols.stop] = acc.astype(BF16)

    units = [(j, n) for j in (4, 0, 1, 2, 5, 6) for n in range(D_MODEL // UNIT_COLS)]
    n_pieces = SUBLANES * (D_MODEL // LANES)
    total_cost = n_pieces * (COST_CONV + COST_GATE + COST_OUT) + (x_ref.shape[0] // SUBLANES) * COST_SCAN_STEP
    cost_per_unit = total_cost / len(units)
    spent = [0.0]

    def side_work(cost):
        spent[0] += cost
        while units and spent[0] >= cost_per_unit:
            spent[0] -= cost_per_unit
            unit(*units.pop(0))

    _rglru_tile(u_cols(3 * D_MODEL, D_MODEL), gr_ref, side_work,
                cw_ref, cb_ref, wr_ref, br_ref, wi_ref, bi_ref, lam_ref,
                hb_ref, xs_ref, xc_ref, rp_ref, ip_ref, a_ref, b_ref, hl_ref, pp_ref, h_ref)
    while units:
        unit(*units.pop(0))


BF16_SUBLANES = 2 * SUBLANES


def _cast_slice_spec(rows, cols, n_steps):
    repeat = next(k for k in (1, 2, 4, 8)
                  if rows * k % n_steps == 0 and (rows * k // n_steps) % BF16_SUBLANES == 0)
    return pl.BlockSpec((rows * repeat // n_steps, cols), lambda i: (i // repeat, 0))


def _inproj(x2, g, w, cw, cb, wr, br, wi, bi, lam, later_weights, seq):
    t = x2.shape[0]
    tm = RG_TILE
    n_steps = t // tm
    n_diag = D_MODEL // MXU_DIM
    scan_scratch = pltpu.VMEM((D_MODEL // LANES, SUBLANES * SEG_PITCH, LANES), F32)
    slab_scratch = pltpu.VMEM((tm, MXU_DIM), F32)
    cast_specs = [_cast_slice_spec(*wt.shape, n_steps) for wt in later_weights]
    outs = pl.pallas_call(
        functools.partial(_inproj_kernel, tiles_per_seq=seq // tm, n_cast=len(later_weights)),
        grid=(n_steps,),
        in_specs=[
            pl.BlockSpec((tm, D_MODEL), lambda i: (i, 0)),
            _resident((1, D_MODEL)),
            pl.BlockSpec(memory_space=pl.ANY),
            _resident((CONV_W, D_MODEL)),
            _resident((1, D_MODEL)),
            _resident((n_diag, MXU_DIM, MXU_DIM)),
            _resident((1, D_MODEL)),
            _resident((n_diag, MXU_DIM, MXU_DIM)),
            _resident((1, D_MODEL)),
            _resident((1, D_MODEL)),
            *cast_specs,
        ],
        out_specs=[
            pl.BlockSpec((3 * N_PAIRS, tm, HEAD_PAIR), lambda i: (0, i, 0)),
            pl.BlockSpec((tm, 2 * D_MODEL), lambda i: (i, 0)),
            pl.BlockSpec((tm, D_MODEL), lambda i: (i, 0)),
            *cast_specs,
        ],
        out_shape=[
            jax.ShapeDtypeStruct((3 * N_PAIRS, t, HEAD_PAIR), BF16),
            jax.ShapeDtypeStruct((t, 2 * D_MODEL), BF16),
            jax.ShapeDtypeStruct((t, D_MODEL), BF16),
            *[jax.ShapeDtypeStruct(wt.shape, BF16) for wt in later_weights],
        ],
        scratch_shapes=[
            pltpu.VMEM(w.shape, BF16),
            pltpu.VMEM((2, W_STAGE_ROWS, w.shape[1]), F32),
            pltpu.SemaphoreType.DMA((2,)),
            pltpu.VMEM((tm, D_MODEL), BF16),
            pltpu.VMEM((tm, D_MODEL), F32),
            pltpu.VMEM((D_MODEL // LANES, tm + SUBLANES, LANES), F32),
            slab_scratch, slab_scratch, slab_scratch,
            scan_scratch, scan_scratch, scan_scratch, scan_scratch,
            pltpu.VMEM((SUBLANES, D_MODEL), F32),
        ],
        compiler_params=pltpu.CompilerParams(
            dimension_semantics=("arbitrary",), vmem_limit_bytes=VMEM_LIMIT),
        name="inproj_rglru",
    )(x2, g, w, cw, cb, wr, br, wi, bi, lam, *later_weights)
    return outs[0], outs[1], outs[2], outs[3:]


def _band_geometry(ntiles):
    kw = ntiles * GROUP
    return kw, BAND - kw


def _attn_scores(q_ref, k_ref, s_ref, r0, ntiles):
    kw, c_off = _band_geometry(ntiles)
    lane = lax.broadcasted_iota(jnp.int32, (GROUP, HEAD_PAIR), 1)
    first_head = lane < HEAD_DIM
    qp = q_ref[0, pl.ds(r0, GROUP), :] * jnp.asarray(1.0 / math.sqrt(HEAD_DIM), BF16)
    zero = jnp.zeros_like(qp)
    qq = jnp.concatenate([jnp.where(first_head, qp, zero),
                          jnp.where(first_head, zero, qp)], axis=0)
    kk = k_ref[0, pl.ds(r0 + GROUP - kw, kw), :]
    s_ref[:, c_off:] = lax.dot_general(qq, kk, (((1,), (1,)), ((), ())),
                                       preferred_element_type=F32)


def _attn_softmax(s_ref, bias_ref, p_ref, l_ref, ntiles):
    _, c_off = _band_geometry(ntiles)
    rb = 32
    lane_rb = lax.broadcasted_iota(jnp.int32, (rb, LANES), 1)
    for hh in range(2):
        for i in range(GROUP // CHUNK):
            lo_col = max(i * CHUNK, c_off)
            hi_col = i * CHUNK + (LEFT_CHUNKS + 1) * CHUNK
            ws = (lo_col // LANES) * LANES
            we = -(-hi_col // LANES) * LANES
            nv = (we - ws) // LANES
            rows_all = slice(hh * GROUP + i * CHUNK, hh * GROUP + (i + 1) * CHUNK)
            if ws > c_off:
                p_ref[rows_all, c_off:ws] = jnp.zeros((CHUNK, ws - c_off), BF16)
            if we < BAND:
                p_ref[rows_all, we:BAND] = jnp.zeros((CHUNK, BAND - we), BF16)
            for b in range(CHUNK // rb):
                rows = slice(hh * GROUP + i * CHUNK + b * rb, hh * GROUP + i * CHUNK + (b + 1) * rb)
                brows = slice(i * CHUNK + b * rb, i * CHUNK + (b + 1) * rb)
                first_biased = brows.start + (BAND - GROUP) - MAX_REL + 1
                cols = []
                for c in range(nv):
                    tile = slice(ws + c * LANES, ws + (c + 1) * LANES)
                    col = s_ref[rows, tile]
                    if tile.stop > first_biased:
                        col = col + bias_ref[hh, brows, tile]
                    cols.append(col)
                if lo_col % LANES:
                    cols[0] = jnp.where(lane_rb >= HEAD_DIM, cols[0], NEG_INF)
                if hi_col % LANES:
                    cols[-1] = jnp.where(lane_rb < HEAD_DIM, cols[-1], NEG_INF)
                x = jnp.concatenate(cols, axis=1)
                m = jnp.max(x, axis=-1, keepdims=True)
                e = jnp.exp(x - m)
                l = jnp.sum(e, axis=-1, keepdims=True)
                p_ref[rows, ws:we] = e.astype(BF16)
                l_ref[rows, :] = jnp.broadcast_to(l, (rb, LANES))

def _attn_values(p_ref, l_ref, v_ref, o_ref, r0, ntiles):
    kw, c_off = _band_geometry(ntiles)
    lane = lax.broadcasted_iota(jnp.int32, (GROUP, HEAD_PAIR), 1)
    vv = v_ref[0, pl.ds(r0 + GROUP - kw, kw), :]
    o2 = jnp.dot(p_ref[:, c_off:], vv, preferred_element_type=F32)
    o2 = o2 / l_ref[...]
    o = jnp.where(lane < HEAD_DIM, o2[:GROUP], o2[GROUP:])
    o_ref[0, pl.ds(r0, GROUP), :] = o.astype(BF16)


def _attn_kernel(q_ref, k_ref, v_ref, w_ref, o_ref, bias_ref, s0, s1, p0, p1, l0, l1):
    n_groups = q_ref.shape[1] // GROUP
    full = BAND_TILES
    row0 = lambda g: pl.multiple_of(g * GROUP, GROUP)

    @pl.when(pl.program_id(1) == 0)
    def _():
        for hh in range(2):
            diag = jnp.broadcast_to(w_ref[hh], (GROUP, DIAG_LANES))
            rolled = pltpu.roll(diag, DIAG_LANES - (GROUP - 1), 1, stride=1, stride_axis=0)
            bias_ref[hh] = rolled[:, :BAND]

    def scores(g, s_ref, ntiles=full):
        _attn_scores(q_ref, k_ref, s_ref, row0(g), ntiles)

    def softmax(s_ref, p_ref, l_ref, ntiles=full):
        _attn_softmax(s_ref, bias_ref, p_ref, l_ref, ntiles)

    def values(g, p_ref, l_ref, ntiles=full):
        _attn_values(p_ref, l_ref, v_ref, o_ref, row0(g), ntiles)

    bufs = ((s0, p0, l0), (s1, p1, l1))
    ntiles = lambda g: min(g + 1, full)
    for t in range(n_groups + 2):
        if t >= 2:
            _, p_ref, l_ref = bufs[t % 2]
            values(t - 2, p_ref, l_ref, ntiles(t - 2))
        if 1 <= t <= n_groups:
            softmax(*bufs[(t - 1) % 2], ntiles(t - 1))
        if t < n_groups:
            scores(t, bufs[t % 2][0], ntiles(t))


def _attention(qkv, diagonals, batch, seq):
    t = batch * seq
    return pl.pallas_call(
        _attn_kernel,
        grid=(N_PAIRS, batch),
        in_specs=[
            pl.BlockSpec((1, seq, HEAD_PAIR), lambda hp, b: (hp, b, 0)),
            pl.BlockSpec((1, seq, HEAD_PAIR), lambda hp, b: (N_PAIRS + hp, b, 0)),
            pl.BlockSpec((1, seq, HEAD_PAIR), lambda hp, b: (2 * N_PAIRS + hp, b, 0)),
            pl.BlockSpec((2, 1, DIAG_LANES), lambda hp, b: (hp, 0, 0)),
        ],
        out_specs=pl.BlockSpec((1, seq, HEAD_PAIR), lambda hp, b: (hp, b, 0)),
        out_shape=jax.ShapeDtypeStruct((N_PAIRS, t, HEAD_PAIR), BF16),
        scratch_shapes=[
            pltpu.VMEM((2, GROUP, BAND), F32),
            pltpu.VMEM((2 * GROUP, BAND), F32),
            pltpu.VMEM((2 * GROUP, BAND), F32),
            pltpu.VMEM((2 * GROUP, BAND), BF16),
            pltpu.VMEM((2 * GROUP, BAND), BF16),
            pltpu.VMEM((2 * GROUP, LANES), F32),
            pltpu.VMEM((2 * GROUP, LANES), F32),
        ],
        compiler_params=pltpu.CompilerParams(
            dimension_semantics=("parallel", "arbitrary"), vmem_limit_bytes=VMEM_LIMIT),
        name="band_attention",
    )(qkv, qkv, qkv, diagonals)


def _band_bias_diagonals(rel_table):
    j = jnp.arange(DIAG_LANES)
    dist = (GROUP - 1 - j) + (BAND - GROUP)
    w = rel_table[:, jnp.clip(dist, -MAX_REL, MAX_REL) + MAX_REL] - rel_table[:, 2 * MAX_REL:]
    return w[:, None, :]


def _gelu_tanh(x):
    k = math.sqrt(2.0 / math.pi)
    inner = x * (k + (k * 0.044715) * (x * x))
    return x * (0.5 + 0.5 * jnp.tanh(inner))


def _rglru_tile(xr, gr_ref, side_work, cw_ref, cb_ref, wr_ref, br_ref, wi_ref, bi_ref, lam_ref,
                o_ref, xs_ref, xc_ref, rp_ref, ip_ref, a_ref, b_ref, hl_ref, pp_ref, h_ref):
    tm = xr.shape[0]
    seg = tm // SUBLANES
    n_lane_tiles = D_MODEL // LANES
    tiles_per_slab = MXU_DIM // LANES
    lane_tile = lambda c: slice(c * LANES, (c + 1) * LANES)

    for c in range(n_lane_tiles):
        xs_ref[c, SUBLANES:, :] = xr[:, lane_tile(c)]

    for d in range(D_MODEL // MXU_DIM):
        for k in range(tiles_per_slab):
            c = d * tiles_per_slab + k
            for j in range(SUBLANES):
                acc = cb_ref[:, lane_tile(c)]
                for tap in range(CONV_W):
                    off = SUBLANES - (CONV_W - 1) + tap + j * seg
                    acc = acc + xs_ref[c, off:off + seg, :] * cw_ref[tap:tap + 1, lane_tile(c)]
                xc_ref[j * seg:(j + 1) * seg, lane_tile(k)] = acc
                side_work(COST_CONV)
            xs_ref[c, 0:SUBLANES, :] = xs_ref[c, tm:tm + SUBLANES, :]
        xcb = xc_ref[...].astype(BF16)
        rp_ref[...] = jnp.dot(xcb, wr_ref[d], preferred_element_type=F32)
        ip_ref[...] = jnp.dot(xcb, wi_ref[d], preferred_element_type=F32)
        for k in range(tiles_per_slab):
            c = d * tiles_per_slab + k
            nlam = -lam_ref[:, lane_tile(c)]
            softplus = jnp.maximum(nlam, 0.0) + jnp.log1p(jnp.exp(-jnp.abs(nlam)))
            log_a_scale = (-RG_C) * softplus
            for j in range(SUBLANES):
                rows = slice(j * seg, (j + 1) * seg)
                xc = xc_ref[rows, lane_tile(k)]
                r = jax.nn.sigmoid(rp_ref[rows, lane_tile(k)] + br_ref[:, lane_tile(c)])
                ig = jax.nn.sigmoid(ip_ref[rows, lane_tile(k)] + bi_ref[:, lane_tile(c)])
                log_a = r * log_a_scale
                a = jnp.exp(log_a)
                y = jnp.tanh(log_a) * (-1.0 - a * a)
                mult = jnp.where(y > 0.0, y * lax.rsqrt(y), 0.0)
                srows = slice(j * SEG_PITCH, j * SEG_PITCH + seg)
                a_ref[c, srows, :] = a
                b_ref[c, srows, :] = mult * (ig * xc)
                side_work(COST_GATE)

    hs = [jnp.zeros((SUBLANES, LANES), F32) for _ in range(n_lane_tiles)]
    ps = [jnp.ones((SUBLANES, LANES), F32) for _ in range(n_lane_tiles)]
    for t in range(seg):
        idx = pl.ds(t, SUBLANES, stride=SEG_PITCH)
        for c in range(n_lane_tiles):
            a_t = a_ref[c, idx, :]
            hs[c] = a_t * hs[c] + b_ref[c, idx, :]
            ps[c] = a_t * ps[c]
            hl_ref[c, idx, :] = hs[c]
            pp_ref[c, idx, :] = ps[c]
        side_work(COST_SCAN_STEP)
    h_end = jnp.concatenate(hs, axis=1)
    p_end = jnp.concatenate(ps, axis=1)

    h_in = h_ref[0:1, :]
    for j in range(SUBLANES):
        rows = slice(j * seg, (j + 1) * seg)
        srows = slice(j * SEG_PITCH, j * SEG_PITCH + seg)
        for c in range(n_lane_tiles):
            hj = hl_ref[c, srows, :] + pp_ref[c, srows, :] * h_in[:, lane_tile(c)]
            o_ref[rows, lane_tile(c)] = (hj * _gelu_tanh(gr_ref[rows, lane_tile(c)])).astype(BF16)
            side_work(COST_OUT)
        h_in = p_end[j:j + 1, :] * h_in + h_end[j:j + 1, :]
    h_ref[0:1, :] = h_in


def _block_diag_tiles(w):
    per = MXU_DIM // RG_BLOCK
    n_diag = D_MODEL // MXU_DIM
    w4 = w.reshape(n_diag, per, RG_BLOCK, RG_BLOCK)
    eye = jnp.eye(per, dtype=w.dtype)
    tiles = jnp.einsum('dpij,pq->dpiqj', w4, eye)
    return tiles.reshape(n_diag, MXU_DIM, MXU_DIM)


def _mixffn_kernel(x_ref, at_ref, hb_ref, ga_ref, gb_ref, bm_ref, wa_ref, wb_ref, wo_ref,
                   gn_ref, win_ref, wout_ref, gf_ref, o_ref, *, splits):
    attn = jnp.concatenate([at_ref[hp] for hp in range(N_PAIRS)], axis=1)
    ya = jnp.dot(attn, wa_ref[...], preferred_element_type=F32)
    yb = jnp.dot(hb_ref[...], wb_ref[...], preferred_element_type=F32)
    g_a = jax.nn.sigmoid(ga_ref[...].astype(F32) + bm_ref[:, :D_MODEL])
    g_b = jax.nn.sigmoid(gb_ref[...].astype(F32) + bm_ref[:, D_MODEL:])
    mix = (g_a * ya + g_b * yb).astype(BF16)
    h = x_ref[...] + jnp.dot(mix, wo_ref[...], preferred_element_type=F32)
    hn = _rms(h, gn_ref[...]).astype(BF16)
    d_ff = wout_ref.shape[0]
    acc = h
    for c0, c1 in splits:
        g = jnp.dot(hn, win_ref[:, c0:c1], preferred_element_type=F32)
        up = jnp.dot(hn, win_ref[:, d_ff + c0:d_ff + c1], preferred_element_type=F32)
        act = (jax.nn.silu(g) * up).astype(BF16)
        acc = acc + jnp.dot(act, wout_ref[c0:c1, :], preferred_element_type=F32)
    o_ref[...] = _rms(acc, gf_ref[...])


def _mixffn(x2, attn, hb, gates, bm, wa, wb, wo, gn, win, wout, gf, tm):
    t = x2.shape[0]
    d_ff = wout.shape[0]
    edges = list(range(0, d_ff, FFN_CHUNK)) + [d_ff]
    splits = tuple(zip(edges[:-1], edges[1:]))
    row = lambda i: (i, 0)
    return pl.pallas_call(
        functools.partial(_mixffn_kernel, splits=splits),
        grid=(t // tm,),
        in_specs=[
            pl.BlockSpec((tm, D_MODEL), row),
            pl.BlockSpec((N_PAIRS, tm, HEAD_PAIR), lambda i: (0, i, 0)),
            pl.BlockSpec((tm, D_MODEL), row),
            pl.BlockSpec((tm, D_MODEL), lambda i: (i, 0)),
            pl.BlockSpec((tm, D_MODEL), lambda i: (i, 1)),
            _resident((1, 2 * D_MODEL)),
            _resident((D_MODEL, D_MODEL)),
            _resident((D_MODEL, D_MODEL)),
            _resident((D_MODEL, D_MODEL)),
            _resident((1, D_MODEL)),
            _resident((D_MODEL, 2 * d_ff)),
            _resident((d_ff, D_MODEL)),
            _resident((1, D_MODEL)),
        ],
        out_specs=pl.BlockSpec((tm, D_MODEL), row),
        out_shape=jax.ShapeDtypeStruct((t, D_MODEL), F32),
        compiler_params=pltpu.CompilerParams(
            dimension_semantics=("parallel",), vmem_limit_bytes=VMEM_LIMIT),
        name="mix_ffn",
    )(x2, attn, hb, gates, gates, bm, wa, wb, wo, gn, win, wout, gf)


def kernel(x, norm_mix_g, w_in, b_merge, rel_table, w_attn_out, conv_w, conv_b, w_rg_r, b_rg_r,
           w_rg_i, b_rg_i, rg_lambda, w_rnn_out, w_o, norm_ffn_g, w_ffn_in, w_ffn_out, final_norm_g):
    batch, seq, d = x.shape
    assert d == D_MODEL and seq % GROUP == 0 and seq // GROUP >= BAND_TILES
    assert w_in.shape[0] == 1, "single-layer block"
    x2 = x.reshape(batch * seq, d)
    row = lambda v: v.reshape(1, -1)

    assert seq % RG_TILE == 0
    later_weights = (w_attn_out[0], w_rnn_out[0], w_o[0], w_ffn_in[0], w_ffn_out[0])
    qkv, gates, hb, (wa, wb, wo, wfi, wfo) = _inproj(
        x2, row(norm_mix_g[0]), w_in[0], conv_w[0], row(conv_b[0]),
        _block_diag_tiles(w_rg_r[0]).astype(BF16), row(b_rg_r[0]),
        _block_diag_tiles(w_rg_i[0]).astype(BF16), row(b_rg_i[0]), row(rg_lambda[0]),
        later_weights, seq)
    attn = _attention(qkv, _band_bias_diagonals(rel_table[0]), batch, seq)
    out = _mixffn(x2, attn, hb, gates, row(b_merge[0]), wa, wb, wo, row(norm_ffn_g[0]), wfi, wfo,
                  row(final_norm_g), tm=512)
    return out.reshape(batch, seq, d)
```

```python
import functools
import math

import jax
import jax.numpy as jnp
from jax import lax
from jax.experimental import pallas as pl
from jax.experimental.pallas import tpu as pltpu

F32 = jnp.float32
BF16 = jnp.bfloat16

D_MODEL = 1024
CHUNK = 64
LEFT_CHUNKS = 8
N_HEADS = 16
HEAD_DIM = 64
MAX_REL = 128
N_RG_BLOCKS = 16
RG_BLOCK = D_MODEL // N_RG_BLOCKS
CONV_W = 4
RG_C = 8.0
EPS = 1e-6
NEG_INF = -1e30

LANES = 128
SUBLANES = 8
MXU_DIM = 256
HEAD_PAIR = 2 * HEAD_DIM
N_PAIRS = N_HEADS // 2
GROUP = 4 * CHUNK
BAND_TILES = LEFT_CHUNKS * CHUNK // GROUP + 1
BAND = BAND_TILES * GROUP
DIAG_LANES = -(-(GROUP + BAND - 1) // LANES) * LANES
VMEM_LIMIT = 56 * 1024 * 1024
RG_TILE = 512
W_STAGE_ROWS = 64
FFN_CHUNK = 3 * MXU_DIM
SEG_PITCH = RG_TILE // SUBLANES + SUBLANES // 2
COST_CONV, COST_GATE, COST_SCAN_STEP, COST_OUT = 16, 52, 23, 24


def _rms(x, g):
    var = jnp.mean(x * x, axis=-1, keepdims=True)
    return x * lax.rsqrt(var + EPS) * g


def _resident(shape):
    zeros = (0,) * len(shape)
    return pl.BlockSpec(shape, lambda *_: zeros, pipeline_mode=pl.Buffered(1))


def _load_rounded(src_hbm, dst_ref, stage_ref, sem):
    rows = stage_ref.shape[1]
    n_chunks = src_hbm.shape[0] // rows

    def chunk_copy(c):
        slot = c % 2
        return pltpu.make_async_copy(src_hbm.at[pl.ds(c * rows, rows), :], stage_ref.at[slot],
                                     sem.at[slot])

    chunk_copy(0).start()
    for c in range(n_chunks):
        if c + 1 < n_chunks:
            chunk_copy(c + 1).start()
        chunk_copy(c).wait()
        dst_ref[c * rows:(c + 1) * rows, :] = stage_ref[c % 2].astype(BF16)


def _inproj_kernel(*refs, tiles_per_seq, n_cast):
    (x_ref, g_ref, w_hbm, cw_ref, cb_ref, wr_ref, br_ref, wi_ref, bi_ref, lam_ref), refs = refs[:10], refs[10:]
    cast_in, refs = refs[:n_cast], refs[n_cast:]
    (qkv_ref, gates_ref, hb_ref), refs = refs[:3], refs[3:]
    cast_out, refs = refs[:n_cast], refs[n_cast:]
    (w_ref, stage_ref, stage_sem, xn_ref, gr_ref, xs_ref, xc_ref, rp_ref, ip_ref,
     a_ref, b_ref, hl_ref, pp_ref, h_ref) = refs

    @pl.when(pl.program_id(0) == 0)
    def _():
        _load_rounded(w_hbm, w_ref, stage_ref, stage_sem)

    for src_ref, dst_ref in zip(cast_in, cast_out):
        dst_ref[...] = src_ref[...].astype(BF16)

    @pl.when(pl.program_id(0) % tiles_per_seq == 0)
    def _():
        xs_ref[:, 0:SUBLANES, :] = jnp.zeros((D_MODEL // LANES, SUBLANES, LANES), F32)
        h_ref[...] = jnp.zeros_like(h_ref)

    xn_ref[...] = _rms(x_ref[...], g_ref[...]).astype(BF16)

    def u_cols(c0, width):
        return jnp.dot(xn_ref[...], w_ref[:, c0:c0 + width], preferred_element_type=F32)

    def unit(j, n):
        acc = u_cols(j * D_MODEL + n * MXU_DIM, MXU_DIM)
        if j < 3:
            for k in range(MXU_DIM // HEAD_PAIR):
                hp = n * (MXU_DIM // HEAD_PAIR) + k
                qkv_ref[j * N_PAIRS + hp] = acc[:, k * HEAD_PAIR:(k + 1) * HEAD_PAIR].astype(BF16)
        elif j == 4:
            gr_ref[:, n * MXU_DIM:(n + 1) * MXU_DIM] = acc
        else:
            c0 = (j - 5) * D_MODEL + n * MXU_DIM
            gates_ref[:, c0:c0 + MXU_DIM] = acc.astype(BF16)

    units = [(j, n) for j in (4, 0, 1, 2, 5, 6) for n in range(D_MODEL // MXU_DIM)]
    n_pieces = SUBLANES * (D_MODEL // LANES)
    total_cost = n_pieces * (COST_CONV + COST_GATE + COST_OUT) + (x_ref.shape[0] // SUBLANES) * COST_SCAN_STEP
    cost_per_unit = total_cost / len(units)
    spent = [0.0]

    def side_work(cost):
        spent[0] += cost
        while units and spent[0] >= cost_per_unit:
            spent[0] -= cost_per_unit
            unit(*units.pop(0))

    _rglru_tile(u_cols(3 * D_MODEL, D_MODEL), gr_ref, side_work,
                cw_ref, cb_ref, wr_ref, br_ref, wi_ref, bi_ref, lam_ref,
                hb_ref, xs_ref, xc_ref, rp_ref, ip_ref, a_ref, b_ref, hl_ref, pp_ref, h_ref)
    while units:
        unit(*units.pop(0))


BF16_SUBLANES = 2 * SUBLANES


def _cast_slice_spec(rows, cols, n_steps):
    repeat = next(k for k in (1, 2, 4, 8)
                  if rows * k % n_steps == 0 and (rows * k // n_steps) % BF16_SUBLANES == 0)
    return pl.BlockSpec((rows * repeat // n_steps, cols), lambda i: (i // repeat, 0))


def _inproj(x2, g, w, cw, cb, wr, br, wi, bi, lam, later_weights, seq):
    t = x2.shape[0]
    tm = RG_TILE
    n_steps = t // tm
    n_diag = D_MODEL // MXU_DIM
    scan_scratch = pltpu.VMEM((D_MODEL // LANES, SUBLANES * SEG_PITCH, LANES), F32)
    slab_scratch = pltpu.VMEM((tm, MXU_DIM), F32)
    cast_specs = [_cast_slice_spec(*wt.shape, n_steps) for wt in later_weights]
    outs = pl.pallas_call(
        functools.partial(_inproj_kernel, tiles_per_seq=seq // tm, n_cast=len(later_weights)),
        grid=(n_steps,),
        in_specs=[
            pl.BlockSpec((tm, D_MODEL), lambda i: (i, 0)),
            _resident((1, D_MODEL)),
            pl.BlockSpec(memory_space=pl.ANY),
            _resident((CONV_W, D_MODEL)),
            _resident((1, D_MODEL)),
            _resident((n_diag, MXU_DIM, MXU_DIM)),
            _resident((1, D_MODEL)),
            _resident((n_diag, MXU_DIM, MXU_DIM)),
            _resident((1, D_MODEL)),
            _resident((1, D_MODEL)),
            *cast_specs,
        ],
        out_specs=[
            pl.BlockSpec((3 * N_PAIRS, tm, HEAD_PAIR), lambda i: (0, i, 0)),
            pl.BlockSpec((tm, 2 * D_MODEL), lambda i: (i, 0)),
            pl.BlockSpec((tm, D_MODEL), lambda i: (i, 0)),
            *cast_specs,
        ],
        out_shape=[
            jax.ShapeDtypeStruct((3 * N_PAIRS, t, HEAD_PAIR), BF16),
            jax.ShapeDtypeStruct((t, 2 * D_MODEL), BF16),
            jax.ShapeDtypeStruct((t, D_MODEL), BF16),
            *[jax.ShapeDtypeStruct(wt.shape, BF16) for wt in later_weights],
        ],
        scratch_shapes=[
            pltpu.VMEM(w.shape, BF16),
            pltpu.VMEM((2, W_STAGE_ROWS, w.shape[1]), F32),
            pltpu.SemaphoreType.DMA((2,)),
            pltpu.VMEM((tm, D_MODEL), BF16),
            pltpu.VMEM((tm, D_MODEL), F32),
            pltpu.VMEM((D_MODEL // LANES, tm + SUBLANES, LANES), F32),
            slab_scratch, slab_scratch, slab_scratch,
            scan_scratch, scan_scratch, scan_scratch, scan_scratch,
            pltpu.VMEM((SUBLANES, D_MODEL), F32),
        ],
        compiler_params=pltpu.CompilerParams(
            dimension_semantics=("arbitrary",), vmem_limit_bytes=VMEM_LIMIT),
        name="inproj_rglru",
    )(x2, g, w, cw, cb, wr, br, wi, bi, lam, *later_weights)
    return outs[0], outs[1], outs[2], outs[3:]


def _band_geometry(ntiles):
    kw = ntiles * GROUP
    return kw, BAND - kw


def _attn_scores(q_ref, k_ref, s_ref, r0, ntiles):
    kw, c_off = _band_geometry(ntiles)
    lane = lax.broadcasted_iota(jnp.int32, (GROUP, HEAD_PAIR), 1)
    first_head = lane < HEAD_DIM
    qp = q_ref[0, pl.ds(r0, GROUP), :] * jnp.asarray(1.0 / math.sqrt(HEAD_DIM), BF16)
    zero = jnp.zeros_like(qp)
    qq = jnp.concatenate([jnp.where(first_head, qp, zero),
                          jnp.where(first_head, zero, qp)], axis=0)
    kk = k_ref[0, pl.ds(r0 + GROUP - kw, kw), :]
    s_ref[:, c_off:] = lax.dot_general(qq, kk, (((1,), (1,)), ((), ())),
                                       preferred_element_type=F32)


def _attn_softmax(s_ref, bias_ref, p_ref, l_ref, ntiles):
    _, c_off = _band_geometry(ntiles)
    rb = 32
    lane_rb = lax.broadcasted_iota(jnp.int32, (rb, LANES), 1)
    for hh in range(2):
        for i in range(GROUP // CHUNK):
            lo_col = max(i * CHUNK, c_off)
            hi_col = i * CHUNK + (LEFT_CHUNKS + 1) * CHUNK
            ws = (lo_col // LANES) * LANES
            we = -(-hi_col // LANES) * LANES
            nv = (we - ws) // LANES
            rows_all = slice(hh * GROUP + i * CHUNK, hh * GROUP + (i + 1) * CHUNK)
            if ws > c_off:
                p_ref[rows_all, c_off:ws] = jnp.zeros((CHUNK, ws - c_off), BF16)
            if we < BAND:
                p_ref[rows_all, we:BAND] = jnp.zeros((CHUNK, BAND - we), BF16)
            for b in range(CHUNK // rb):
                rows = slice(hh * GROUP + i * CHUNK + b * rb, hh * GROUP + i * CHUNK + (b + 1) * rb)
                brows = slice(i * CHUNK + b * rb, i * CHUNK + (b + 1) * rb)
                first_biased = brows.start + (BAND - GROUP) - MAX_REL + 1
                cols = []
                for c in range(nv):
                    tile = slice(ws + c * LANES, ws + (c + 1) * LANES)
                    col = s_ref[rows, tile]
                    if tile.stop > first_biased:
                        col = col + bias_ref[hh, brows, tile]
                    cols.append(col)
                if lo_col % LANES:
                    cols[0] = jnp.where(lane_rb >= HEAD_DIM, cols[0], NEG_INF)
                if hi_col % LANES:
                    cols[-1] = jnp.where(lane_rb < HEAD_DIM, cols[-1], NEG_INF)
                x = jnp.concatenate(cols, axis=1)
                m = jnp.max(x, axis=-1, keepdims=True)
                e = jnp.exp(x - m)
                l = jnp.sum(e, axis=-1, keepdims=True)
                p_ref[rows, ws:we] = e.astype(BF16)
                l_ref[rows, :] = jnp.broadcast_to(l, (rb, LANES))

def _attn_values(p_ref, l_ref, v_ref, o_ref, r0, ntiles):
    kw, c_off = _band_geometry(ntiles)
    lane = lax.broadcasted_iota(jnp.int32, (GROUP, HEAD_PAIR), 1)
    vv = v_ref[0, pl.ds(r0 + GROUP - kw, kw), :]
    o2 = jnp.dot(p_ref[:, c_off:], vv, preferred_element_type=F32)
    o2 = o2 / l_ref[...]
    o = jnp.where(lane < HEAD_DIM, o2[:GROUP], o2[GROUP:])
    o_ref[0, pl.ds(r0, GROUP), :] = o.astype(BF16)


def _attn_kernel(q_ref, k_ref, v_ref, w_ref, o_ref, bias_ref, s0, s1, p0, p1, l0, l1):
    n_groups = q_ref.shape[1] // GROUP
    full = BAND_TILES
    row0 = lambda g: pl.multiple_of(g * GROUP, GROUP)

    @pl.when(pl.program_id(1) == 0)
    def _():
        for hh in range(2):
            diag = jnp.broadcast_to(w_ref[hh], (GROUP, DIAG_LANES))
            rolled = pltpu.roll(diag, DIAG_LANES - (GROUP - 1), 1, stride=1, stride_axis=0)
            bias_ref[hh] = rolled[:, :BAND]

    def scores(g, s_ref, ntiles=full):
        _attn_scores(q_ref, k_ref, s_ref, row0(g), ntiles)

    def softmax(s_ref, p_ref, l_ref, ntiles=full):
        _attn_softmax(s_ref, bias_ref, p_ref, l_ref, ntiles)

    def values(g, p_ref, l_ref, ntiles=full):
        _attn_values(p_ref, l_ref, v_ref, o_ref, row0(g), ntiles)

    bufs = ((s0, p0, l0), (s1, p1, l1))
    ntiles = lambda g: min(g + 1, full)
    for t in range(n_groups + 2):
        if t >= 2:
            _, p_ref, l_ref = bufs[t % 2]
            values(t - 2, p_ref, l_ref, ntiles(t - 2))
        if 1 <= t <= n_groups:
            softmax(*bufs[(t - 1) % 2], ntiles(t - 1))
        if t < n_groups:
            scores(t, bufs[t % 2][0], ntiles(t))


def _attention(qkv, diagonals, batch, seq):
    t = batch * seq
    return pl.pallas_call(
        _attn_kernel,
        grid=(N_PAIRS, batch),
        in_specs=[
            pl.BlockSpec((1, seq, HEAD_PAIR), lambda hp, b: (hp, b, 0)),
            pl.BlockSpec((1, seq, HEAD_PAIR), lambda hp, b: (N_PAIRS + hp, b, 0)),
            pl.BlockSpec((1, seq, HEAD_PAIR), lambda hp, b: (2 * N_PAIRS + hp, b, 0)),
            pl.BlockSpec((2, 1, DIAG_LANES), lambda hp, b: (hp, 0, 0)),
        ],
        out_specs=pl.BlockSpec((1, seq, HEAD_PAIR), lambda hp, b: (hp, b, 0)),
        out_shape=jax.ShapeDtypeStruct((N_PAIRS, t, HEAD_PAIR), BF16),
        scratch_shapes=[
            pltpu.VMEM((2, GROUP, BAND), F32),
            pltpu.VMEM((2 * GROUP, BAND), F32),
            pltpu.VMEM((2 * GROUP, BAND), F32),
            pltpu.VMEM((2 * GROUP, BAND), BF16),
            pltpu.VMEM((2 * GROUP, BAND), BF16),
            pltpu.VMEM((2 * GROUP, LANES), F32),
            pltpu.VMEM((2 * GROUP, LANES), F32),
        ],
        compiler_params=pltpu.CompilerParams(
            dimension_semantics=("parallel", "arbitrary"), vmem_limit_bytes=VMEM_LIMIT),
        name="band_attention",
    )(qkv, qkv, qkv, diagonals)


def _band_bias_diagonals(rel_table):
    j = jnp.arange(DIAG_LANES)
    dist = (GROUP - 1 - j) + (BAND - GROUP)
    w = rel_table[:, jnp.clip(dist, -MAX_REL, MAX_REL) + MAX_REL] - rel_table[:, 2 * MAX_REL:]
    return w[:, None, :]


def _gelu_tanh(x):
    k = math.sqrt(2.0 / math.pi)
    inner = x * (k + (k * 0.044715) * (x * x))
    return x * (0.5 + 0.5 * jnp.tanh(inner))


def _rglru_tile(xr, gr_ref, side_work, cw_ref, cb_ref, wr_ref, br_ref, wi_ref, bi_ref, lam_ref,
                o_ref, xs_ref, xc_ref, rp_ref, ip_ref, a_ref, b_ref, hl_ref, pp_ref, h_ref):
    tm = xr.shape[0]
    seg = tm // SUBLANES
    n_lane_tiles = D_MODEL // LANES
    tiles_per_slab = MXU_DIM // LANES
    lane_tile = lambda c: slice(c * LANES, (c + 1) * LANES)

    for c in range(n_lane_tiles):
        xs_ref[c, SUBLANES:, :] = xr[:, lane_tile(c)]

    for d in range(D_MODEL // MXU_DIM):
        for k in range(tiles_per_slab):
            c = d * tiles_per_slab + k
            for j in range(SUBLANES):
                acc = cb_ref[:, lane_tile(c)]
                for tap in range(CONV_W):
                    off = SUBLANES - (CONV_W - 1) + tap + j * seg
                    acc = acc + xs_ref[c, off:off + seg, :] * cw_ref[tap:tap + 1, lane_tile(c)]
                xc_ref[j * seg:(j + 1) * seg, lane_tile(k)] = acc
                side_work(COST_CONV)
            xs_ref[c, 0:SUBLANES, :] = xs_ref[c, tm:tm + SUBLANES, :]
        xcb = xc_ref[...].astype(BF16)
        rp_ref[...] = jnp.dot(xcb, wr_ref[d], preferred_element_type=F32)
        ip_ref[...] = jnp.dot(xcb, wi_ref[d], preferred_element_type=F32)
        for k in range(tiles_per_slab):
            c = d * tiles_per_slab + k
            nlam = -lam_ref[:, lane_tile(c)]
            softplus = jnp.maximum(nlam, 0.0) + jnp.log1p(jnp.exp(-jnp.abs(nlam)))
            log_a_scale = (-RG_C) * softplus
            for j in range(SUBLANES):
                rows = slice(j * seg, (j + 1) * seg)
                xc = xc_ref[rows, lane_tile(k)]
                r = jax.nn.sigmoid(rp_ref[rows, lane_tile(k)] + br_ref[:, lane_tile(c)])
                ig = jax.nn.sigmoid(ip_ref[rows, lane_tile(k)] + bi_ref[:, lane_tile(c)])
                log_a = r * log_a_scale
                a = jnp.exp(log_a)
                y = jnp.tanh(log_a) * (-1.0 - a * a)
                mult = jnp.where(y > 0.0, y * lax.rsqrt(y), 0.0)
                srows = slice(j * SEG_PITCH, j * SEG_PITCH + seg)
                a_ref[c, srows, :] = a
                b_ref[c, srows, :] = mult * (ig * xc)
                side_work(COST_GATE)

    hs = [jnp.zeros((SUBLANES, LANES), F32) for _ in range(n_lane_tiles)]
    ps = [jnp.ones((SUBLANES, LANES), F32) for _ in range(n_lane_tiles)]
    for t in range(seg):
        idx = pl.ds(t, SUBLANES, stride=SEG_PITCH)
        for c in range(n_lane_tiles):
            a_t = a_ref[c, idx, :]
            hs[c] = a_t * hs[c] + b_ref[c, idx, :]
            ps[c] = a_t * ps[c]
            hl_ref[c, idx, :] = hs[c]
            pp_ref[c, idx, :] = ps[c]
        side_work(COST_SCAN_STEP)
    h_end = jnp.concatenate(hs, axis=1)
    p_end = jnp.concatenate(ps, axis=1)

    h_in = h_ref[0:1, :]
    for j in range(SUBLANES):
        rows = slice(j * seg, (j + 1) * seg)
        srows = slice(j * SEG_PITCH, j * SEG_PITCH + seg)
        for c in range(n_lane_tiles):
            hj = hl_ref[c, srows, :] + pp_ref[c, srows, :] * h_in[:, lane_tile(c)]
            o_ref[rows, lane_tile(c)] = (hj * _gelu_tanh(gr_ref[rows, lane_tile(c)])).astype(BF16)
            side_work(COST_OUT)
        h_in = p_end[j:j + 1, :] * h_in + h_end[j:j + 1, :]
    h_ref[0:1, :] = h_in


def _block_diag_tiles(w):
    per = MXU_DIM // RG_BLOCK
    n_diag = D_MODEL // MXU_DIM
    w4 = w.reshape(n_diag, per, RG_BLOCK, RG_BLOCK)
    eye = jnp.eye(per, dtype=w.dtype)
    tiles = jnp.einsum('dpij,pq->dpiqj', w4, eye)
    return tiles.reshape(n_diag, MXU_DIM, MXU_DIM)


def _mixffn_kernel(x_ref, at_ref, hb_ref, ga_ref, gb_ref, bm_ref, wa_ref, wb_ref, wo_ref,
                   gn_ref, win_ref, wout_ref, gf_ref, o_ref, *, splits):
    attn = jnp.concatenate([at_ref[hp] for hp in range(N_PAIRS)], axis=1)
    ya = jnp.dot(attn, wa_ref[...], preferred_element_type=F32)
    yb = jnp.dot(hb_ref[...], wb_ref[...], preferred_element_type=F32)
    g_a = jax.nn.sigmoid(ga_ref[...].astype(F32) + bm_ref[:, :D_MODEL])
    g_b = jax.nn.sigmoid(gb_ref[...].astype(F32) + bm_ref[:, D_MODEL:])
    mix = (g_a * ya + g_b * yb).astype(BF16)
    h = x_ref[...] + jnp.dot(mix, wo_ref[...], preferred_element_type=F32)
    hn = _rms(h, gn_ref[...]).astype(BF16)
    d_ff = wout_ref.shape[0]
    acts = []
    for c0, c1 in splits:
        g = jnp.dot(hn, win_ref[:, c0:c1], preferred_element_type=F32)
        up = jnp.dot(hn, win_ref[:, d_ff + c0:d_ff + c1], preferred_element_type=F32)
        acts.append((jax.nn.silu(g) * up).astype(BF16))
    act = jnp.concatenate(acts, axis=1)
    o_ref[...] = _rms(h + jnp.dot(act, wout_ref[...], preferred_element_type=F32), gf_ref[...])


def _mixffn(x2, attn, hb, gates, bm, wa, wb, wo, gn, win, wout, gf, tm):
    t = x2.shape[0]
    d_ff = wout.shape[0]
    edges = list(range(0, d_ff, FFN_CHUNK)) + [d_ff]
    splits = tuple(zip(edges[:-1], edges[1:]))
    row = lambda i: (i, 0)
    return pl.pallas_call(
        functools.partial(_mixffn_kernel, splits=splits),
        grid=(t // tm,),
        in_specs=[
            pl.BlockSpec((tm, D_MODEL), row),
            pl.BlockSpec((N_PAIRS, tm, HEAD_PAIR), lambda i: (0, i, 0)),
            pl.BlockSpec((tm, D_MODEL), row),
            pl.BlockSpec((tm, D_MODEL), lambda i: (i, 0)),
            pl.BlockSpec((tm, D_MODEL), lambda i: (i, 1)),
            _resident((1, 2 * D_MODEL)),
            _resident((D_MODEL, D_MODEL)),
            _resident((D_MODEL, D_MODEL)),
            _resident((D_MODEL, D_MODEL)),
            _resident((1, D_MODEL)),
            _resident((D_MODEL, 2 * d_ff)),
            _resident((d_ff, D_MODEL)),
            _resident((1, D_MODEL)),
        ],
        out_specs=pl.BlockSpec((tm, D_MODEL), row),
        out_shape=jax.ShapeDtypeStruct((t, D_MODEL), F32),
        compiler_params=pltpu.CompilerParams(
            dimension_semantics=("parallel",), vmem_limit_bytes=VMEM_LIMIT),
        name="mix_ffn",
    )(x2, attn, hb, gates, gates, bm, wa, wb, wo, gn, win, wout, gf)


def kernel(x, norm_mix_g, w_in, b_merge, rel_table, w_attn_out, conv_w, conv_b, w_rg_r, b_rg_r,
           w_rg_i, b_rg_i, rg_lambda, w_rnn_out, w_o, norm_ffn_g, w_ffn_in, w_ffn_out, final_norm_g):
    batch, seq, d = x.shape
    assert d == D_MODEL and seq % GROUP == 0 and seq // GROUP >= BAND_TILES
    assert w_in.shape[0] == 1, "single-layer block"
    x2 = x.reshape(batch * seq, d)
    row = lambda v: v.reshape(1, -1)

    assert seq % RG_TILE == 0
    later_weights = (w_attn_out[0], w_rnn_out[0], w_o[0], w_ffn_in[0], w_ffn_out[0])
    qkv, gates, hb, (wa, wb, wo, wfi, wfo) = _inproj(
        x2, row(norm_mix_g[0]), w_in[0], conv_w[0], row(conv_b[0]),
        _block_diag_tiles(w_rg_r[0]).astype(BF16), row(b_rg_r[0]),
        _block_diag_tiles(w_rg_i[0]).astype(BF16), row(b_rg_i[0]), row(rg_lambda[0]),
        later_weights, seq)
    attn = _attention(qkv, _band_bias_diagonals(rel_table[0]), batch, seq)
    out = _mixffn(x2, attn, hb, gates, row(b_merge[0]), wa, wb, wo, row(norm_ffn_g[0]), wfi, wfo,
                  row(final_norm_g), tm=512)
    return out.reshape(batch, seq, d)
```

```python
import functools
import math

import jax
import jax.numpy as jnp
from jax import lax
from jax.experimental import pallas as pl
from jax.experimental.pallas import tpu as pltpu

F32 = jnp.float32
BF16 = jnp.bfloat16

D_MODEL = 1024
CHUNK = 64
LEFT_CHUNKS = 8
N_HEADS = 16
HEAD_DIM = 64
MAX_REL = 128
N_RG_BLOCKS = 16
RG_BLOCK = D_MODEL // N_RG_BLOCKS
CONV_W = 4
RG_C = 8.0
EPS = 1e-6
NEG_INF = -1e30

LANES = 128
SUBLANES = 8
MXU_DIM = 256
HEAD_PAIR = 2 * HEAD_DIM
N_PAIRS = N_HEADS // 2
GROUP = 4 * CHUNK
BAND_TILES = LEFT_CHUNKS * CHUNK // GROUP + 1
BAND = BAND_TILES * GROUP
DIAG_LANES = -(-(GROUP + BAND - 1) // LANES) * LANES
VMEM_LIMIT = 56 * 1024 * 1024
RG_TILE = 512
W_STAGE_ROWS = 64
FFN_CHUNK = 3 * MXU_DIM
SEG_PITCH = RG_TILE // SUBLANES + SUBLANES // 2
COST_CONV, COST_GATE, COST_SCAN_STEP, COST_OUT = 16, 52, 23, 24


def _rms(x, g):
    var = jnp.mean(x * x, axis=-1, keepdims=True)
    return x * lax.rsqrt(var + EPS) * g


def _resident(shape):
    zeros = (0,) * len(shape)
    return pl.BlockSpec(shape, lambda *_: zeros, pipeline_mode=pl.Buffered(1))


def _load_rounded(src_hbm, dst_ref, stage_ref, sem):
    rows = stage_ref.shape[1]
    n_chunks = src_hbm.shape[0] // rows

    def chunk_copy(c):
        slot = c % 2
        return pltpu.make_async_copy(src_hbm.at[pl.ds(c * rows, rows), :], stage_ref.at[slot],
                                     sem.at[slot])

    chunk_copy(0).start()
    for c in range(n_chunks):
        if c + 1 < n_chunks:
            chunk_copy(c + 1).start()
        chunk_copy(c).wait()
        dst_ref[c * rows:(c + 1) * rows, :] = stage_ref[c % 2].astype(BF16)


def _inproj_kernel(*refs, tiles_per_seq, n_cast):
    (x_ref, g_ref, w_hbm, cw_ref, cb_ref, wr_ref, br_ref, wi_ref, bi_ref, lam_ref), refs = refs[:10], refs[10:]
    cast_in, refs = refs[:n_cast], refs[n_cast:]
    (qkv_ref, gates_ref, hb_ref), refs = refs[:3], refs[3:]
    cast_out, refs = refs[:n_cast], refs[n_cast:]
    (w_ref, stage_ref, stage_sem, xn_ref, gr_ref, xs_ref, xc_ref, rp_ref, ip_ref,
     a_ref, b_ref, hl_ref, pp_ref, h_ref) = refs

    @pl.when(pl.program_id(0) == 0)
    def _():
        _load_rounded(w_hbm, w_ref, stage_ref, stage_sem)

    for src_ref, dst_ref in zip(cast_in, cast_out):
        dst_ref[...] = src_ref[...].astype(BF16)

    @pl.when(pl.program_id(0) % tiles_per_seq == 0)
    def _():
        xs_ref[:, 0:SUBLANES, :] = jnp.zeros((D_MODEL // LANES, SUBLANES, LANES), F32)
        h_ref[...] = jnp.zeros_like(h_ref)

    xn_ref[...] = _rms(x_ref[...], g_ref[...]).astype(BF16)

    def u_cols(c0, width):
        return jnp.dot(xn_ref[...], w_ref[:, c0:c0 + width], preferred_element_type=F32)

    def unit(j, n):
        acc = u_cols(j * D_MODEL + n * MXU_DIM, MXU_DIM)
        if j < 3:
            for k in range(MXU_DIM // HEAD_PAIR):
                hp = n * (MXU_DIM // HEAD_PAIR) + k
                qkv_ref[j * N_PAIRS + hp] = acc[:, k * HEAD_PAIR:(k + 1) * HEAD_PAIR].astype(BF16)
        elif j == 4:
            gr_ref[:, n * MXU_DIM:(n + 1) * MXU_DIM] = acc
        else:
            c0 = (j - 5) * D_MODEL + n * MXU_DIM
            gates_ref[:, c0:c0 + MXU_DIM] = acc.astype(BF16)

    units = [(j, n) for j in (4, 0, 1, 2, 5, 6) for n in range(D_MODEL // MXU_DIM)]
    n_pieces = SUBLANES * (D_MODEL // LANES)
    total_cost = n_pieces * (COST_CONV + COST_GATE + COST_OUT) + (x_ref.shape[0] // SUBLANES) * COST_SCAN_STEP
    cost_per_unit = total_cost / len(units)
    spent = [0.0]

    def side_work(cost):
        spent[0] += cost
        while units and spent[0] >= cost_per_unit:
            spent[0] -= cost_per_unit
            unit(*units.pop(0))

    _rglru_tile(u_cols(3 * D_MODEL, D_MODEL), gr_ref, side_work,
                cw_ref, cb_ref, wr_ref, br_ref, wi_ref, bi_ref, lam_ref,
                hb_ref, xs_ref, xc_ref, rp_ref, ip_ref, a_ref, b_ref, hl_ref, pp_ref, h_ref)
    while units:
        unit(*units.pop(0))


BF16_SUBLANES = 2 * SUBLANES


def _cast_slice_spec(rows, cols, n_steps):
    repeat = next(k for k in (1, 2, 4, 8)
                  if rows * k % n_steps == 0 and (rows * k // n_steps) % BF16_SUBLANES == 0)
    return pl.BlockSpec((rows * repeat // n_steps, cols), lambda i: (i // repeat, 0))


def _inproj(x2, g, w, cw, cb, wr, br, wi, bi, lam, later_weights, seq):
    t = x2.shape[0]
    tm = RG_TILE
    n_steps = t // tm
    n_diag = D_MODEL // MXU_DIM
    scan_scratch = pltpu.VMEM((D_MODEL // LANES, SUBLANES * SEG_PITCH, LANES), F32)
    slab_scratch = pltpu.VMEM((tm, MXU_DIM), F32)
    cast_specs = [_cast_slice_spec(*wt.shape, n_steps) for wt in later_weights]
    outs = pl.pallas_call(
        functools.partial(_inproj_kernel, tiles_per_seq=seq // tm, n_cast=len(later_weights)),
        grid=(n_steps,),
        in_specs=[
            pl.BlockSpec((tm, D_MODEL), lambda i: (i, 0)),
            _resident((1, D_MODEL)),
            pl.BlockSpec(memory_space=pl.ANY),
            _resident((CONV_W, D_MODEL)),
            _resident((1, D_MODEL)),
            _resident((n_diag, MXU_DIM, MXU_DIM)),
            _resident((1, D_MODEL)),
            _resident((n_diag, MXU_DIM, MXU_DIM)),
            _resident((1, D_MODEL)),
            _resident((1, D_MODEL)),
            *cast_specs,
        ],
        out_specs=[
            pl.BlockSpec((3 * N_PAIRS, tm, HEAD_PAIR), lambda i: (0, i, 0)),
            pl.BlockSpec((tm, 2 * D_MODEL), lambda i: (i, 0)),
            pl.BlockSpec((tm, D_MODEL), lambda i: (i, 0)),
            *cast_specs,
        ],
        out_shape=[
            jax.ShapeDtypeStruct((3 * N_PAIRS, t, HEAD_PAIR), BF16),
            jax.ShapeDtypeStruct((t, 2 * D_MODEL), BF16),
            jax.ShapeDtypeStruct((t, D_MODEL), BF16),
            *[jax.ShapeDtypeStruct(wt.shape, BF16) for wt in later_weights],
        ],
        scratch_shapes=[
            pltpu.VMEM(w.shape, BF16),
            pltpu.VMEM((2, W_STAGE_ROWS, w.shape[1]), F32),
            pltpu.SemaphoreType.DMA((2,)),
            pltpu.VMEM((tm, D_MODEL), BF16),
            pltpu.VMEM((tm, D_MODEL), F32),
            pltpu.VMEM((D_MODEL // LANES, tm + SUBLANES, LANES), F32),
            slab_scratch, slab_scratch, slab_scratch,
            scan_scratch, scan_scratch, scan_scratch, scan_scratch,
            pltpu.VMEM((SUBLANES, D_MODEL), F32),
        ],
        compiler_params=pltpu.CompilerParams(
            dimension_semantics=("arbitrary",), vmem_limit_bytes=VMEM_LIMIT),
        name="inproj_rglru",
    )(x2, g, w, cw, cb, wr, br, wi, bi, lam, *later_weights)
    return outs[0], outs[1], outs[2], outs[3:]


def _band_geometry(ntiles):
    kw = ntiles * GROUP
    return kw, BAND - kw


def _attn_scores(q_ref, k_ref, s_ref, r0, ntiles):
    kw, c_off = _band_geometry(ntiles)
    lane = lax.broadcasted_iota(jnp.int32, (GROUP, HEAD_PAIR), 1)
    first_head = lane < HEAD_DIM
    qp = q_ref[0, pl.ds(r0, GROUP), :] * jnp.asarray(1.0 / math.sqrt(HEAD_DIM), BF16)
    zero = jnp.zeros_like(qp)
    qq = jnp.concatenate([jnp.where(first_head, qp, zero),
                          jnp.where(first_head, zero, qp)], axis=0)
    kk = k_ref[0, pl.ds(r0 + GROUP - kw, kw), :]
    s_ref[:, c_off:] = lax.dot_general(qq, kk, (((1,), (1,)), ((), ())),
                                       preferred_element_type=F32)


def _attn_softmax(s_ref, bias_ref, p_ref, ntiles):
    _, c_off = _band_geometry(ntiles)
    rb = 32
    lane_rb = lax.broadcasted_iota(jnp.int32, (rb, LANES), 1)
    for hh in range(2):
        for i in range(GROUP // CHUNK):
            lo_col = max(i * CHUNK, c_off)
            hi_col = i * CHUNK + (LEFT_CHUNKS + 1) * CHUNK
            ws = (lo_col // LANES) * LANES
            we = -(-hi_col // LANES) * LANES
            nv = (we - ws) // LANES
            rows_all = slice(hh * GROUP + i * CHUNK, hh * GROUP + (i + 1) * CHUNK)
            if ws > c_off:
                p_ref[rows_all, c_off:ws] = jnp.zeros((CHUNK, ws - c_off), BF16)
            if we < BAND:
                p_ref[rows_all, we:BAND] = jnp.zeros((CHUNK, BAND - we), BF16)
            for b in range(CHUNK // rb):
                rows = slice(hh * GROUP + i * CHUNK + b * rb, hh * GROUP + i * CHUNK + (b + 1) * rb)
                brows = slice(i * CHUNK + b * rb, i * CHUNK + (b + 1) * rb)
                first_biased = brows.start + (BAND - GROUP) - MAX_REL + 1
                cols = []
                for c in range(nv):
                    tile = slice(ws + c * LANES, ws + (c + 1) * LANES)
                    col = s_ref[rows, tile]
                    if tile.stop > first_biased:
                        col = col + bias_ref[hh, brows, tile]
                    cols.append(col)
                if lo_col % LANES:
                    cols[0] = jnp.where(lane_rb >= HEAD_DIM, cols[0], NEG_INF)
                if hi_col % LANES:
                    cols[-1] = jnp.where(lane_rb < HEAD_DIM, cols[-1], NEG_INF)
                x = jnp.concatenate(cols, axis=1)
                m = jnp.max(x, axis=-1, keepdims=True)
                p_ref[rows, ws:we] = jnp.exp(x - m).astype(BF16)


def _attn_values(p_ref, v_ref, o_ref, r0, ntiles):
    kw, c_off = _band_geometry(ntiles)
    lane = lax.broadcasted_iota(jnp.int32, (GROUP, HEAD_PAIR), 1)
    vv = v_ref[0, pl.ds(r0 + GROUP - kw, kw), :]
    v_ones = jnp.concatenate([vv, jnp.ones((kw, LANES), BF16)], axis=1)
    o2 = jnp.dot(p_ref[:, c_off:], v_ones, preferred_element_type=F32)
    o2 = o2[:, :HEAD_PAIR] / o2[:, HEAD_PAIR:]
    o = jnp.where(lane < HEAD_DIM, o2[:GROUP], o2[GROUP:])
    o_ref[0, pl.ds(r0, GROUP), :] = o.astype(BF16)


def _attn_kernel(q_ref, k_ref, v_ref, w_ref, o_ref, bias_ref, s0, s1, p0, p1):
    n_groups = q_ref.shape[1] // GROUP
    full = BAND_TILES
    row0 = lambda g: pl.multiple_of(g * GROUP, GROUP)

    @pl.when(pl.program_id(1) == 0)
    def _():
        for hh in range(2):
            diag = jnp.broadcast_to(w_ref[hh], (GROUP, DIAG_LANES))
            rolled = pltpu.roll(diag, DIAG_LANES - (GROUP - 1), 1, stride=1, stride_axis=0)
            bias_ref[hh] = rolled[:, :BAND]

    def scores(g, s_ref, ntiles=full):
        _attn_scores(q_ref, k_ref, s_ref, row0(g), ntiles)

    def softmax(s_ref, p_ref, ntiles=full):
        _attn_softmax(s_ref, bias_ref, p_ref, ntiles)

    def values(g, p_ref, ntiles=full):
        _attn_values(p_ref, v_ref, o_ref, row0(g), ntiles)

    bufs = ((s0, p0), (s1, p1))
    ntiles = lambda g: min(g + 1, full)
    for t in range(n_groups + 2):
        if t >= 2:
            values(t - 2, bufs[t % 2][1], ntiles(t - 2))
        if 1 <= t <= n_groups:
            softmax(*bufs[(t - 1) % 2], ntiles(t - 1))
        if t < n_groups:
            scores(t, bufs[t % 2][0], ntiles(t))


def _attention(qkv, diagonals, batch, seq):
    t = batch * seq
    return pl.pallas_call(
        _attn_kernel,
        grid=(N_PAIRS, batch),
        in_specs=[
            pl.BlockSpec((1, seq, HEAD_PAIR), lambda hp, b: (hp, b, 0)),
            pl.BlockSpec((1, seq, HEAD_PAIR), lambda hp, b: (N_PAIRS + hp, b, 0)),
            pl.BlockSpec((1, seq, HEAD_PAIR), lambda hp, b: (2 * N_PAIRS + hp, b, 0)),
            pl.BlockSpec((2, 1, DIAG_LANES), lambda hp, b: (hp, 0, 0)),
        ],
        out_specs=pl.BlockSpec((1, seq, HEAD_PAIR), lambda hp, b: (hp, b, 0)),
        out_shape=jax.ShapeDtypeStruct((N_PAIRS, t, HEAD_PAIR), BF16),
        scratch_shapes=[
            pltpu.VMEM((2, GROUP, BAND), F32),
            pltpu.VMEM((2 * GROUP, BAND), F32),
            pltpu.VMEM((2 * GROUP, BAND), F32),
            pltpu.VMEM((2 * GROUP, BAND), BF16),
            pltpu.VMEM((2 * GROUP, BAND), BF16),
        ],
        compiler_params=pltpu.CompilerParams(
            dimension_semantics=("parallel", "arbitrary"), vmem_limit_bytes=VMEM_LIMIT),
        name="band_attention",
    )(qkv, qkv, qkv, diagonals)


def _band_bias_diagonals(rel_table):
    j = jnp.arange(DIAG_LANES)
    dist = (GROUP - 1 - j) + (BAND - GROUP)
    w = rel_table[:, jnp.clip(dist, -MAX_REL, MAX_REL) + MAX_REL] - rel_table[:, 2 * MAX_REL:]
    return w[:, None, :]


def _gelu_tanh(x):
    k = math.sqrt(2.0 / math.pi)
    inner = x * (k + (k * 0.044715) * (x * x))
    return x * (0.5 + 0.5 * jnp.tanh(inner))


def _rglru_tile(xr, gr_ref, side_work, cw_ref, cb_ref, wr_ref, br_ref, wi_ref, bi_ref, lam_ref,
                o_ref, xs_ref, xc_ref, rp_ref, ip_ref, a_ref, b_ref, hl_ref, pp_ref, h_ref):
    tm = xr.shape[0]
    seg = tm // SUBLANES
    n_lane_tiles = D_MODEL // LANES
    tiles_per_slab = MXU_DIM // LANES
    lane_tile = lambda c: slice(c * LANES, (c + 1) * LANES)

    for c in range(n_lane_tiles):
        xs_ref[c, SUBLANES:, :] = xr[:, lane_tile(c)]

    for d in range(D_MODEL // MXU_DIM):
        for k in range(tiles_per_slab):
            c = d * tiles_per_slab + k
            for j in range(SUBLANES):
                acc = cb_ref[:, lane_tile(c)]
                for tap in range(CONV_W):
                    off = SUBLANES - (CONV_W - 1) + tap + j * seg
                    acc = acc + xs_ref[c, off:off + seg, :] * cw_ref[tap:tap + 1, lane_tile(c)]
                xc_ref[j * seg:(j + 1) * seg, lane_tile(k)] = acc
                side_work(COST_CONV)
            xs_ref[c, 0:SUBLANES, :] = xs_ref[c, tm:tm + SUBLANES, :]
        xcb = xc_ref[...].astype(BF16)
        rp_ref[...] = jnp.dot(xcb, wr_ref[d], preferred_element_type=F32)
        ip_ref[...] = jnp.dot(xcb, wi_ref[d], preferred_element_type=F32)
        for k in range(tiles_per_slab):
            c = d * tiles_per_slab + k
            nlam = -lam_ref[:, lane_tile(c)]
            softplus = jnp.maximum(nlam, 0.0) + jnp.log1p(jnp.exp(-jnp.abs(nlam)))
            log_a_scale = (-RG_C) * softplus
            for j in range(SUBLANES):
                rows = slice(j * seg, (j + 1) * seg)
                xc = xc_ref[rows, lane_tile(k)]
                r = jax.nn.sigmoid(rp_ref[rows, lane_tile(k)] + br_ref[:, lane_tile(c)])
                ig = jax.nn.sigmoid(ip_ref[rows, lane_tile(k)] + bi_ref[:, lane_tile(c)])
                log_a = r * log_a_scale
                a = jnp.exp(log_a)
                y = jnp.tanh(log_a) * (-1.0 - a * a)
                mult = jnp.where(y > 0.0, y * lax.rsqrt(y), 0.0)
                srows = slice(j * SEG_PITCH, j * SEG_PITCH + seg)
                a_ref[c, srows, :] = a
                b_ref[c, srows, :] = mult * (ig * xc)
                side_work(COST_GATE)

    hs = [jnp.zeros((SUBLANES, LANES), F32) for _ in range(n_lane_tiles)]
    ps = [jnp.ones((SUBLANES, LANES), F32) for _ in range(n_lane_tiles)]
    for t in range(seg):
        idx = pl.ds(t, SUBLANES, stride=SEG_PITCH)
        for c in range(n_lane_tiles):
            a_t = a_ref[c, idx, :]
            hs[c] = a_t * hs[c] + b_ref[c, idx, :]
            ps[c] = a_t * ps[c]
            hl_ref[c, idx, :] = hs[c]
            pp_ref[c, idx, :] = ps[c]
        side_work(COST_SCAN_STEP)
    h_end = jnp.concatenate(hs, axis=1)
    p_end = jnp.concatenate(ps, axis=1)

    h_in = h_ref[0:1, :]
    for j in range(SUBLANES):
        rows = slice(j * seg, (j + 1) * seg)
        srows = slice(j * SEG_PITCH, j * SEG_PITCH + seg)
        for c in range(n_lane_tiles):
            hj = hl_ref[c, srows, :] + pp_ref[c, srows, :] * h_in[:, lane_tile(c)]
            o_ref[rows, lane_tile(c)] = (hj * _gelu_tanh(gr_ref[rows, lane_tile(c)])).astype(BF16)
            side_work(COST_OUT)
        h_in = p_end[j:j + 1, :] * h_in + h_end[j:j + 1, :]
    h_ref[0:1, :] = h_in


def _block_diag_tiles(w):
    per = MXU_DIM // RG_BLOCK
    n_diag = D_MODEL // MXU_DIM
    w4 = w.reshape(n_diag, per, RG_BLOCK, RG_BLOCK)
    eye = jnp.eye(per, dtype=w.dtype)
    tiles = jnp.einsum('dpij,pq->dpiqj', w4, eye)
    return tiles.reshape(n_diag, MXU_DIM, MXU_DIM)


def _mixffn_kernel(x_ref, at_ref, hb_ref, ga_ref, gb_ref, bm_ref, wa_ref, wb_ref, wo_ref,
                   gn_ref, win_ref, wout_ref, gf_ref, o_ref, *, splits):
    attn = jnp.concatenate([at_ref[hp] for hp in range(N_PAIRS)], axis=1)
    ya = jnp.dot(attn, wa_ref[...], preferred_element_type=F32)
    yb = jnp.dot(hb_ref[...], wb_ref[...], preferred_element_type=F32)
    g_a = jax.nn.sigmoid(ga_ref[...].astype(F32) + bm_ref[:, :D_MODEL])
    g_b = jax.nn.sigmoid(gb_ref[...].astype(F32) + bm_ref[:, D_MODEL:])
    mix = (g_a * ya + g_b * yb).astype(BF16)
    h = x_ref[...] + jnp.dot(mix, wo_ref[...], preferred_element_type=F32)
    hn = _rms(h, gn_ref[...]).astype(BF16)
    d_ff = wout_ref.shape[0]
    acc = h
    for c0, c1 in splits:
        g = jnp.dot(hn, win_ref[:, c0:c1], preferred_element_type=F32)
        up = jnp.dot(hn, win_ref[:, d_ff + c0:d_ff + c1], preferred_element_type=F32)
        act = (jax.nn.silu(g) * up).astype(BF16)
        acc = acc + jnp.dot(act, wout_ref[c0:c1, :], preferred_element_type=F32)
    o_ref[...] = _rms(acc, gf_ref[...])


def _mixffn(x2, attn, hb, gates, bm, wa, wb, wo, gn, win, wout, gf, tm):
    t = x2.shape[0]
    d_ff = wout.shape[0]
    edges = list(range(0, d_ff, FFN_CHUNK)) + [d_ff]
    splits = tuple(zip(edges[:-1], edges[1:]))
    row = lambda i: (i, 0)
    return pl.pallas_call(
        functools.partial(_mixffn_kernel, splits=splits),
        grid=(t // tm,),
        in_specs=[
            pl.BlockSpec((tm, D_MODEL), row),
            pl.BlockSpec((N_PAIRS, tm, HEAD_PAIR), lambda i: (0, i, 0)),
            pl.BlockSpec((tm, D_MODEL), row),
            pl.BlockSpec((tm, D_MODEL), lambda i: (i, 0)),
            pl.BlockSpec((tm, D_MODEL), lambda i: (i, 1)),
            _resident((1, 2 * D_MODEL)),
            _resident((D_MODEL, D_MODEL)),
            _resident((D_MODEL, D_MODEL)),
            _resident((D_MODEL, D_MODEL)),
            _resident((1, D_MODEL)),
            _resident((D_MODEL, 2 * d_ff)),
            _resident((d_ff, D_MODEL)),
            _resident((1, D_MODEL)),
        ],
        out_specs=pl.BlockSpec((tm, D_MODEL), row),
        out_shape=jax.ShapeDtypeStruct((t, D_MODEL), F32),
        compiler_params=pltpu.CompilerParams(
            dimension_semantics=("parallel",), vmem_limit_bytes=VMEM_LIMIT),
        name="mix_ffn",
    )(x2, attn, hb, gates, gates, bm, wa, wb, wo, gn, win, wout, gf)


def kernel(x, norm_mix_g, w_in, b_merge, rel_table, w_attn_out, conv_w, conv_b, w_rg_r, b_rg_r,
           w_rg_i, b_rg_i, rg_lambda, w_rnn_out, w_o, norm_ffn_g, w_ffn_in, w_ffn_out, final_norm_g):
    batch, seq, d = x.shape
    assert d == D_MODEL and seq % GROUP == 0 and seq // GROUP >= BAND_TILES
    assert w_in.shape[0] == 1, "single-layer block"
    x2 = x.reshape(batch * seq, d)
    row = lambda v: v.reshape(1, -1)

    assert seq % RG_TILE == 0
    later_weights = (w_attn_out[0], w_rnn_out[0], w_o[0], w_ffn_in[0], w_ffn_out[0])
    qkv, gates, hb, (wa, wb, wo, wfi, wfo) = _inproj(
        x2, row(norm_mix_g[0]), w_in[0], conv_w[0], row(conv_b[0]),
        _block_diag_tiles(w_rg_r[0]).astype(BF16), row(b_rg_r[0]),
        _block_diag_tiles(w_rg_i[0]).astype(BF16), row(b_rg_i[0]), row(rg_lambda[0]),
        later_weights, seq)
    attn = _attention(qkv, _band_bias_diagonals(rel_table[0]), batch, seq)
    out = _mixffn(x2, attn, hb, gates, row(b_merge[0]), wa, wb, wo, row(norm_ffn_g[0]), wfi, wfo,
                  row(final_norm_g), tm=512)
    return out.reshape(batch, seq, d)
```

```python
import functools
import math

import jax
import jax.numpy as jnp
from jax import lax
from jax.experimental import pallas as pl
from jax.experimental.pallas import tpu as pltpu

F32 = jnp.float32
BF16 = jnp.bfloat16

D_MODEL = 1024
CHUNK = 64
LEFT_CHUNKS = 8
N_HEADS = 16
HEAD_DIM = 64
MAX_REL = 128
N_RG_BLOCKS = 16
RG_BLOCK = D_MODEL // N_RG_BLOCKS
CONV_W = 4
RG_C = 8.0
EPS = 1e-6
NEG_INF = -1e30

LANES = 128
SUBLANES = 8
MXU_DIM = 256
HEAD_PAIR = 2 * HEAD_DIM
N_PAIRS = N_HEADS // 2
GROUP = 4 * CHUNK
BAND_TILES = LEFT_CHUNKS * CHUNK // GROUP + 1
BAND = BAND_TILES * GROUP
LOG2_E = math.log2(math.e)
Q_SCALE = LOG2_E / math.sqrt(HEAD_DIM)
DIAG_LANES = -(-(GROUP + BAND - 1) // LANES) * LANES
VMEM_LIMIT = 56 * 1024 * 1024
RG_TILE = 512
W_STAGE_ROWS = 64
FFN_CHUNK = 3 * MXU_DIM
SEG_PITCH = RG_TILE // SUBLANES + SUBLANES // 2
COST_CONV, COST_GATE, COST_SCAN_STEP, COST_OUT = 16, 52, 23, 24


def _rms(x, g):
    var = jnp.mean(x * x, axis=-1, keepdims=True)
    return x * lax.rsqrt(var + EPS) * g


def _resident(shape):
    zeros = (0,) * len(shape)
    return pl.BlockSpec(shape, lambda *_: zeros, pipeline_mode=pl.Buffered(1))


def _load_rounded(src_hbm, dst_ref, stage_ref, sem):
    rows = stage_ref.shape[1]
    n_chunks = src_hbm.shape[0] // rows

    def chunk_copy(c):
        slot = c % 2
        return pltpu.make_async_copy(src_hbm.at[pl.ds(c * rows, rows), :], stage_ref.at[slot],
                                     sem.at[slot])

    chunk_copy(0).start()
    for c in range(n_chunks):
        if c + 1 < n_chunks:
            chunk_copy(c + 1).start()
        chunk_copy(c).wait()
        dst_ref[c * rows:(c + 1) * rows, :] = stage_ref[c % 2].astype(BF16)


def _inproj_kernel(*refs, tiles_per_seq, n_cast):
    (x_ref, g_ref, w_hbm, cw_ref, cb_ref, wr_ref, br_ref, wi_ref, bi_ref, lam_ref), refs = refs[:10], refs[10:]
    cast_in, refs = refs[:n_cast], refs[n_cast:]
    (qkv_ref, gates_ref, hb_ref), refs = refs[:3], refs[3:]
    cast_out, refs = refs[:n_cast], refs[n_cast:]
    (w_ref, stage_ref, stage_sem, xn_ref, gr_ref, xs_ref, xc_ref, rp_ref, ip_ref,
     a_ref, b_ref, hl_ref, pp_ref, h_ref) = refs

    @pl.when(pl.program_id(0) == 0)
    def _():
        _load_rounded(w_hbm, w_ref, stage_ref, stage_sem)

    for src_ref, dst_ref in zip(cast_in, cast_out):
        dst_ref[...] = src_ref[...].astype(BF16)

    @pl.when(pl.program_id(0) % tiles_per_seq == 0)
    def _():
        xs_ref[:, 0:SUBLANES, :] = jnp.zeros((D_MODEL // LANES, SUBLANES, LANES), F32)
        h_ref[...] = jnp.zeros_like(h_ref)

    xn_ref[...] = _rms(x_ref[...], g_ref[...]).astype(BF16)

    def u_cols(c0, width):
        return jnp.dot(xn_ref[...], w_ref[:, c0:c0 + width], preferred_element_type=F32)

    def unit(j, n):
        acc = u_cols(j * D_MODEL + n * MXU_DIM, MXU_DIM)
        if j == 0:
            acc = acc * Q_SCALE
        if j < 3:
            for k in range(MXU_DIM // HEAD_PAIR):
                hp = n * (MXU_DIM // HEAD_PAIR) + k
                qkv_ref[j * N_PAIRS + hp] = acc[:, k * HEAD_PAIR:(k + 1) * HEAD_PAIR].astype(BF16)
        elif j == 4:
            gr_ref[:, n * MXU_DIM:(n + 1) * MXU_DIM] = acc
        else:
            c0 = (j - 5) * D_MODEL + n * MXU_DIM
            gates_ref[:, c0:c0 + MXU_DIM] = acc.astype(BF16)

    units = [(j, n) for j in (4, 0, 1, 2, 5, 6) for n in range(D_MODEL // MXU_DIM)]
    n_pieces = SUBLANES * (D_MODEL // LANES)
    total_cost = n_pieces * (COST_CONV + COST_GATE + COST_OUT) + (x_ref.shape[0] // SUBLANES) * COST_SCAN_STEP
    cost_per_unit = total_cost / len(units)
    spent = [0.0]

    def side_work(cost):
        spent[0] += cost
        while units and spent[0] >= cost_per_unit:
            spent[0] -= cost_per_unit
            unit(*units.pop(0))

    _rglru_tile(u_cols(3 * D_MODEL, D_MODEL), gr_ref, side_work,
                cw_ref, cb_ref, wr_ref, br_ref, wi_ref, bi_ref, lam_ref,
                hb_ref, xs_ref, xc_ref, rp_ref, ip_ref, a_ref, b_ref, hl_ref, pp_ref, h_ref)
    while units:
        unit(*units.pop(0))


BF16_SUBLANES = 2 * SUBLANES


def _cast_slice_spec(rows, cols, n_steps):
    repeat = next(k for k in (1, 2, 4, 8)
                  if rows * k % n_steps == 0 and (rows * k // n_steps) % BF16_SUBLANES == 0)
    return pl.BlockSpec((rows * repeat // n_steps, cols), lambda i: (i // repeat, 0))


def _inproj(x2, g, w, cw, cb, wr, br, wi, bi, lam, later_weights, seq):
    t = x2.shape[0]
    tm = RG_TILE
    n_steps = t // tm
    n_diag = D_MODEL // MXU_DIM
    scan_scratch = pltpu.VMEM((D_MODEL // LANES, SUBLANES * SEG_PITCH, LANES), F32)
    slab_scratch = pltpu.VMEM((tm, MXU_DIM), F32)
    cast_specs = [_cast_slice_spec(*wt.shape, n_steps) for wt in later_weights]
    outs = pl.pallas_call(
        functools.partial(_inproj_kernel, tiles_per_seq=seq // tm, n_cast=len(later_weights)),
        grid=(n_steps,),
        in_specs=[
            pl.BlockSpec((tm, D_MODEL), lambda i: (i, 0)),
            _resident((1, D_MODEL)),
            pl.BlockSpec(memory_space=pl.ANY),
            _resident((CONV_W, D_MODEL)),
            _resident((1, D_MODEL)),
            _resident((n_diag, MXU_DIM, MXU_DIM)),
            _resident((1, D_MODEL)),
            _resident((n_diag, MXU_DIM, MXU_DIM)),
            _resident((1, D_MODEL)),
            _resident((1, D_MODEL)),
            *cast_specs,
        ],
        out_specs=[
            pl.BlockSpec((3 * N_PAIRS, tm, HEAD_PAIR), lambda i: (0, i, 0)),
            pl.BlockSpec((tm, 2 * D_MODEL), lambda i: (i, 0)),
            pl.BlockSpec((tm, D_MODEL), lambda i: (i, 0)),
            *cast_specs,
        ],
        out_shape=[
            jax.ShapeDtypeStruct((3 * N_PAIRS, t, HEAD_PAIR), BF16),
            jax.ShapeDtypeStruct((t, 2 * D_MODEL), BF16),
            jax.ShapeDtypeStruct((t, D_MODEL), BF16),
            *[jax.ShapeDtypeStruct(wt.shape, BF16) for wt in later_weights],
        ],
        scratch_shapes=[
            pltpu.VMEM(w.shape, BF16),
            pltpu.VMEM((2, W_STAGE_ROWS, w.shape[1]), F32),
            pltpu.SemaphoreType.DMA((2,)),
            pltpu.VMEM((tm, D_MODEL), BF16),
            pltpu.VMEM((tm, D_MODEL), F32),
            pltpu.VMEM((D_MODEL // LANES, tm + SUBLANES, LANES), F32),
            slab_scratch, slab_scratch, slab_scratch,
            scan_scratch, scan_scratch, scan_scratch, scan_scratch,
            pltpu.VMEM((SUBLANES, D_MODEL), F32),
        ],
        compiler_params=pltpu.CompilerParams(
            dimension_semantics=("arbitrary",), vmem_limit_bytes=VMEM_LIMIT),
        name="inproj_rglru",
    )(x2, g, w, cw, cb, wr, br, wi, bi, lam, *later_weights)
    return outs[0], outs[1], outs[2], outs[3:]


def _band_geometry(ntiles):
    kw = ntiles * GROUP
    return kw, BAND - kw


def _attn_scores(q_ref, k_ref, s_ref, r0, ntiles):
    kw, c_off = _band_geometry(ntiles)
    lane = lax.broadcasted_iota(jnp.int32, (GROUP, HEAD_PAIR), 1)
    first_head = lane < HEAD_DIM
    qp = q_ref[0, pl.ds(r0, GROUP), :]
    zero = jnp.zeros_like(qp)
    qq = jnp.concatenate([jnp.where(first_head, qp, zero),
                          jnp.where(first_head, zero, qp)], axis=0)
    kk = k_ref[0, pl.ds(r0 + GROUP - kw, kw), :]
    s_ref[:, c_off:] = lax.dot_general(qq, kk, (((1,), (1,)), ((), ())),
                                       preferred_element_type=F32)


def _attn_softmax(s_ref, bias_ref, p_ref, ntiles):
    _, c_off = _band_geometry(ntiles)
    rb = 32
    lane_rb = lax.broadcasted_iota(jnp.int32, (rb, LANES), 1)
    for hh in range(2):
        for i in range(GROUP // CHUNK):
            lo_col = max(i * CHUNK, c_off)
            hi_col = i * CHUNK + (LEFT_CHUNKS + 1) * CHUNK
            ws = (lo_col // LANES) * LANES
            we = -(-hi_col // LANES) * LANES
            nv = (we - ws) // LANES
            rows_all = slice(hh * GROUP + i * CHUNK, hh * GROUP + (i + 1) * CHUNK)
            if ws > c_off:
                p_ref[rows_all, c_off:ws] = jnp.zeros((CHUNK, ws - c_off), BF16)
            if we < BAND:
                p_ref[rows_all, we:BAND] = jnp.zeros((CHUNK, BAND - we), BF16)
            for b in range(CHUNK // rb):
                rows = slice(hh * GROUP + i * CHUNK + b * rb, hh * GROUP + i * CHUNK + (b + 1) * rb)
                brows = slice(i * CHUNK + b * rb, i * CHUNK + (b + 1) * rb)
                first_biased = brows.start + (BAND - GROUP) - MAX_REL + 1
                cols = []
                for c in range(nv):
                    tile = slice(ws + c * LANES, ws + (c + 1) * LANES)
                    col = s_ref[rows, tile]
                    if tile.stop > first_biased:
                        col = col + bias_ref[hh, brows, tile]
                    cols.append(col)
                if lo_col % LANES:
                    cols[0] = jnp.where(lane_rb >= HEAD_DIM, cols[0], NEG_INF)
                if hi_col % LANES:
                    cols[-1] = jnp.where(lane_rb < HEAD_DIM, cols[-1], NEG_INF)
                x = jnp.concatenate(cols, axis=1)
                m = jnp.max(x, axis=-1, keepdims=True)
                p_ref[rows, ws:we] = jnp.exp2(x - m).astype(BF16)


def _attn_values(p_ref, v_ref, o_ref, r0, ntiles):
    kw, c_off = _band_geometry(ntiles)
    lane = lax.broadcasted_iota(jnp.int32, (GROUP, HEAD_PAIR), 1)
    vv = v_ref[0, pl.ds(r0 + GROUP - kw, kw), :]
    v_ones = jnp.concatenate([vv, jnp.ones((kw, LANES), BF16)], axis=1)
    o2 = jnp.dot(p_ref[:, c_off:], v_ones, preferred_element_type=F32)
    o2 = o2[:, :HEAD_PAIR] / o2[:, HEAD_PAIR:]
    o = jnp.where(lane < HEAD_DIM, o2[:GROUP], o2[GROUP:])
    o_ref[0, pl.ds(r0, GROUP), :] = o.astype(BF16)


def _attn_kernel(q_ref, k_ref, v_ref, w_ref, o_ref, bias_ref, s0, s1, p0, p1):
    n_groups = q_ref.shape[1] // GROUP
    full = BAND_TILES
    row0 = lambda g: pl.multiple_of(g * GROUP, GROUP)

    @pl.when(pl.program_id(1) == 0)
    def _():
        for hh in range(2):
            diag = jnp.broadcast_to(w_ref[hh], (GROUP, DIAG_LANES))
            rolled = pltpu.roll(diag, DIAG_LANES - (GROUP - 1), 1, stride=1, stride_axis=0)
            bias_ref[hh] = rolled[:, :BAND]

    def scores(g, s_ref, ntiles=full):
        _attn_scores(q_ref, k_ref, s_ref, row0(g), ntiles)

    def softmax(s_ref, p_ref, ntiles=full):
        _attn_softmax(s_ref, bias_ref, p_ref, ntiles)

    def values(g, p_ref, ntiles=full):
        _attn_values(p_ref, v_ref, o_ref, row0(g), ntiles)

    bufs = ((s0, p0), (s1, p1))
    ntiles = lambda g: min(g + 1, full)
    for t in range(n_groups + 2):
        if t >= 2:
            values(t - 2, bufs[t % 2][1], ntiles(t - 2))
        if 1 <= t <= n_groups:
            softmax(*bufs[(t - 1) % 2], ntiles(t - 1))
        if t < n_groups:
            scores(t, bufs[t % 2][0], ntiles(t))


def _attention(qkv, diagonals, batch, seq):
    t = batch * seq
    return pl.pallas_call(
        _attn_kernel,
        grid=(N_PAIRS, batch),
        in_specs=[
            pl.BlockSpec((1, seq, HEAD_PAIR), lambda hp, b: (hp, b, 0)),
            pl.BlockSpec((1, seq, HEAD_PAIR), lambda hp, b: (N_PAIRS + hp, b, 0)),
            pl.BlockSpec((1, seq, HEAD_PAIR), lambda hp, b: (2 * N_PAIRS + hp, b, 0)),
            pl.BlockSpec((2, 1, DIAG_LANES), lambda hp, b: (hp, 0, 0)),
        ],
        out_specs=pl.BlockSpec((1, seq, HEAD_PAIR), lambda hp, b: (hp, b, 0)),
        out_shape=jax.ShapeDtypeStruct((N_PAIRS, t, HEAD_PAIR), BF16),
        scratch_shapes=[
            pltpu.VMEM((2, GROUP, BAND), F32),
            pltpu.VMEM((2 * GROUP, BAND), F32),
            pltpu.VMEM((2 * GROUP, BAND), F32),
            pltpu.VMEM((2 * GROUP, BAND), BF16),
            pltpu.VMEM((2 * GROUP, BAND), BF16),
        ],
        compiler_params=pltpu.CompilerParams(
            dimension_semantics=("parallel", "arbitrary"), vmem_limit_bytes=VMEM_LIMIT),
        name="band_attention",
    )(qkv, qkv, qkv, diagonals)


def _band_bias_diagonals(rel_table):
    j = jnp.arange(DIAG_LANES)
    dist = (GROUP - 1 - j) + (BAND - GROUP)
    w = rel_table[:, jnp.clip(dist, -MAX_REL, MAX_REL) + MAX_REL] - rel_table[:, 2 * MAX_REL:]
    return (w * LOG2_E)[:, None, :]


def _gelu_tanh(x):
    k = math.sqrt(2.0 / math.pi)
    inner = x * (k + (k * 0.044715) * (x * x))
    return x * (0.5 + 0.5 * jnp.tanh(inner))


def _rglru_tile(xr, gr_ref, side_work, cw_ref, cb_ref, wr_ref, br_ref, wi_ref, bi_ref, lam_ref,
                o_ref, xs_ref, xc_ref, rp_ref, ip_ref, a_ref, b_ref, hl_ref, pp_ref, h_ref):
    tm = xr.shape[0]
    seg = tm // SUBLANES
    n_lane_tiles = D_MODEL // LANES
    tiles_per_slab = MXU_DIM // LANES
    lane_tile = lambda c: slice(c * LANES, (c + 1) * LANES)

    for c in range(n_lane_tiles):
        xs_ref[c, SUBLANES:, :] = xr[:, lane_tile(c)]

    for d in range(D_MODEL // MXU_DIM):
        for k in range(tiles_per_slab):
            c = d * tiles_per_slab + k
            for j in range(SUBLANES):
                acc = cb_ref[:, lane_tile(c)]
                for tap in range(CONV_W):
                    off = SUBLANES - (CONV_W - 1) + tap + j * seg
                    acc = acc + xs_ref[c, off:off + seg, :] * cw_ref[tap:tap + 1, lane_tile(c)]
                xc_ref[j * seg:(j + 1) * seg, lane_tile(k)] = acc
                side_work(COST_CONV)
            xs_ref[c, 0:SUBLANES, :] = xs_ref[c, tm:tm + SUBLANES, :]
        xcb = xc_ref[...].astype(BF16)
        rp_ref[...] = jnp.dot(xcb, wr_ref[d], preferred_element_type=F32)
        ip_ref[...] = jnp.dot(xcb, wi_ref[d], preferred_element_type=F32)
        for k in range(tiles_per_slab):
            c = d * tiles_per_slab + k
            nlam = -lam_ref[:, lane_tile(c)]
            softplus = jnp.maximum(nlam, 0.0) + jnp.log1p(jnp.exp(-jnp.abs(nlam)))
            log_a_scale = (-RG_C) * softplus
            for j in range(SUBLANES):
                rows = slice(j * seg, (j + 1) * seg)
                xc = xc_ref[rows, lane_tile(k)]
                r = jax.nn.sigmoid(rp_ref[rows, lane_tile(k)] + br_ref[:, lane_tile(c)])
                ig = jax.nn.sigmoid(ip_ref[rows, lane_tile(k)] + bi_ref[:, lane_tile(c)])
                log_a = r * log_a_scale
                a = jnp.exp(log_a)
                y = jnp.tanh(log_a) * (-1.0 - a * a)
                mult = jnp.where(y > 0.0, y * lax.rsqrt(y), 0.0)
                srows = slice(j * SEG_PITCH, j * SEG_PITCH + seg)
                a_ref[c, srows, :] = a
                b_ref[c, srows, :] = mult * (ig * xc)
                side_work(COST_GATE)

    hs = [jnp.zeros((SUBLANES, LANES), F32) for _ in range(n_lane_tiles)]
    ps = [jnp.ones((SUBLANES, LANES), F32) for _ in range(n_lane_tiles)]
    for t in range(seg):
        idx = pl.ds(t, SUBLANES, stride=SEG_PITCH)
        for c in range(n_lane_tiles):
            a_t = a_ref[c, idx, :]
            hs[c] = a_t * hs[c] + b_ref[c, idx, :]
            ps[c] = a_t * ps[c]
            hl_ref[c, idx, :] = hs[c]
            pp_ref[c, idx, :] = ps[c]
        side_work(COST_SCAN_STEP)
    h_end = jnp.concatenate(hs, axis=1)
    p_end = jnp.concatenate(ps, axis=1)

    h_in = h_ref[0:1, :]
    for j in range(SUBLANES):
        rows = slice(j * seg, (j + 1) * seg)
        srows = slice(j * SEG_PITCH, j * SEG_PITCH + seg)
        for c in range(n_lane_tiles):
            hj = hl_ref[c, srows, :] + pp_ref[c, srows, :] * h_in[:, lane_tile(c)]
            o_ref[rows, lane_tile(c)] = (hj * _gelu_tanh(gr_ref[rows, lane_tile(c)])).astype(BF16)
            side_work(COST_OUT)
        h_in = p_end[j:j + 1, :] * h_in + h_end[j:j + 1, :]
    h_ref[0:1, :] = h_in


def _block_diag_tiles(w):
    per = MXU_DIM // RG_BLOCK
    n_diag = D_MODEL // MXU_DIM
    w4 = w.reshape(n_diag, per, RG_BLOCK, RG_BLOCK)
    eye = jnp.eye(per, dtype=w.dtype)
    tiles = jnp.einsum('dpij,pq->dpiqj', w4, eye)
    return tiles.reshape(n_diag, MXU_DIM, MXU_DIM)


def _mixffn_kernel(x_ref, at_ref, hb_ref, ga_ref, gb_ref, bm_ref, wa_ref, wb_ref, wo_ref,
                   gn_ref, win_ref, wout_ref, gf_ref, o_ref, *, splits):
    attn = jnp.concatenate([at_ref[hp] for hp in range(N_PAIRS)], axis=1)
    ya = jnp.dot(attn, wa_ref[...], preferred_element_type=F32)
    yb = jnp.dot(hb_ref[...], wb_ref[...], preferred_element_type=F32)
    g_a = jax.nn.sigmoid(ga_ref[...].astype(F32) + bm_ref[:, :D_MODEL])
    g_b = jax.nn.sigmoid(gb_ref[...].astype(F32) + bm_ref[:, D_MODEL:])
    mix = (g_a * ya + g_b * yb).astype(BF16)
    h = x_ref[...] + jnp.dot(mix, wo_ref[...], preferred_element_type=F32)
    hn = _rms(h, gn_ref[...]).astype(BF16)
    d_ff = wout_ref.shape[0]
    acc = h
    for c0, c1 in splits:
        g = jnp.dot(hn, win_ref[:, c0:c1], preferred_element_type=F32)
        up = jnp.dot(hn, win_ref[:, d_ff + c0:d_ff + c1], preferred_element_type=F32)
        act = (jax.nn.silu(g) * up).astype(BF16)
        acc = acc + jnp.dot(act, wout_ref[c0:c1, :], preferred_element_type=F32)
    o_ref[...] = _rms(acc, gf_ref[...])


def _mixffn(x2, attn, hb, gates, bm, wa, wb, wo, gn, win, wout, gf, tm):
    t = x2.shape[0]
    d_ff = wout.shape[0]
    edges = list(range(0, d_ff, FFN_CHUNK)) + [d_ff]
    splits = tuple(zip(edges[:-1], edges[1:]))
    row = lambda i: (i, 0)
    return pl.pallas_call(
        functools.partial(_mixffn_kernel, splits=splits),
        grid=(t // tm,),
        in_specs=[
            pl.BlockSpec((tm, D_MODEL), row),
            pl.BlockSpec((N_PAIRS, tm, HEAD_PAIR), lambda i: (0, i, 0)),
            pl.BlockSpec((tm, D_MODEL), row),
            pl.BlockSpec((tm, D_MODEL), lambda i: (i, 0)),
            pl.BlockSpec((tm, D_MODEL), lambda i: (i, 1)),
            _resident((1, 2 * D_MODEL)),
            _resident((D_MODEL, D_MODEL)),
            _resident((D_MODEL, D_MODEL)),
            _resident((D_MODEL, D_MODEL)),
            _resident((1, D_MODEL)),
            _resident((D_MODEL, 2 * d_ff)),
            _resident((d_ff, D_MODEL)),
            _resident((1, D_MODEL)),
        ],
        out_specs=pl.BlockSpec((tm, D_MODEL), row),
        out_shape=jax.ShapeDtypeStruct((t, D_MODEL), F32),
        compiler_params=pltpu.CompilerParams(
            dimension_semantics=("parallel",), vmem_limit_bytes=VMEM_LIMIT),
        name="mix_ffn",
    )(x2, attn, hb, gates, gates, bm, wa, wb, wo, gn, win, wout, gf)


def kernel(x, norm_mix_g, w_in, b_merge, rel_table, w_attn_out, conv_w, conv_b, w_rg_r, b_rg_r,
           w_rg_i, b_rg_i, rg_lambda, w_rnn_out, w_o, norm_ffn_g, w_ffn_in, w_ffn_out, final_norm_g):
    batch, seq, d = x.shape
    assert d == D_MODEL and seq % GROUP == 0 and seq // GROUP >= BAND_TILES
    assert w_in.shape[0] == 1, "single-layer block"
    x2 = x.reshape(batch * seq, d)
    row = lambda v: v.reshape(1, -1)

    assert seq % RG_TILE == 0
    later_weights = (w_attn_out[0], w_rnn_out[0], w_o[0], w_ffn_in[0], w_ffn_out[0])
    qkv, gates, hb, (wa, wb, wo, wfi, wfo) = _inproj(
        x2, row(norm_mix_g[0]), w_in[0], conv_w[0], row(conv_b[0]),
        _block_diag_tiles(w_rg_r[0]).astype(BF16), row(b_rg_r[0]),
        _block_diag_tiles(w_rg_i[0]).astype(BF16), row(b_rg_i[0]), row(rg_lambda[0]),
        later_weights, seq)
    attn = _attention(qkv, _band_bias_diagonals(rel_table[0]), batch, seq)
    out = _mixffn(x2, attn, hb, gates, row(b_merge[0]), wa, wb, wo, row(norm_ffn_g[0]), wfi, wfo,
                  row(final_norm_g), tm=512)
    return out.reshape(batch, seq, d)
```

```python
import functools
import math

import jax
import jax.numpy as jnp
from jax import lax
from jax.experimental import pallas as pl
from jax.experimental.pallas import tpu as pltpu

F32 = jnp.float32
BF16 = jnp.bfloat16

D_MODEL = 1024
CHUNK = 64
LEFT_CHUNKS = 8
N_HEADS = 16
HEAD_DIM = 64
MAX_REL = 128
N_RG_BLOCKS = 16
RG_BLOCK = D_MODEL // N_RG_BLOCKS
CONV_W = 4
RG_C = 8.0
EPS = 1e-6
NEG_INF = -1e30

LANES = 128
SUBLANES = 8
MXU_DIM = 256
HEAD_PAIR = 2 * HEAD_DIM
N_PAIRS = N_HEADS // 2
GROUP = 4 * CHUNK
BAND_TILES = LEFT_CHUNKS * CHUNK // GROUP + 1
BAND = BAND_TILES * GROUP
LOG2_E = math.log2(math.e)
Q_SCALE = LOG2_E / math.sqrt(HEAD_DIM)
DIAG_LANES = -(-(GROUP + BAND - 1) // LANES) * LANES
VMEM_LIMIT = 56 * 1024 * 1024
RG_TILE = 512
W_STAGE_ROWS = 64
FFN_CHUNK = 3 * MXU_DIM
SEG_PITCH = RG_TILE // SUBLANES + SUBLANES // 2
COST_CONV, COST_GATE, COST_SCAN_STEP, COST_OUT = 16, 52, 23, 24


def _rms(x, g):
    var = jnp.mean(x * x, axis=-1, keepdims=True)
    return x * lax.rsqrt(var + EPS) * g


def _resident(shape):
    zeros = (0,) * len(shape)
    return pl.BlockSpec(shape, lambda *_: zeros, pipeline_mode=pl.Buffered(1))


def _load_rounded(src_hbm, dst_ref, stage_ref, sem):
    rows = stage_ref.shape[1]
    n_chunks = src_hbm.shape[0] // rows

    def chunk_copy(c):
        slot = c % 2
        return pltpu.make_async_copy(src_hbm.at[pl.ds(c * rows, rows), :], stage_ref.at[slot],
                                     sem.at[slot])

    chunk_copy(0).start()
    for c in range(n_chunks):
        if c + 1 < n_chunks:
            chunk_copy(c + 1).start()
        chunk_copy(c).wait()
        dst_ref[c * rows:(c + 1) * rows, :] = stage_ref[c % 2].astype(BF16)


def _inproj_kernel(*refs, tiles_per_seq, n_cast):
    (x_ref, g_ref, w_hbm, cw_ref, cb_ref, wr_ref, br_ref, wi_ref, bi_ref, lam_ref), refs = refs[:10], refs[10:]
    cast_in, refs = refs[:n_cast], refs[n_cast:]
    (qkv_ref, gates_ref, hb_ref), refs = refs[:3], refs[3:]
    cast_out, refs = refs[:n_cast], refs[n_cast:]
    (w_ref, stage_ref, stage_sem, xn_ref, gr_ref, xs_ref, xc_ref, rp_ref, ip_ref,
     a_ref, b_ref, hl_ref, pp_ref, h_ref) = refs

    @pl.when(pl.program_id(0) == 0)
    def _():
        _load_rounded(w_hbm, w_ref, stage_ref, stage_sem)

    @pl.when(pl.program_id(0) % tiles_per_seq == 0)
    def _():
        xs_ref[:, 0:SUBLANES, :] = jnp.zeros((D_MODEL // LANES, SUBLANES, LANES), F32)
        h_ref[...] = jnp.zeros_like(h_ref)

    xn_ref[...] = _rms(x_ref[...], g_ref[...]).astype(BF16)

    def u_cols(c0, width):
        return jnp.dot(xn_ref[...], w_ref[:, c0:c0 + width], preferred_element_type=F32)

    def unit(j, n):
        acc = u_cols(j * D_MODEL + n * MXU_DIM, MXU_DIM)
        if j == 0:
            acc = acc * Q_SCALE
        if j < 3:
            for k in range(MXU_DIM // HEAD_PAIR):
                hp = n * (MXU_DIM // HEAD_PAIR) + k
                qkv_ref[j * N_PAIRS + hp] = acc[:, k * HEAD_PAIR:(k + 1) * HEAD_PAIR].astype(BF16)
        elif j == 4:
            gr_ref[:, n * MXU_DIM:(n + 1) * MXU_DIM] = acc
        else:
            c0 = (j - 5) * D_MODEL + n * MXU_DIM
            gates_ref[:, c0:c0 + MXU_DIM] = acc.astype(BF16)

    units = [(j, n) for j in (4, 0, 1, 2, 5, 6) for n in range(D_MODEL // MXU_DIM)]
    n_pieces = SUBLANES * (D_MODEL // LANES)
    total_cost = n_pieces * (COST_CONV + COST_GATE + COST_OUT) + (x_ref.shape[0] // SUBLANES) * COST_SCAN_STEP
    cost_per_unit = total_cost / len(units)
    spent = [0.0]

    def side_work(cost):
        spent[0] += cost
        while units and spent[0] >= cost_per_unit:
            spent[0] -= cost_per_unit
            unit(*units.pop(0))

    _rglru_tile(u_cols(3 * D_MODEL, D_MODEL), gr_ref, side_work,
                cw_ref, cb_ref, wr_ref, br_ref, wi_ref, bi_ref, lam_ref,
                hb_ref, xs_ref, xc_ref, rp_ref, ip_ref, a_ref, b_ref, hl_ref, pp_ref, h_ref)
    for src_ref, dst_ref in zip(cast_in, cast_out):
        dst_ref[...] = src_ref[...].astype(BF16)
    while units:
        unit(*units.pop(0))


BF16_SUBLANES = 2 * SUBLANES


def _cast_slice_spec(rows, cols, n_steps):
    repeat = next(k for k in (1, 2, 4, 8)
                  if rows * k % n_steps == 0 and (rows * k // n_steps) % BF16_SUBLANES == 0)
    return pl.BlockSpec((rows * repeat // n_steps, cols), lambda i: (i // repeat, 0))


def _inproj(x2, g, w, cw, cb, wr, br, wi, bi, lam, later_weights, seq):
    t = x2.shape[0]
    tm = RG_TILE
    n_steps = t // tm
    n_diag = D_MODEL // MXU_DIM
    scan_scratch = pltpu.VMEM((D_MODEL // LANES, SUBLANES * SEG_PITCH, LANES), F32)
    slab_scratch = pltpu.VMEM((tm, MXU_DIM), F32)
    cast_specs = [_cast_slice_spec(*wt.shape, n_steps) for wt in later_weights]
    outs = pl.pallas_call(
        functools.partial(_inproj_kernel, tiles_per_seq=seq // tm, n_cast=len(later_weights)),
        grid=(n_steps,),
        in_specs=[
            pl.BlockSpec((tm, D_MODEL), lambda i: (i, 0)),
            _resident((1, D_MODEL)),
            pl.BlockSpec(memory_space=pl.ANY),
            _resident((CONV_W, D_MODEL)),
            _resident((1, D_MODEL)),
            _resident((n_diag, MXU_DIM, MXU_DIM)),
            _resident((1, D_MODEL)),
            _resident((n_diag, MXU_DIM, MXU_DIM)),
            _resident((1, D_MODEL)),
            _resident((1, D_MODEL)),
            *cast_specs,
        ],
        out_specs=[
            pl.BlockSpec((3 * N_PAIRS, tm, HEAD_PAIR), lambda i: (0, i, 0)),
            pl.BlockSpec((tm, 2 * D_MODEL), lambda i: (i, 0)),
            pl.BlockSpec((tm, D_MODEL), lambda i: (i, 0)),
            *cast_specs,
        ],
        out_shape=[
            jax.ShapeDtypeStruct((3 * N_PAIRS, t, HEAD_PAIR), BF16),
            jax.ShapeDtypeStruct((t, 2 * D_MODEL), BF16),
            jax.ShapeDtypeStruct((t, D_MODEL), BF16),
            *[jax.ShapeDtypeStruct(wt.shape, BF16) for wt in later_weights],
        ],
        scratch_shapes=[
            pltpu.VMEM(w.shape, BF16),
            pltpu.VMEM((2, W_STAGE_ROWS, w.shape[1]), F32),
            pltpu.SemaphoreType.DMA((2,)),
            pltpu.VMEM((tm, D_MODEL), BF16),
            pltpu.VMEM((tm, D_MODEL), F32),
            pltpu.VMEM((D_MODEL // LANES, tm + SUBLANES, LANES), F32),
            slab_scratch, slab_scratch, slab_scratch,
            scan_scratch, scan_scratch, scan_scratch, scan_scratch,
            pltpu.VMEM((SUBLANES, D_MODEL), F32),
        ],
        compiler_params=pltpu.CompilerParams(
            dimension_semantics=("arbitrary",), vmem_limit_bytes=VMEM_LIMIT),
        name="inproj_rglru",
    )(x2, g, w, cw, cb, wr, br, wi, bi, lam, *later_weights)
    return outs[0], outs[1], outs[2], outs[3:]


def _band_geometry(ntiles):
    kw = ntiles * GROUP
    return kw, BAND - kw


def _attn_scores(q_ref, k_ref, s_ref, r0, ntiles):
    kw, c_off = _band_geometry(ntiles)
    lane = lax.broadcasted_iota(jnp.int32, (GROUP, HEAD_PAIR), 1)
    first_head = lane < HEAD_DIM
    qp = q_ref[0, pl.ds(r0, GROUP), :]
    zero = jnp.zeros_like(qp)
    qq = jnp.concatenate([jnp.where(first_head, qp, zero),
                          jnp.where(first_head, zero, qp)], axis=0)
    kk = k_ref[0, pl.ds(r0 + GROUP - kw, kw), :]
    s_ref[:, c_off:] = lax.dot_general(qq, kk, (((1,), (1,)), ((), ())),
                                       preferred_element_type=F32)


def _attn_softmax(s_ref, bias_ref, p_ref, ntiles):
    _, c_off = _band_geometry(ntiles)
    rb = 32
    lane_rb = lax.broadcasted_iota(jnp.int32, (rb, LANES), 1)
    for hh in range(2):
        for i in range(GROUP // CHUNK):
            lo_col = max(i * CHUNK, c_off)
            hi_col = i * CHUNK + (LEFT_CHUNKS + 1) * CHUNK
            ws = (lo_col // LANES) * LANES
            we = -(-hi_col // LANES) * LANES
            nv = (we - ws) // LANES
            rows_all = slice(hh * GROUP + i * CHUNK, hh * GROUP + (i + 1) * CHUNK)
            if ws > c_off:
                p_ref[rows_all, c_off:ws] = jnp.zeros((CHUNK, ws - c_off), BF16)
            if we < BAND:
                p_ref[rows_all, we:BAND] = jnp.zeros((CHUNK, BAND - we), BF16)
            for b in range(CHUNK // rb):
                rows = slice(hh * GROUP + i * CHUNK + b * rb, hh * GROUP + i * CHUNK + (b + 1) * rb)
                brows = slice(i * CHUNK + b * rb, i * CHUNK + (b + 1) * rb)
                first_biased = brows.start + (BAND - GROUP) - MAX_REL + 1
                cols = []
                for c in range(nv):
                    tile = slice(ws + c * LANES, ws + (c + 1) * LANES)
                    col = s_ref[rows, tile]
                    if tile.stop > first_biased:
                        col = col + bias_ref[hh, brows, tile]
                    cols.append(col)
                if lo_col % LANES:
                    cols[0] = jnp.where(lane_rb >= HEAD_DIM, cols[0], NEG_INF)
                if hi_col % LANES:
                    cols[-1] = jnp.where(lane_rb < HEAD_DIM, cols[-1], NEG_INF)
                x = jnp.concatenate(cols, axis=1)
                m = jnp.max(x, axis=-1, keepdims=True)
                p_ref[rows, ws:we] = jnp.exp2(x - m).astype(BF16)


def _attn_values(p_ref, v_ref, o_ref, r0, ntiles):
    kw, c_off = _band_geometry(ntiles)
    lane = lax.broadcasted_iota(jnp.int32, (GROUP, HEAD_PAIR), 1)
    vv = v_ref[0, pl.ds(r0 + GROUP - kw, kw), :]
    v_ones = jnp.concatenate([vv, jnp.ones((kw, LANES), BF16)], axis=1)
    o2 = jnp.dot(p_ref[:, c_off:], v_ones, preferred_element_type=F32)
    o2 = o2[:, :HEAD_PAIR] / o2[:, HEAD_PAIR:]
    o = jnp.where(lane < HEAD_DIM, o2[:GROUP], o2[GROUP:])
    o_ref[0, pl.ds(r0, GROUP), :] = o.astype(BF16)


def _attn_kernel(q_ref, k_ref, v_ref, w_ref, o_ref, bias_ref, s0, s1, p0, p1):
    n_groups = q_ref.shape[1] // GROUP
    full = BAND_TILES
    row0 = lambda g: pl.multiple_of(g * GROUP, GROUP)

    @pl.when(pl.program_id(1) == 0)
    def _():
        for hh in range(2):
            diag = jnp.broadcast_to(w_ref[hh], (GROUP, DIAG_LANES))
            rolled = pltpu.roll(diag, DIAG_LANES - (GROUP - 1), 1, stride=1, stride_axis=0)
            bias_ref[hh] = rolled[:, :BAND]

    def scores(g, s_ref, ntiles=full):
        _attn_scores(q_ref, k_ref, s_ref, row0(g), ntiles)

    def softmax(s_ref, p_ref, ntiles=full):
        _attn_softmax(s_ref, bias_ref, p_ref, ntiles)

    def values(g, p_ref, ntiles=full):
        _attn_values(p_ref, v_ref, o_ref, row0(g), ntiles)

    bufs = ((s0, p0), (s1, p1))
    ntiles = lambda g: min(g + 1, full)
    for t in range(n_groups + 2):
        if t >= 2:
            values(t - 2, bufs[t % 2][1], ntiles(t - 2))
        if 1 <= t <= n_groups:
            softmax(*bufs[(t - 1) % 2], ntiles(t - 1))
        if t < n_groups:
            scores(t, bufs[t % 2][0], ntiles(t))


def _attention(qkv, diagonals, batch, seq):
    t = batch * seq
    return pl.pallas_call(
        _attn_kernel,
        grid=(N_PAIRS, batch),
        in_specs=[
            pl.BlockSpec((1, seq, HEAD_PAIR), lambda hp, b: (hp, b, 0)),
            pl.BlockSpec((1, seq, HEAD_PAIR), lambda hp, b: (N_PAIRS + hp, b, 0)),
            pl.BlockSpec((1, seq, HEAD_PAIR), lambda hp, b: (2 * N_PAIRS + hp, b, 0)),
            pl.BlockSpec((2, 1, DIAG_LANES), lambda hp, b: (hp, 0, 0)),
        ],
        out_specs=pl.BlockSpec((1, seq, HEAD_PAIR), lambda hp, b: (hp, b, 0)),
        out_shape=jax.ShapeDtypeStruct((N_PAIRS, t, HEAD_PAIR), BF16),
        scratch_shapes=[
            pltpu.VMEM((2, GROUP, BAND), F32),
            pltpu.VMEM((2 * GROUP, BAND), F32),
            pltpu.VMEM((2 * GROUP, BAND), F32),
            pltpu.VMEM((2 * GROUP, BAND), BF16),
            pltpu.VMEM((2 * GROUP, BAND), BF16),
        ],
        compiler_params=pltpu.CompilerParams(
            dimension_semantics=("parallel", "arbitrary"), vmem_limit_bytes=VMEM_LIMIT),
        name="band_attention",
    )(qkv, qkv, qkv, diagonals)


def _band_bias_diagonals(rel_table):
    j = jnp.arange(DIAG_LANES)
    dist = (GROUP - 1 - j) + (BAND - GROUP)
    w = rel_table[:, jnp.clip(dist, -MAX_REL, MAX_REL) + MAX_REL] - rel_table[:, 2 * MAX_REL:]
    return (w * LOG2_E)[:, None, :]


def _gelu_tanh(x):
    k = math.sqrt(2.0 / math.pi)
    inner = x * (k + (k * 0.044715) * (x * x))
    return x * (0.5 + 0.5 * jnp.tanh(inner))


def _rglru_tile(xr, gr_ref, side_work, cw_ref, cb_ref, wr_ref, br_ref, wi_ref, bi_ref, lam_ref,
                o_ref, xs_ref, xc_ref, rp_ref, ip_ref, a_ref, b_ref, hl_ref, pp_ref, h_ref):
    tm = xr.shape[0]
    seg = tm // SUBLANES
    n_lane_tiles = D_MODEL // LANES
    tiles_per_slab = MXU_DIM // LANES
    lane_tile = lambda c: slice(c * LANES, (c + 1) * LANES)

    for c in range(n_lane_tiles):
        xs_ref[c, SUBLANES:, :] = xr[:, lane_tile(c)]

    for d in range(D_MODEL // MXU_DIM):
        for k in range(tiles_per_slab):
            c = d * tiles_per_slab + k
            for j in range(SUBLANES):
                acc = cb_ref[:, lane_tile(c)]
                for tap in range(CONV_W):
                    off = SUBLANES - (CONV_W - 1) + tap + j * seg
                    acc = acc + xs_ref[c, off:off + seg, :] * cw_ref[tap:tap + 1, lane_tile(c)]
                xc_ref[j * seg:(j + 1) * seg, lane_tile(k)] = acc
                side_work(COST_CONV)
            xs_ref[c, 0:SUBLANES, :] = xs_ref[c, tm:tm + SUBLANES, :]
        xcb = xc_ref[...].astype(BF16)
        rp_ref[...] = jnp.dot(xcb, wr_ref[d], preferred_element_type=F32)
        ip_ref[...] = jnp.dot(xcb, wi_ref[d], preferred_element_type=F32)
        for k in range(tiles_per_slab):
            c = d * tiles_per_slab + k
            nlam = -lam_ref[:, lane_tile(c)]
            softplus = jnp.maximum(nlam, 0.0) + jnp.log1p(jnp.exp(-jnp.abs(nlam)))
            log_a_scale = (-RG_C) * softplus
            for j in range(SUBLANES):
                rows = slice(j * seg, (j + 1) * seg)
                xc = xc_ref[rows, lane_tile(k)]
                r = jax.nn.sigmoid(rp_ref[rows, lane_tile(k)] + br_ref[:, lane_tile(c)])
                ig = jax.nn.sigmoid(ip_ref[rows, lane_tile(k)] + bi_ref[:, lane_tile(c)])
                log_a = r * log_a_scale
                a = jnp.exp(log_a)
                y = jnp.tanh(log_a) * (-1.0 - a * a)
                mult = jnp.where(y > 0.0, y * lax.rsqrt(y), 0.0)
                srows = slice(j * SEG_PITCH, j * SEG_PITCH + seg)
                a_ref[c, srows, :] = a
                b_ref[c, srows, :] = mult * (ig * xc)
                side_work(COST_GATE)

    hs = [jnp.zeros((SUBLANES, LANES), F32) for _ in range(n_lane_tiles)]
    ps = [jnp.ones((SUBLANES, LANES), F32) for _ in range(n_lane_tiles)]
    for t in range(seg):
        idx = pl.ds(t, SUBLANES, stride=SEG_PITCH)
        for c in range(n_lane_tiles):
            a_t = a_ref[c, idx, :]
            hs[c] = a_t * hs[c] + b_ref[c, idx, :]
            ps[c] = a_t * ps[c]
            hl_ref[c, idx, :] = hs[c]
            pp_ref[c, idx, :] = ps[c]
        side_work(COST_SCAN_STEP)
    h_end = jnp.concatenate(hs, axis=1)
    p_end = jnp.concatenate(ps, axis=1)

    h_in = h_ref[0:1, :]
    for j in range(SUBLANES):
        rows = slice(j * seg, (j + 1) * seg)
        srows = slice(j * SEG_PITCH, j * SEG_PITCH + seg)
        for c in range(n_lane_tiles):
            hj = hl_ref[c, srows, :] + pp_ref[c, srows, :] * h_in[:, lane_tile(c)]
            o_ref[rows, lane_tile(c)] = (hj * _gelu_tanh(gr_ref[rows, lane_tile(c)])).astype(BF16)
            side_work(COST_OUT)
        h_in = p_end[j:j + 1, :] * h_in + h_end[j:j + 1, :]
    h_ref[0:1, :] = h_in


def _block_diag_tiles(w):
    per = MXU_DIM // RG_BLOCK
    n_diag = D_MODEL // MXU_DIM
    w4 = w.reshape(n_diag, per, RG_BLOCK, RG_BLOCK)
    eye = jnp.eye(per, dtype=w.dtype)
    tiles = jnp.einsum('dpij,pq->dpiqj', w4, eye)
    return tiles.reshape(n_diag, MXU_DIM, MXU_DIM)


def _mixffn_kernel(x_ref, at_ref, hb_ref, ga_ref, gb_ref, bm_ref, wa_ref, wb_ref, wo_ref,
                   gn_ref, win_ref, wout_ref, gf_ref, o_ref, *, splits):
    attn = jnp.concatenate([at_ref[hp] for hp in range(N_PAIRS)], axis=1)
    ya = jnp.dot(attn, wa_ref[...], preferred_element_type=F32)
    yb = jnp.dot(hb_ref[...], wb_ref[...], preferred_element_type=F32)
    g_a = jax.nn.sigmoid(ga_ref[...].astype(F32) + bm_ref[:, :D_MODEL])
    g_b = jax.nn.sigmoid(gb_ref[...].astype(F32) + bm_ref[:, D_MODEL:])
    mix = (g_a * ya + g_b * yb).astype(BF16)
    h = x_ref[...] + jnp.dot(mix, wo_ref[...], preferred_element_type=F32)
    hn = _rms(h, gn_ref[...]).astype(BF16)
    d_ff = wout_ref.shape[0]
    acc = h
    for c0, c1 in splits:
        g = jnp.dot(hn, win_ref[:, c0:c1], preferred_element_type=F32)
        up = jnp.dot(hn, win_ref[:, d_ff + c0:d_ff + c1], preferred_element_type=F32)
        act = (jax.nn.silu(g) * up).astype(BF16)
        acc = acc + jnp.dot(act, wout_ref[c0:c1, :], preferred_element_type=F32)
    o_ref[...] = _rms(acc, gf_ref[...])


def _mixffn(x2, attn, hb, gates, bm, wa, wb, wo, gn, win, wout, gf, tm):
    t = x2.shape[0]
    d_ff = wout.shape[0]
    edges = list(range(0, d_ff, FFN_CHUNK)) + [d_ff]
    splits = tuple(zip(edges[:-1], edges[1:]))
    row = lambda i: (i, 0)
    return pl.pallas_call(
        functools.partial(_mixffn_kernel, splits=splits),
        grid=(t // tm,),
        in_specs=[
            pl.BlockSpec((tm, D_MODEL), row),
            pl.BlockSpec((N_PAIRS, tm, HEAD_PAIR), lambda i: (0, i, 0)),
            pl.BlockSpec((tm, D_MODEL), row),
            pl.BlockSpec((tm, D_MODEL), lambda i: (i, 0)),
            pl.BlockSpec((tm, D_MODEL), lambda i: (i, 1)),
            _resident((1, 2 * D_MODEL)),
            _resident((D_MODEL, D_MODEL)),
            _resident((D_MODEL, D_MODEL)),
            _resident((D_MODEL, D_MODEL)),
            _resident((1, D_MODEL)),
            _resident((D_MODEL, 2 * d_ff)),
            _resident((d_ff, D_MODEL)),
            _resident((1, D_MODEL)),
        ],
        out_specs=pl.BlockSpec((tm, D_MODEL), row),
        out_shape=jax.ShapeDtypeStruct((t, D_MODEL), F32),
        compiler_params=pltpu.CompilerParams(
            dimension_semantics=("parallel",), vmem_limit_bytes=VMEM_LIMIT),
        name="mix_ffn",
    )(x2, attn, hb, gates, gates, bm, wa, wb, wo, gn, win, wout, gf)


def kernel(x, norm_mix_g, w_in, b_merge, rel_table, w_attn_out, conv_w, conv_b, w_rg_r, b_rg_r,
           w_rg_i, b_rg_i, rg_lambda, w_rnn_out, w_o, norm_ffn_g, w_ffn_in, w_ffn_out, final_norm_g):
    batch, seq, d = x.shape
    assert d == D_MODEL and seq % GROUP == 0 and seq // GROUP >= BAND_TILES
    assert w_in.shape[0] == 1, "single-layer block"
    x2 = x.reshape(batch * seq, d)
    row = lambda v: v.reshape(1, -1)

    assert seq % RG_TILE == 0
    later_weights = (w_attn_out[0], w_rnn_out[0], w_o[0], w_ffn_in[0], w_ffn_out[0])
    qkv, gates, hb, (wa, wb, wo, wfi, wfo) = _inproj(
        x2, row(norm_mix_g[0]), w_in[0], conv_w[0], row(conv_b[0]),
        _block_diag_tiles(w_rg_r[0]).astype(BF16), row(b_rg_r[0]),
        _block_diag_tiles(w_rg_i[0]).astype(BF16), row(b_rg_i[0]), row(rg_lambda[0]),
        later_weights, seq)
    attn = _attention(qkv, _band_bias_diagonals(rel_table[0]), batch, seq)
    out = _mixffn(x2, attn, hb, gates, row(b_merge[0]), wa, wb, wo, row(norm_ffn_g[0]), wfi, wfo,
                  row(final_norm_g), tm=512)
    return out.reshape(batch, seq, d)
```

```python
import functools
import math

import jax
import jax.numpy as jnp
from jax import lax
from jax.experimental import pallas as pl
from jax.experimental.pallas import tpu as pltpu

F32 = jnp.float32
BF16 = jnp.bfloat16

D_MODEL = 1024
CHUNK = 64
LEFT_CHUNKS = 8
N_HEADS = 16
HEAD_DIM = 64
MAX_REL = 128
N_RG_BLOCKS = 16
RG_BLOCK = D_MODEL // N_RG_BLOCKS
CONV_W = 4
RG_C = 8.0
EPS = 1e-6
NEG_INF = -1e30

LANES = 128
SUBLANES = 8
MXU_DIM = 256
HEAD_PAIR = 2 * HEAD_DIM
N_PAIRS = N_HEADS // 2
GROUP = 4 * CHUNK
BAND_TILES = LEFT_CHUNKS * CHUNK // GROUP + 1
BAND = BAND_TILES * GROUP
LOG2_E = math.log2(math.e)
Q_SCALE = LOG2_E / math.sqrt(HEAD_DIM)
DIAG_LANES = -(-(GROUP + BAND - 1) // LANES) * LANES
VMEM_LIMIT = 56 * 1024 * 1024
RG_TILE = 512
W_STAGE_ROWS = 64
FFN_CHUNK = 3 * MXU_DIM
SEG_PITCH = RG_TILE // SUBLANES + SUBLANES // 2
COST_CONV, COST_GATE, COST_SCAN_STEP, COST_OUT = 16, 52, 23, 24


def _rms(x, g):
    var = jnp.mean(x * x, axis=-1, keepdims=True)
    return x * lax.rsqrt(var + EPS) * g


def _resident(shape, single_buffer=False):
    zeros = (0,) * len(shape)
    if single_buffer:
        return pl.BlockSpec(shape, lambda *_: zeros, pipeline_mode=pl.Buffered(1))
    return pl.BlockSpec(shape, lambda *_: zeros)


def _load_rounded(src_hbm, dst_ref, stage_ref, sem):
    rows = stage_ref.shape[1]
    n_chunks = src_hbm.shape[0] // rows

    def chunk_copy(c):
        slot = c % 2
        return pltpu.make_async_copy(src_hbm.at[pl.ds(c * rows, rows), :], stage_ref.at[slot],
                                     sem.at[slot])

    chunk_copy(0).start()
    for c in range(n_chunks):
        if c + 1 < n_chunks:
            chunk_copy(c + 1).start()
        chunk_copy(c).wait()
        dst_ref[c * rows:(c + 1) * rows, :] = stage_ref[c % 2].astype(BF16)


def _inproj_kernel(*refs, tiles_per_seq, n_cast):
    (x_ref, g_ref, w_hbm, cw_ref, cb_ref, wr_ref, br_ref, wi_ref, bi_ref, lam_ref), refs = refs[:10], refs[10:]
    cast_in, refs = refs[:n_cast], refs[n_cast:]
    (qkv_ref, gates_ref, hb_ref), refs = refs[:3], refs[3:]
    cast_out, refs = refs[:n_cast], refs[n_cast:]
    (w_ref, stage_ref, stage_sem, xn_ref, gr_ref, xs_ref, xc_ref, rp_ref, ip_ref,
     a_ref, b_ref, hl_ref, pp_ref, h_ref) = refs

    @pl.when(pl.program_id(0) == 0)
    def _():
        _load_rounded(w_hbm, w_ref, stage_ref, stage_sem)

    for src_ref, dst_ref in zip(cast_in, cast_out):
        dst_ref[...] = src_ref[...].astype(BF16)

    @pl.when(pl.program_id(0) % tiles_per_seq == 0)
    def _():
        xs_ref[:, 0:SUBLANES, :] = jnp.zeros((D_MODEL // LANES, SUBLANES, LANES), F32)
        h_ref[...] = jnp.zeros_like(h_ref)

    xn_ref[...] = _rms(x_ref[...], g_ref[...]).astype(BF16)

    def u_cols(c0, width):
        return jnp.dot(xn_ref[...], w_ref[:, c0:c0 + width], preferred_element_type=F32)

    def unit(j, n):
        acc = u_cols(j * D_MODEL + n * MXU_DIM, MXU_DIM)
        if j == 0:
            acc = acc * Q_SCALE
        if j < 3:
            for k in range(MXU_DIM // HEAD_PAIR):
                hp = n * (MXU_DIM // HEAD_PAIR) + k
                qkv_ref[j * N_PAIRS + hp] = acc[:, k * HEAD_PAIR:(k + 1) * HEAD_PAIR].astype(BF16)
        elif j == 4:
            gr_ref[:, n * MXU_DIM:(n + 1) * MXU_DIM] = acc
        else:
            c0 = (j - 5) * D_MODEL + n * MXU_DIM
            gates_ref[:, c0:c0 + MXU_DIM] = acc.astype(BF16)

    units = [(j, n) for j in (4, 0, 1, 2, 5, 6) for n in range(D_MODEL // MXU_DIM)]
    n_pieces = SUBLANES * (D_MODEL // LANES)
    total_cost = n_pieces * (COST_CONV + COST_GATE + COST_OUT) + (x_ref.shape[0] // SUBLANES) * COST_SCAN_STEP
    cost_per_unit = total_cost / len(units)
    spent = [0.0]

    def side_work(cost):
        spent[0] += cost
        while units and spent[0] >= cost_per_unit:
            spent[0] -= cost_per_unit
            unit(*units.pop(0))

    _rglru_tile(u_cols(3 * D_MODEL, D_MODEL), gr_ref, side_work,
                cw_ref, cb_ref, wr_ref, br_ref, wi_ref, bi_ref, lam_ref,
                hb_ref, xs_ref, xc_ref, rp_ref, ip_ref, a_ref, b_ref, hl_ref, pp_ref, h_ref)
    while units:
        unit(*units.pop(0))


BF16_SUBLANES = 2 * SUBLANES


def _cast_slice_spec(rows, cols, n_steps):
    repeat = next(k for k in (1, 2, 4, 8)
                  if rows * k % n_steps == 0 and (rows * k // n_steps) % BF16_SUBLANES == 0)
    return pl.BlockSpec((rows * repeat // n_steps, cols), lambda i: (i // repeat, 0))


def _inproj(x2, g, w, cw, cb, wr, br, wi, bi, lam, later_weights, seq):
    t = x2.shape[0]
    tm = RG_TILE
    n_steps = t // tm
    n_diag = D_MODEL // MXU_DIM
    scan_scratch = pltpu.VMEM((D_MODEL // LANES, SUBLANES * SEG_PITCH, LANES), F32)
    slab_scratch = pltpu.VMEM((tm, MXU_DIM), F32)
    cast_specs = [_cast_slice_spec(*wt.shape, n_steps) for wt in later_weights]
    outs = pl.pallas_call(
        functools.partial(_inproj_kernel, tiles_per_seq=seq // tm, n_cast=len(later_weights)),
        grid=(n_steps,),
        in_specs=[
            pl.BlockSpec((tm, D_MODEL), lambda i: (i, 0)),
            _resident((1, D_MODEL)),
            pl.BlockSpec(memory_space=pl.ANY),
            _resident((CONV_W, D_MODEL)),
            _resident((1, D_MODEL)),
            _resident((n_diag, MXU_DIM, MXU_DIM)),
            _resident((1, D_MODEL)),
            _resident((n_diag, MXU_DIM, MXU_DIM)),
            _resident((1, D_MODEL)),
            _resident((1, D_MODEL)),
            *cast_specs,
        ],
        out_specs=[
            pl.BlockSpec((3 * N_PAIRS, tm, HEAD_PAIR), lambda i: (0, i, 0)),
            pl.BlockSpec((tm, 2 * D_MODEL), lambda i: (i, 0)),
            pl.BlockSpec((tm, D_MODEL), lambda i: (i, 0)),
            *cast_specs,
        ],
        out_shape=[
            jax.ShapeDtypeStruct((3 * N_PAIRS, t, HEAD_PAIR), BF16),
            jax.ShapeDtypeStruct((t, 2 * D_MODEL), BF16),
            jax.ShapeDtypeStruct((t, D_MODEL), BF16),
            *[jax.ShapeDtypeStruct(wt.shape, BF16) for wt in later_weights],
        ],
        scratch_shapes=[
            pltpu.VMEM(w.shape, BF16),
            pltpu.VMEM((2, W_STAGE_ROWS, w.shape[1]), F32),
            pltpu.SemaphoreType.DMA((2,)),
            pltpu.VMEM((tm, D_MODEL), BF16),
            pltpu.VMEM((tm, D_MODEL), F32),
            pltpu.VMEM((D_MODEL // LANES, tm + SUBLANES, LANES), F32),
            slab_scratch, slab_scratch, slab_scratch,
            scan_scratch, scan_scratch, scan_scratch, scan_scratch,
            pltpu.VMEM((SUBLANES, D_MODEL), F32),
        ],
        compiler_params=pltpu.CompilerParams(
            dimension_semantics=("arbitrary",), vmem_limit_bytes=VMEM_LIMIT),
        name="inproj_rglru",
    )(x2, g, w, cw, cb, wr, br, wi, bi, lam, *later_weights)
    return outs[0], outs[1], outs[2], outs[3:]


def _band_geometry(ntiles):
    kw = ntiles * GROUP
    return kw, BAND - kw


def _attn_scores(q_ref, k_ref, s_ref, r0, ntiles):
    kw, c_off = _band_geometry(ntiles)
    lane = lax.broadcasted_iota(jnp.int32, (GROUP, HEAD_PAIR), 1)
    first_head = lane < HEAD_DIM
    qp = q_ref[0, pl.ds(r0, GROUP), :]
    zero = jnp.zeros_like(qp)
    qq = jnp.concatenate([jnp.where(first_head, qp, zero),
                          jnp.where(first_head, zero, qp)], axis=0)
    kk = k_ref[0, pl.ds(r0 + GROUP - kw, kw), :]
    s_ref[:, c_off:] = lax.dot_general(qq, kk, (((1,), (1,)), ((), ())),
                                       preferred_element_type=F32)


def _attn_softmax(s_ref, bias_ref, p_ref, ntiles):
    _, c_off = _band_geometry(ntiles)
    rb = 32
    lane_rb = lax.broadcasted_iota(jnp.int32, (rb, LANES), 1)
    for hh in range(2):
        for i in range(GROUP // CHUNK):
            lo_col = max(i * CHUNK, c_off)
            hi_col = i * CHUNK + (LEFT_CHUNKS + 1) * CHUNK
            ws = (lo_col // LANES) * LANES
            we = -(-hi_col // LANES) * LANES
            nv = (we - ws) // LANES
            rows_all = slice(hh * GROUP + i * CHUNK, hh * GROUP + (i + 1) * CHUNK)
            if ws > c_off:
                p_ref[rows_all, c_off:ws] = jnp.zeros((CHUNK, ws - c_off), BF16)
            if we < BAND:
                p_ref[rows_all, we:BAND] = jnp.zeros((CHUNK, BAND - we), BF16)
            for b in range(CHUNK // rb):
                rows = slice(hh * GROUP + i * CHUNK + b * rb, hh * GROUP + i * CHUNK + (b + 1) * rb)
                brows = slice(i * CHUNK + b * rb, i * CHUNK + (b + 1) * rb)
                first_biased = brows.start + (BAND - GROUP) - MAX_REL + 1
                cols = []
                for c in range(nv):
                    tile = slice(ws + c * LANES, ws + (c + 1) * LANES)
                    col = s_ref[rows, tile]
                    if tile.stop > first_biased:
                        col = col + bias_ref[hh, brows, tile]
                    cols.append(col)
                if lo_col % LANES:
                    cols[0] = jnp.where(lane_rb >= HEAD_DIM, cols[0], NEG_INF)
                if hi_col % LANES:
                    cols[-1] = jnp.where(lane_rb < HEAD_DIM, cols[-1], NEG_INF)
                x = jnp.concatenate(cols, axis=1)
                m = jnp.max(x, axis=-1, keepdims=True)
                p_ref[rows, ws:we] = jnp.exp2(x - m).astype(BF16)


def _attn_values(p_ref, v_ref, o_ref, r0, ntiles):
    kw, c_off = _band_geometry(ntiles)
    lane = lax.broadcasted_iota(jnp.int32, (GROUP, HEAD_PAIR), 1)
    vv = v_ref[0, pl.ds(r0 + GROUP - kw, kw), :]
    v_ones = jnp.concatenate([vv, jnp.ones((kw, LANES), BF16)], axis=1)
    o2 = jnp.dot(p_ref[:, c_off:], v_ones, preferred_element_type=F32)
    o2 = o2[:, :HEAD_PAIR] / o2[:, HEAD_PAIR:]
    o = jnp.where(lane < HEAD_DIM, o2[:GROUP], o2[GROUP:])
    o_ref[0, pl.ds(r0, GROUP), :] = o.astype(BF16)


def _attn_kernel(q_ref, k_ref, v_ref, w_ref, o_ref, bias_ref, s0, s1, p0, p1):
    n_groups = q_ref.shape[1] // GROUP
    full = BAND_TILES
    row0 = lambda g: pl.multiple_of(g * GROUP, GROUP)

    @pl.when(pl.program_id(1) == 0)
    def _():
        for hh in range(2):
            diag = jnp.broadcast_to(w_ref[hh], (GROUP, DIAG_LANES))
            rolled = pltpu.roll(diag, DIAG_LANES - (GROUP - 1), 1, stride=1, stride_axis=0)
            bias_ref[hh] = rolled[:, :BAND]

    def scores(g, s_ref, ntiles=full):
        _attn_scores(q_ref, k_ref, s_ref, row0(g), ntiles)

    def softmax(s_ref, p_ref, ntiles=full):
        _attn_softmax(s_ref, bias_ref, p_ref, ntiles)

    def values(g, p_ref, ntiles=full):
        _attn_values(p_ref, v_ref, o_ref, row0(g), ntiles)

    bufs = ((s0, p0), (s1, p1))
    ntiles = lambda g: min(g + 1, full)
    for t in range(n_groups + 2):
        if t >= 2:
            values(t - 2, bufs[t % 2][1], ntiles(t - 2))
        if 1 <= t <= n_groups:
            softmax(*bufs[(t - 1) % 2], ntiles(t - 1))
        if t < n_groups:
            scores(t, bufs[t % 2][0], ntiles(t))


def _attention(qkv, diagonals, batch, seq):
    t = batch * seq
    return pl.pallas_call(
        _attn_kernel,
        grid=(N_PAIRS, batch),
        in_specs=[
            pl.BlockSpec((1, seq, HEAD_PAIR), lambda hp, b: (hp, b, 0)),
            pl.BlockSpec((1, seq, HEAD_PAIR), lambda hp, b: (N_PAIRS + hp, b, 0)),
            pl.BlockSpec((1, seq, HEAD_PAIR), lambda hp, b: (2 * N_PAIRS + hp, b, 0)),
            pl.BlockSpec((2, 1, DIAG_LANES), lambda hp, b: (hp, 0, 0)),
        ],
        out_specs=pl.BlockSpec((1, seq, HEAD_PAIR), lambda hp, b: (hp, b, 0)),
        out_shape=jax.ShapeDtypeStruct((N_PAIRS, t, HEAD_PAIR), BF16),
        scratch_shapes=[
            pltpu.VMEM((2, GROUP, BAND), F32),
            pltpu.VMEM((2 * GROUP, BAND), F32),
            pltpu.VMEM((2 * GROUP, BAND), F32),
            pltpu.VMEM((2 * GROUP, BAND), BF16),
            pltpu.VMEM((2 * GROUP, BAND), BF16),
        ],
        compiler_params=pltpu.CompilerParams(
            dimension_semantics=("parallel", "arbitrary"), vmem_limit_bytes=VMEM_LIMIT),
        name="band_attention",
    )(qkv, qkv, qkv, diagonals)


def _band_bias_diagonals(rel_table):
    j = jnp.arange(DIAG_LANES)
    dist = (GROUP - 1 - j) + (BAND - GROUP)
    w = rel_table[:, jnp.clip(dist, -MAX_REL, MAX_REL) + MAX_REL] - rel_table[:, 2 * MAX_REL:]
    return (w * LOG2_E)[:, None, :]


def _gelu_tanh(x):
    k = math.sqrt(2.0 / math.pi)
    inner = x * (k + (k * 0.044715) * (x * x))
    return x * (0.5 + 0.5 * jnp.tanh(inner))


def _rglru_tile(xr, gr_ref, side_work, cw_ref, cb_ref, wr_ref, br_ref, wi_ref, bi_ref, lam_ref,
                o_ref, xs_ref, xc_ref, rp_ref, ip_ref, a_ref, b_ref, hl_ref, pp_ref, h_ref):
    tm = xr.shape[0]
    seg = tm // SUBLANES
    n_lane_tiles = D_MODEL // LANES
    tiles_per_slab = MXU_DIM // LANES
    lane_tile = lambda c: slice(c * LANES, (c + 1) * LANES)

    for c in range(n_lane_tiles):
        xs_ref[c, SUBLANES:, :] = xr[:, lane_tile(c)]

    for d in range(D_MODEL // MXU_DIM):
        for k in range(tiles_per_slab):
            c = d * tiles_per_slab + k
            for j in range(SUBLANES):
                acc = cb_ref[:, lane_tile(c)]
                for tap in range(CONV_W):
                    off = SUBLANES - (CONV_W - 1) + tap + j * seg
                    acc = acc + xs_ref[c, off:off + seg, :] * cw_ref[tap:tap + 1, lane_tile(c)]
                xc_ref[j * seg:(j + 1) * seg, lane_tile(k)] = acc
                side_work(COST_CONV)
            xs_ref[c, 0:SUBLANES, :] = xs_ref[c, tm:tm + SUBLANES, :]
        xcb = xc_ref[...].astype(BF16)
        rp_ref[...] = jnp.dot(xcb, wr_ref[d], preferred_element_type=F32)
        ip_ref[...] = jnp.dot(xcb, wi_ref[d], preferred_element_type=F32)
        for k in range(tiles_per_slab):
            c = d * tiles_per_slab + k
            nlam = -lam_ref[:, lane_tile(c)]
            softplus = jnp.maximum(nlam, 0.0) + jnp.log1p(jnp.exp(-jnp.abs(nlam)))
            log_a_scale = (-RG_C) * softplus
            for j in range(SUBLANES):
                rows = slice(j * seg, (j + 1) * seg)
                xc = xc_ref[rows, lane_tile(k)]
                r = jax.nn.sigmoid(rp_ref[rows, lane_tile(k)] + br_ref[:, lane_tile(c)])
                ig = jax.nn.sigmoid(ip_ref[rows, lane_tile(k)] + bi_ref[:, lane_tile(c)])
                log_a = r * log_a_scale
                a = jnp.exp(log_a)
                y = jnp.tanh(log_a) * (-1.0 - a * a)
                mult = jnp.where(y > 0.0, y * lax.rsqrt(y), 0.0)
                srows = slice(j * SEG_PITCH, j * SEG_PITCH + seg)
                a_ref[c, srows, :] = a
                b_ref[c, srows, :] = mult * (ig * xc)
                side_work(COST_GATE)

    hs = [jnp.zeros((SUBLANES, LANES), F32) for _ in range(n_lane_tiles)]
    ps = [jnp.ones((SUBLANES, LANES), F32) for _ in range(n_lane_tiles)]
    for t in range(seg):
        idx = pl.ds(t, SUBLANES, stride=SEG_PITCH)
        for c in range(n_lane_tiles):
            a_t = a_ref[c, idx, :]
            hs[c] = a_t * hs[c] + b_ref[c, idx, :]
            ps[c] = a_t * ps[c]
            hl_ref[c, idx, :] = hs[c]
            pp_ref[c, idx, :] = ps[c]
        side_work(COST_SCAN_STEP)
    h_end = jnp.concatenate(hs, axis=1)
    p_end = jnp.concatenate(ps, axis=1)

    h_in = h_ref[0:1, :]
    for j in range(SUBLANES):
        rows = slice(j * seg, (j + 1) * seg)
        srows = slice(j * SEG_PITCH, j * SEG_PITCH + seg)
        for c in range(n_lane_tiles):
            hj = hl_ref[c, srows, :] + pp_ref[c, srows, :] * h_in[:, lane_tile(c)]
            o_ref[rows, lane_tile(c)] = (hj * _gelu_tanh(gr_ref[rows, lane_tile(c)])).astype(BF16)
            side_work(COST_OUT)
        h_in = p_end[j:j + 1, :] * h_in + h_end[j:j + 1, :]
    h_ref[0:1, :] = h_in


def _block_diag_tiles(w):
    per = MXU_DIM // RG_BLOCK
    n_diag = D_MODEL // MXU_DIM
    w4 = w.reshape(n_diag, per, RG_BLOCK, RG_BLOCK)
    eye = jnp.eye(per, dtype=w.dtype)
    tiles = jnp.einsum('dpij,pq->dpiqj', w4, eye)
    return tiles.reshape(n_diag, MXU_DIM, MXU_DIM)


def _mixffn_kernel(x_ref, at_ref, hb_ref, ga_ref, gb_ref, bm_ref, wa_ref, wb_ref, wo_ref,
                   gn_ref, win_ref, wout_ref, gf_ref, o_ref, *, splits):
    attn = jnp.concatenate([at_ref[hp] for hp in range(N_PAIRS)], axis=1)
    ya = jnp.dot(attn, wa_ref[...], preferred_element_type=F32)
    yb = jnp.dot(hb_ref[...], wb_ref[...], preferred_element_type=F32)
    g_a = jax.nn.sigmoid(ga_ref[...].astype(F32) + bm_ref[:, :D_MODEL])
    g_b = jax.nn.sigmoid(gb_ref[...].astype(F32) + bm_ref[:, D_MODEL:])
    mix = (g_a * ya + g_b * yb).astype(BF16)
    h = x_ref[...] + jnp.dot(mix, wo_ref[...], preferred_element_type=F32)
    hn = _rms(h, gn_ref[...]).astype(BF16)
    d_ff = wout_ref.shape[0]
    acc = h
    for c0, c1 in splits:
        g = jnp.dot(hn, win_ref[:, c0:c1], preferred_element_type=F32)
        up = jnp.dot(hn, win_ref[:, d_ff + c0:d_ff + c1], preferred_element_type=F32)
        act = (jax.nn.silu(g) * up).astype(BF16)
        acc = acc + jnp.dot(act, wout_ref[c0:c1, :], preferred_element_type=F32)
    o_ref[...] = _rms(acc, gf_ref[...])


def _mixffn(x2, attn, hb, gates, bm, wa, wb, wo, gn, win, wout, gf, tm):
    t = x2.shape[0]
    d_ff = wout.shape[0]
    edges = list(range(0, d_ff, FFN_CHUNK)) + [d_ff]
    splits = tuple(zip(edges[:-1], edges[1:]))
    row = lambda i: (i, 0)
    return pl.pallas_call(
        functools.partial(_mixffn_kernel, splits=splits),
        grid=(t // tm,),
        in_specs=[
            pl.BlockSpec((tm, D_MODEL), row),
            pl.BlockSpec((N_PAIRS, tm, HEAD_PAIR), lambda i: (0, i, 0)),
            pl.BlockSpec((tm, D_MODEL), row),
            pl.BlockSpec((tm, D_MODEL), lambda i: (i, 0)),
            pl.BlockSpec((tm, D_MODEL), lambda i: (i, 1)),
            _resident((1, 2 * D_MODEL)),
            _resident((D_MODEL, D_MODEL), single_buffer=True),
            _resident((D_MODEL, D_MODEL), single_buffer=True),
            _resident((D_MODEL, D_MODEL), single_buffer=True),
            _resident((1, D_MODEL)),
            _resident((D_MODEL, 2 * d_ff), single_buffer=True),
            _resident((d_ff, D_MODEL), single_buffer=True),
            _resident((1, D_MODEL)),
        ],
        out_specs=pl.BlockSpec((tm, D_MODEL), row),
        out_shape=jax.ShapeDtypeStruct((t, D_MODEL), F32),
        compiler_params=pltpu.CompilerParams(
            dimension_semantics=("parallel",), vmem_limit_bytes=VMEM_LIMIT),
        name="mix_ffn",
    )(x2, attn, hb, gates, gates, bm, wa, wb, wo, gn, win, wout, gf)


def kernel(x, norm_mix_g, w_in, b_merge, rel_table, w_attn_out, conv_w, conv_b, w_rg_r, b_rg_r,
           w_rg_i, b_rg_i, rg_lambda, w_rnn_out, w_o, norm_ffn_g, w_ffn_in, w_ffn_out, final_norm_g):
    batch, seq, d = x.shape
    assert d == D_MODEL and seq % GROUP == 0 and seq // GROUP >= BAND_TILES
    assert w_in.shape[0] == 1, "single-layer block"
    x2 = x.reshape(batch * seq, d)
    row = lambda v: v.reshape(1, -1)

    assert seq % RG_TILE == 0
    later_weights = (w_attn_out[0], w_rnn_out[0], w_o[0], w_ffn_in[0], w_ffn_out[0])
    qkv, gates, hb, (wa, wb, wo, wfi, wfo) = _inproj(
        x2, row(norm_mix_g[0]), w_in[0], conv_w[0], row(conv_b[0]),
        _block_diag_tiles(w_rg_r[0]).astype(BF16), row(b_rg_r[0]),
        _block_diag_tiles(w_rg_i[0]).astype(BF16), row(b_rg_i[0]), row(rg_lambda[0]),
        later_weights, seq)
    attn = _attention(qkv, _band_bias_diagonals(rel_table[0]), batch, seq)
    out = _mixffn(x2, attn, hb, gates, row(b_merge[0]), wa, wb, wo, row(norm_ffn_g[0]), wfi, wfo,
                  row(final_norm_g), tm=512)
    return out.reshape(batch, seq, d)
```

```python
import functools
import math

import jax
import jax.numpy as jnp
from jax import lax
from jax.experimental import pallas as pl
from jax.experimental.pallas import tpu as pltpu

F32 = jnp.float32
BF16 = jnp.bfloat16

D_MODEL = 1024
CHUNK = 64
LEFT_CHUNKS = 8
N_HEADS = 16
HEAD_DIM = 64
MAX_REL = 128
N_RG_BLOCKS = 16
RG_BLOCK = D_MODEL // N_RG_BLOCKS
CONV_W = 4
RG_C = 8.0
EPS = 1e-6
NEG_INF = -1e30

LANES = 128
SUBLANES = 8
MXU_DIM = 256
HEAD_PAIR = 2 * HEAD_DIM
N_PAIRS = N_HEADS // 2
GROUP = 4 * CHUNK
BAND_TILES = LEFT_CHUNKS * CHUNK // GROUP + 1
BAND = BAND_TILES * GROUP
LOG2_E = math.log2(math.e)
Q_SCALE = LOG2_E / math.sqrt(HEAD_DIM)
DIAG_LANES = -(-(GROUP + BAND - 1) // LANES) * LANES
VMEM_LIMIT = 56 * 1024 * 1024
RG_TILE = 512
W_STAGE_ROWS = 64
FFN_CHUNK = 3 * MXU_DIM
SEG_PITCH = RG_TILE // SUBLANES + SUBLANES // 2
COST_CONV, COST_GATE, COST_SCAN_STEP, COST_OUT = 16, 52, 23, 24


def _rms(x, g):
    var = jnp.mean(x * x, axis=-1, keepdims=True)
    return x * lax.rsqrt(var + EPS) * g


def _resident(shape, single_buffer=False):
    zeros = (0,) * len(shape)
    if single_buffer:
        return pl.BlockSpec(shape, lambda *_: zeros, pipeline_mode=pl.Buffered(1))
    return pl.BlockSpec(shape, lambda *_: zeros)


def _load_rounded(src_hbm, dst_ref, stage_ref, sem):
    rows = stage_ref.shape[1]
    n_chunks = src_hbm.shape[0] // rows

    def chunk_copy(c):
        slot = c % 2
        return pltpu.make_async_copy(src_hbm.at[pl.ds(c * rows, rows), :], stage_ref.at[slot],
                                     sem.at[slot])

    chunk_copy(0).start()
    for c in range(n_chunks):
        if c + 1 < n_chunks:
            chunk_copy(c + 1).start()
        chunk_copy(c).wait()
        dst_ref[c * rows:(c + 1) * rows, :] = stage_ref[c % 2].astype(BF16)


def _fill_block_diag(raw_ref, tiles_ref):
    per = MXU_DIM // RG_BLOCK
    tiles_ref[...] = jnp.zeros_like(tiles_ref)
    for blk in range(N_RG_BLOCKS):
        d, p = divmod(blk, per)
        span = slice(p * RG_BLOCK, (p + 1) * RG_BLOCK)
        tiles_ref[d, span, span] = raw_ref[blk].astype(BF16)


def _inproj_kernel(*refs, tiles_per_seq, n_cast):
    (x_ref, g_ref, w_hbm, cw_ref, cb_ref, wr_raw, br_ref, wi_raw, bi_ref, lam_ref), refs = refs[:10], refs[10:]
    cast_in, refs = refs[:n_cast], refs[n_cast:]
    (qkv_ref, gates_ref, hb_ref), refs = refs[:3], refs[3:]
    cast_out, refs = refs[:n_cast], refs[n_cast:]
    (w_ref, stage_ref, stage_sem, wr_ref, wi_ref, xn_ref, gr_ref, xs_ref, xc_ref, rp_ref, ip_ref,
     a_ref, b_ref, hl_ref, pp_ref, h_ref) = refs

    @pl.when(pl.program_id(0) == 0)
    def _():
        _load_rounded(w_hbm, w_ref, stage_ref, stage_sem)
        _fill_block_diag(wr_raw, wr_ref)
        _fill_block_diag(wi_raw, wi_ref)

    for src_ref, dst_ref in zip(cast_in, cast_out):
        dst_ref[...] = src_ref[...].astype(BF16)

    @pl.when(pl.program_id(0) % tiles_per_seq == 0)
    def _():
        xs_ref[:, 0:SUBLANES, :] = jnp.zeros((D_MODEL // LANES, SUBLANES, LANES), F32)
        h_ref[...] = jnp.zeros_like(h_ref)

    xn_ref[...] = _rms(x_ref[...], g_ref[...]).astype(BF16)

    def u_cols(c0, width):
        return jnp.dot(xn_ref[...], w_ref[:, c0:c0 + width], preferred_element_type=F32)

    def unit(j, n):
        acc = u_cols(j * D_MODEL + n * MXU_DIM, MXU_DIM)
        if j == 0:
            acc = acc * Q_SCALE
        if j < 3:
            for k in range(MXU_DIM // HEAD_PAIR):
                hp = n * (MXU_DIM // HEAD_PAIR) + k
                qkv_ref[j * N_PAIRS + hp] = acc[:, k * HEAD_PAIR:(k + 1) * HEAD_PAIR].astype(BF16)
        elif j == 4:
            gr_ref[:, n * MXU_DIM:(n + 1) * MXU_DIM] = acc
        else:
            c0 = (j - 5) * D_MODEL + n * MXU_DIM
            gates_ref[:, c0:c0 + MXU_DIM] = acc.astype(BF16)

    units = [(j, n) for j in (4, 0, 1, 2, 5, 6) for n in range(D_MODEL // MXU_DIM)]
    n_pieces = SUBLANES * (D_MODEL // LANES)
    total_cost = n_pieces * (COST_CONV + COST_GATE + COST_OUT) + (x_ref.shape[0] // SUBLANES) * COST_SCAN_STEP
    cost_per_unit = total_cost / len(units)
    spent = [0.0]

    def side_work(cost):
        spent[0] += cost
        while units and spent[0] >= cost_per_unit:
            spent[0] -= cost_per_unit
            unit(*units.pop(0))

    _rglru_tile(u_cols(3 * D_MODEL, D_MODEL), gr_ref, side_work,
                cw_ref, cb_ref, wr_ref, br_ref, wi_ref, bi_ref, lam_ref,
                hb_ref, xs_ref, xc_ref, rp_ref, ip_ref, a_ref, b_ref, hl_ref, pp_ref, h_ref)
    while units:
        unit(*units.pop(0))


BF16_SUBLANES = 2 * SUBLANES


def _cast_slice_spec(rows, cols, n_steps):
    repeat = next(k for k in (1, 2, 4, 8)
                  if rows * k % n_steps == 0 and (rows * k // n_steps) % BF16_SUBLANES == 0)
    return pl.BlockSpec((rows * repeat // n_steps, cols), lambda i: (i // repeat, 0))


def _inproj(x2, g, w, cw, cb, wr, br, wi, bi, lam, later_weights, seq):
    t = x2.shape[0]
    tm = RG_TILE
    n_steps = t // tm
    n_diag = D_MODEL // MXU_DIM
    scan_scratch = pltpu.VMEM((D_MODEL // LANES, SUBLANES * SEG_PITCH, LANES), F32)
    slab_scratch = pltpu.VMEM((tm, MXU_DIM), F32)
    cast_specs = [_cast_slice_spec(*wt.shape, n_steps) for wt in later_weights]
    outs = pl.pallas_call(
        functools.partial(_inproj_kernel, tiles_per_seq=seq // tm, n_cast=len(later_weights)),
        grid=(n_steps,),
        in_specs=[
            pl.BlockSpec((tm, D_MODEL), lambda i: (i, 0)),
            _resident((1, D_MODEL)),
            pl.BlockSpec(memory_space=pl.ANY),
            _resident((CONV_W, D_MODEL)),
            _resident((1, D_MODEL)),
            _resident((N_RG_BLOCKS, RG_BLOCK, RG_BLOCK)),
            _resident((1, D_MODEL)),
            _resident((N_RG_BLOCKS, RG_BLOCK, RG_BLOCK)),
            _resident((1, D_MODEL)),
            _resident((1, D_MODEL)),
            *cast_specs,
        ],
        out_specs=[
            pl.BlockSpec((3 * N_PAIRS, tm, HEAD_PAIR), lambda i: (0, i, 0)),
            pl.BlockSpec((tm, 2 * D_MODEL), lambda i: (i, 0)),
            pl.BlockSpec((tm, D_MODEL), lambda i: (i, 0)),
            *cast_specs,
        ],
        out_shape=[
            jax.ShapeDtypeStruct((3 * N_PAIRS, t, HEAD_PAIR), BF16),
            jax.ShapeDtypeStruct((t, 2 * D_MODEL), BF16),
            jax.ShapeDtypeStruct((t, D_MODEL), BF16),
            *[jax.ShapeDtypeStruct(wt.shape, BF16) for wt in later_weights],
        ],
        scratch_shapes=[
            pltpu.VMEM(w.shape, BF16),
            pltpu.VMEM((2, W_STAGE_ROWS, w.shape[1]), F32),
            pltpu.SemaphoreType.DMA((2,)),
            pltpu.VMEM((n_diag, MXU_DIM, MXU_DIM), BF16),
            pltpu.VMEM((n_diag, MXU_DIM, MXU_DIM), BF16),
            pltpu.VMEM((tm, D_MODEL), BF16),
            pltpu.VMEM((tm, D_MODEL), F32),
            pltpu.VMEM((D_MODEL // LANES, tm + SUBLANES, LANES), F32),
            slab_scratch, slab_scratch, slab_scratch,
            scan_scratch, scan_scratch, scan_scratch, scan_scratch,
            pltpu.VMEM((SUBLANES, D_MODEL), F32),
        ],
        compiler_params=pltpu.CompilerParams(
            dimension_semantics=("arbitrary",), vmem_limit_bytes=VMEM_LIMIT),
        name="inproj_rglru",
    )(x2, g, w, cw, cb, wr, br, wi, bi, lam, *later_weights)
    return outs[0], outs[1], outs[2], outs[3:]


def _band_geometry(ntiles):
    kw = ntiles * GROUP
    return kw, BAND - kw


def _attn_scores(q_ref, k_ref, s_ref, r0, ntiles):
    kw, c_off = _band_geometry(ntiles)
    lane = lax.broadcasted_iota(jnp.int32, (GROUP, HEAD_PAIR), 1)
    first_head = lane < HEAD_DIM
    qp = q_ref[0, pl.ds(r0, GROUP), :]
    zero = jnp.zeros_like(qp)
    qq = jnp.concatenate([jnp.where(first_head, qp, zero),
                          jnp.where(first_head, zero, qp)], axis=0)
    kk = k_ref[0, pl.ds(r0 + GROUP - kw, kw), :]
    s_ref[:, c_off:] = lax.dot_general(qq, kk, (((1,), (1,)), ((), ())),
                                       preferred_element_type=F32)


def _attn_softmax(s_ref, bias_ref, p_ref, ntiles):
    _, c_off = _band_geometry(ntiles)
    rb = 32
    lane_rb = lax.broadcasted_iota(jnp.int32, (rb, LANES), 1)
    for hh in range(2):
        for i in range(GROUP // CHUNK):
            lo_col = max(i * CHUNK, c_off)
            hi_col = i * CHUNK + (LEFT_CHUNKS + 1) * CHUNK
            ws = (lo_col // LANES) * LANES
            we = -(-hi_col // LANES) * LANES
            nv = (we - ws) // LANES
            rows_all = slice(hh * GROUP + i * CHUNK, hh * GROUP + (i + 1) * CHUNK)
            if ws > c_off:
                p_ref[rows_all, c_off:ws] = jnp.zeros((CHUNK, ws - c_off), BF16)
            if we < BAND:
                p_ref[rows_all, we:BAND] = jnp.zeros((CHUNK, BAND - we), BF16)
            for b in range(CHUNK // rb):
                rows = slice(hh * GROUP + i * CHUNK + b * rb, hh * GROUP + i * CHUNK + (b + 1) * rb)
                brows = slice(i * CHUNK + b * rb, i * CHUNK + (b + 1) * rb)
                first_biased = brows.start + (BAND - GROUP) - MAX_REL + 1
                cols = []
                for c in range(nv):
                    tile = slice(ws + c * LANES, ws + (c + 1) * LANES)
                    col = s_ref[rows, tile]
                    if tile.stop > first_biased:
                        col = col + bias_ref[hh, brows, tile]
                    cols.append(col)
                if lo_col % LANES:
                    cols[0] = jnp.where(lane_rb >= HEAD_DIM, cols[0], NEG_INF)
                if hi_col % LANES:
                    cols[-1] = jnp.where(lane_rb < HEAD_DIM, cols[-1], NEG_INF)
                x = jnp.concatenate(cols, axis=1)
                m = jnp.max(x, axis=-1, keepdims=True)
                p_ref[rows, ws:we] = jnp.exp2(x - m).astype(BF16)


def _attn_values(p_ref, v_ref, o_ref, r0, ntiles):
    kw, c_off = _band_geometry(ntiles)
    lane = lax.broadcasted_iota(jnp.int32, (GROUP, HEAD_PAIR), 1)
    vv = v_ref[0, pl.ds(r0 + GROUP - kw, kw), :]
    v_ones = jnp.concatenate([vv, jnp.ones((kw, LANES), BF16)], axis=1)
    o2 = jnp.dot(p_ref[:, c_off:], v_ones, preferred_element_type=F32)
    o2 = o2[:, :HEAD_PAIR] / o2[:, HEAD_PAIR:]
    o = jnp.where(lane < HEAD_DIM, o2[:GROUP], o2[GROUP:])
    o_ref[0, pl.ds(r0, GROUP), :] = o.astype(BF16)


def _attn_kernel(q_ref, k_ref, v_ref, w_ref, o_ref, bias_ref, s0, s1, p0, p1):
    n_groups = q_ref.shape[1] // GROUP
    full = BAND_TILES
    row0 = lambda g: pl.multiple_of(g * GROUP, GROUP)

    @pl.when(pl.program_id(1) == 0)
    def _():
        for hh in range(2):
            diag = jnp.broadcast_to(w_ref[hh], (GROUP, DIAG_LANES))
            rolled = pltpu.roll(diag, DIAG_LANES - (GROUP - 1), 1, stride=1, stride_axis=0)
            bias_ref[hh] = rolled[:, :BAND]

    def scores(g, s_ref, ntiles=full):
        _attn_scores(q_ref, k_ref, s_ref, row0(g), ntiles)

    def softmax(s_ref, p_ref, ntiles=full):
        _attn_softmax(s_ref, bias_ref, p_ref, ntiles)

    def values(g, p_ref, ntiles=full):
        _attn_values(p_ref, v_ref, o_ref, row0(g), ntiles)

    bufs = ((s0, p0), (s1, p1))
    ntiles = lambda g: min(g + 1, full)
    for t in range(n_groups + 2):
        if t >= 2:
            values(t - 2, bufs[t % 2][1], ntiles(t - 2))
        if 1 <= t <= n_groups:
            softmax(*bufs[(t - 1) % 2], ntiles(t - 1))
        if t < n_groups:
            scores(t, bufs[t % 2][0], ntiles(t))


def _attention(qkv, diagonals, batch, seq):
    t = batch * seq
    return pl.pallas_call(
        _attn_kernel,
        grid=(N_PAIRS, batch),
        in_specs=[
            pl.BlockSpec((1, seq, HEAD_PAIR), lambda hp, b: (hp, b, 0)),
            pl.BlockSpec((1, seq, HEAD_PAIR), lambda hp, b: (N_PAIRS + hp, b, 0)),
            pl.BlockSpec((1, seq, HEAD_PAIR), lambda hp, b: (2 * N_PAIRS + hp, b, 0)),
            pl.BlockSpec((2, 1, DIAG_LANES), lambda hp, b: (hp, 0, 0)),
        ],
        out_specs=pl.BlockSpec((1, seq, HEAD_PAIR), lambda hp, b: (hp, b, 0)),
        out_shape=jax.ShapeDtypeStruct((N_PAIRS, t, HEAD_PAIR), BF16),
        scratch_shapes=[
            pltpu.VMEM((2, GROUP, BAND), F32),
            pltpu.VMEM((2 * GROUP, BAND), F32),
            pltpu.VMEM((2 * GROUP, BAND), F32),
            pltpu.VMEM((2 * GROUP, BAND), BF16),
            pltpu.VMEM((2 * GROUP, BAND), BF16),
        ],
        compiler_params=pltpu.CompilerParams(
            dimension_semantics=("parallel", "arbitrary"), vmem_limit_bytes=VMEM_LIMIT),
        name="band_attention",
    )(qkv, qkv, qkv, diagonals)


def _band_bias_diagonals(rel_table):
    j = jnp.arange(DIAG_LANES)
    dist = (GROUP - 1 - j) + (BAND - GROUP)
    w = rel_table[:, jnp.clip(dist, -MAX_REL, MAX_REL) + MAX_REL] - rel_table[:, 2 * MAX_REL:]
    return (w * LOG2_E)[:, None, :]


def _gelu_tanh(x):
    k = math.sqrt(2.0 / math.pi)
    inner = x * (k + (k * 0.044715) * (x * x))
    return x * (0.5 + 0.5 * jnp.tanh(inner))


def _rglru_tile(xr, gr_ref, side_work, cw_ref, cb_ref, wr_ref, br_ref, wi_ref, bi_ref, lam_ref,
                o_ref, xs_ref, xc_ref, rp_ref, ip_ref, a_ref, b_ref, hl_ref, pp_ref, h_ref):
    tm = xr.shape[0]
    seg = tm // SUBLANES
    n_lane_tiles = D_MODEL // LANES
    tiles_per_slab = MXU_DIM // LANES
    lane_tile = lambda c: slice(c * LANES, (c + 1) * LANES)

    for c in range(n_lane_tiles):
        xs_ref[c, SUBLANES:, :] = xr[:, lane_tile(c)]

    for d in range(D_MODEL // MXU_DIM):
        for k in range(tiles_per_slab):
            c = d * tiles_per_slab + k
            for j in range(SUBLANES):
                acc = cb_ref[:, lane_tile(c)]
                for tap in range(CONV_W):
                    off = SUBLANES - (CONV_W - 1) + tap + j * seg
                    acc = acc + xs_ref[c, off:off + seg, :] * cw_ref[tap:tap + 1, lane_tile(c)]
                xc_ref[j * seg:(j + 1) * seg, lane_tile(k)] = acc
                side_work(COST_CONV)
            xs_ref[c, 0:SUBLANES, :] = xs_ref[c, tm:tm + SUBLANES, :]
        xcb = xc_ref[...].astype(BF16)
        rp_ref[...] = jnp.dot(xcb, wr_ref[d], preferred_element_type=F32)
        ip_ref[...] = jnp.dot(xcb, wi_ref[d], preferred_element_type=F32)
        for k in range(tiles_per_slab):
            c = d * tiles_per_slab + k
            nlam = -lam_ref[:, lane_tile(c)]
            softplus = jnp.maximum(nlam, 0.0) + jnp.log1p(jnp.exp(-jnp.abs(nlam)))
            log_a_scale = (-RG_C) * softplus
            for j in range(SUBLANES):
                rows = slice(j * seg, (j + 1) * seg)
                xc = xc_ref[rows, lane_tile(k)]
                r = jax.nn.sigmoid(rp_ref[rows, lane_tile(k)] + br_ref[:, lane_tile(c)])
                ig = jax.nn.sigmoid(ip_ref[rows, lane_tile(k)] + bi_ref[:, lane_tile(c)])
                log_a = r * log_a_scale
                a = jnp.exp(log_a)
                y = jnp.tanh(log_a) * (-1.0 - a * a)
                mult = jnp.where(y > 0.0, y * lax.rsqrt(y), 0.0)
                srows = slice(j * SEG_PITCH, j * SEG_PITCH + seg)
                a_ref[c, srows, :] = a
                b_ref[c, srows, :] = mult * (ig * xc)
                side_work(COST_GATE)

    hs = [jnp.zeros((SUBLANES, LANES), F32) for _ in range(n_lane_tiles)]
    ps = [jnp.ones((SUBLANES, LANES), F32) for _ in range(n_lane_tiles)]
    for t in range(seg):
        idx = pl.ds(t, SUBLANES, stride=SEG_PITCH)
        for c in range(n_lane_tiles):
            a_t = a_ref[c, idx, :]
            hs[c] = a_t * hs[c] + b_ref[c, idx, :]
            ps[c] = a_t * ps[c]
            hl_ref[c, idx, :] = hs[c]
            pp_ref[c, idx, :] = ps[c]
        side_work(COST_SCAN_STEP)
    h_end = jnp.concatenate(hs, axis=1)
    p_end = jnp.concatenate(ps, axis=1)

    h_in = h_ref[0:1, :]
    for j in range(SUBLANES):
        rows = slice(j * seg, (j + 1) * seg)
        srows = slice(j * SEG_PITCH, j * SEG_PITCH + seg)
        for c in range(n_lane_tiles):
            hj = hl_ref[c, srows, :] + pp_ref[c, srows, :] * h_in[:, lane_tile(c)]
            o_ref[rows, lane_tile(c)] = (hj * _gelu_tanh(gr_ref[rows, lane_tile(c)])).astype(BF16)
            side_work(COST_OUT)
        h_in = p_end[j:j + 1, :] * h_in + h_end[j:j + 1, :]
    h_ref[0:1, :] = h_in


def _mixffn_kernel(x_ref, at_ref, hb_ref, ga_ref, gb_ref, bm_ref, wa_ref, wb_ref, wo_ref,
                   gn_ref, win_ref, wout_ref, gf_ref, o_ref, *, splits):
    attn = jnp.concatenate([at_ref[hp] for hp in range(N_PAIRS)], axis=1)
    ya = jnp.dot(attn, wa_ref[...], preferred_element_type=F32)
    yb = jnp.dot(hb_ref[...], wb_ref[...], preferred_element_type=F32)
    g_a = jax.nn.sigmoid(ga_ref[...].astype(F32) + bm_ref[:, :D_MODEL])
    g_b = jax.nn.sigmoid(gb_ref[...].astype(F32) + bm_ref[:, D_MODEL:])
    mix = (g_a * ya + g_b * yb).astype(BF16)
    h = x_ref[...] + jnp.dot(mix, wo_ref[...], preferred_element_type=F32)
    hn = _rms(h, gn_ref[...]).astype(BF16)
    d_ff = wout_ref.shape[0]
    acc = h
    for c0, c1 in splits:
        g = jnp.dot(hn, win_ref[:, c0:c1], preferred_element_type=F32)
        up = jnp.dot(hn, win_ref[:, d_ff + c0:d_ff + c1], preferred_element_type=F32)
        act = (jax.nn.silu(g) * up).astype(BF16)
        acc = acc + jnp.dot(act, wout_ref[c0:c1, :], preferred_element_type=F32)
    o_ref[...] = _rms(acc, gf_ref[...])


def _mixffn(x2, attn, hb, gates, bm, wa, wb, wo, gn, win, wout, gf, tm):
    t = x2.shape[0]
    d_ff = wout.shape[0]
    edges = list(range(0, d_ff, FFN_CHUNK)) + [d_ff]
    splits = tuple(zip(edges[:-1], edges[1:]))
    row = lambda i: (i, 0)
    return pl.pallas_call(
        functools.partial(_mixffn_kernel, splits=splits),
        grid=(t // tm,),
        in_specs=[
            pl.BlockSpec((tm, D_MODEL), row),
            pl.BlockSpec((N_PAIRS, tm, HEAD_PAIR), lambda i: (0, i, 0)),
            pl.BlockSpec((tm, D_MODEL), row),
            pl.BlockSpec((tm, D_MODEL), lambda i: (i, 0)),
            pl.BlockSpec((tm, D_MODEL), lambda i: (i, 1)),
            _resident((1, 2 * D_MODEL)),
            _resident((D_MODEL, D_MODEL), single_buffer=True),
            _resident((D_MODEL, D_MODEL), single_buffer=True),
            _resident((D_MODEL, D_MODEL), single_buffer=True),
            _resident((1, D_MODEL)),
            _resident((D_MODEL, 2 * d_ff), single_buffer=True),
            _resident((d_ff, D_MODEL), single_buffer=True),
            _resident((1, D_MODEL)),
        ],
        out_specs=pl.BlockSpec((tm, D_MODEL), row),
        out_shape=jax.ShapeDtypeStruct((t, D_MODEL), F32),
        compiler_params=pltpu.CompilerParams(
            dimension_semantics=("parallel",), vmem_limit_bytes=VMEM_LIMIT),
        name="mix_ffn",
    )(x2, attn, hb, gates, gates, bm, wa, wb, wo, gn, win, wout, gf)


def kernel(x, norm_mix_g, w_in, b_merge, rel_table, w_attn_out, conv_w, conv_b, w_rg_r, b_rg_r,
           w_rg_i, b_rg_i, rg_lambda, w_rnn_out, w_o, norm_ffn_g, w_ffn_in, w_ffn_out, final_norm_g):
    batch, seq, d = x.shape
    assert d == D_MODEL and seq % GROUP == 0 and seq // GROUP >= BAND_TILES
    assert w_in.shape[0] == 1, "single-layer block"
    x2 = x.reshape(batch * seq, d)
    row = lambda v: v.reshape(1, -1)

    assert seq % RG_TILE == 0
    later_weights = (w_attn_out[0], w_rnn_out[0], w_o[0], w_ffn_in[0], w_ffn_out[0])
    qkv, gates, hb, (wa, wb, wo, wfi, wfo) = _inproj(
        x2, row(norm_mix_g[0]), w_in[0], conv_w[0], row(conv_b[0]),
        w_rg_r[0], row(b_rg_r[0]), w_rg_i[0], row(b_rg_i[0]), row(rg_lambda[0]),
        later_weights, seq)
    attn = _attention(qkv, _band_bias_diagonals(rel_table[0]), batch, seq)
    out = _mixffn(x2, attn, hb, gates, row(b_merge[0]), wa, wb, wo, row(norm_ffn_g[0]), wfi, wfo,
                  row(final_norm_g), tm=512)
    return out.reshape(batch, seq, d)
```

```python
import functools
import math

import jax
import jax.numpy as jnp
from jax import lax
from jax.experimental import pallas as pl
from jax.experimental.pallas import tpu as pltpu

F32 = jnp.float32
BF16 = jnp.bfloat16

D_MODEL = 1024
CHUNK = 64
LEFT_CHUNKS = 8
N_HEADS = 16
HEAD_DIM = 64
MAX_REL = 128
N_RG_BLOCKS = 16
RG_BLOCK = D_MODEL // N_RG_BLOCKS
CONV_W = 4
RG_C = 8.0
EPS = 1e-6
NEG_INF = -1e30

LANES = 128
SUBLANES = 8
MXU_DIM = 256
HEAD_PAIR = 2 * HEAD_DIM
N_PAIRS = N_HEADS // 2
GROUP = 4 * CHUNK
BAND_TILES = LEFT_CHUNKS * CHUNK // GROUP + 1
BAND = BAND_TILES * GROUP
LOG2_E = math.log2(math.e)
Q_SCALE = LOG2_E / math.sqrt(HEAD_DIM)
DIAG_LANES = -(-(GROUP + BAND - 1) // LANES) * LANES
VMEM_LIMIT = 56 * 1024 * 1024
RG_TILE = 512
W_STAGE_ROWS = 64
FFN_CHUNK = 3 * MXU_DIM
X_RING_SLOTS = 3
SEG_PITCH = RG_TILE // SUBLANES + SUBLANES // 2
COST_CONV, COST_GATE, COST_SCAN_STEP, COST_OUT = 16, 52, 23, 24


def _rms(x, g):
    var = jnp.mean(x * x, axis=-1, keepdims=True)
    return x * lax.rsqrt(var + EPS) * g


def _resident(shape, single_buffer=False):
    zeros = (0,) * len(shape)
    if single_buffer:
        return pl.BlockSpec(shape, lambda *_: zeros, pipeline_mode=pl.Buffered(1))
    return pl.BlockSpec(shape, lambda *_: zeros)


def _load_rounded(src_hbm, dst_ref, stage_ref, sem):
    rows = stage_ref.shape[1]
    n_chunks = src_hbm.shape[0] // rows

    def chunk_copy(c):
        slot = c % 2
        return pltpu.make_async_copy(src_hbm.at[pl.ds(c * rows, rows), :], stage_ref.at[slot],
                                     sem.at[slot])

    chunk_copy(0).start()
    for c in range(n_chunks):
        if c + 1 < n_chunks:
            chunk_copy(c + 1).start()
        chunk_copy(c).wait()
        dst_ref[c * rows:(c + 1) * rows, :] = stage_ref[c % 2].astype(BF16)


def _fill_block_diag(raw_ref, tiles_ref):
    per = MXU_DIM // RG_BLOCK
    tiles_ref[...] = jnp.zeros_like(tiles_ref)
    for blk in range(N_RG_BLOCKS):
        d, p = divmod(blk, per)
        span = slice(p * RG_BLOCK, (p + 1) * RG_BLOCK)
        tiles_ref[d, span, span] = raw_ref[blk].astype(BF16)


def _inproj_kernel(*refs, tiles_per_seq, n_cast):
    (x_ref, g_ref, w_hbm, cw_ref, cb_ref, wr_raw, br_ref, wi_raw, bi_ref, lam_ref), refs = refs[:10], refs[10:]
    cast_in, refs = refs[:n_cast], refs[n_cast:]
    (qkv_ref, gates_ref, hb_ref), refs = refs[:3], refs[3:]
    cast_out, refs = refs[:n_cast], refs[n_cast:]
    (w_ref, stage_ref, stage_sem, wr_ref, wi_ref, xn_ref, gr_ref, xs_ref, xc_ref, rp_ref, ip_ref,
     a_ref, b_ref, hl_ref, pp_ref, h_ref) = refs

    @pl.when(pl.program_id(0) == 0)
    def _():
        _load_rounded(w_hbm, w_ref, stage_ref, stage_sem)
        _fill_block_diag(wr_raw, wr_ref)
        _fill_block_diag(wi_raw, wi_ref)

    for src_ref, dst_ref in zip(cast_in, cast_out):
        dst_ref[...] = src_ref[...].astype(BF16)

    @pl.when(pl.program_id(0) % tiles_per_seq == 0)
    def _():
        xs_ref[:, 0:SUBLANES, :] = jnp.zeros((D_MODEL // LANES, SUBLANES, LANES), F32)
        h_ref[...] = jnp.zeros_like(h_ref)

    xn_ref[...] = _rms(x_ref[...], g_ref[...]).astype(BF16)

    def u_cols(c0, width):
        return jnp.dot(xn_ref[...], w_ref[:, c0:c0 + width], preferred_element_type=F32)

    def unit(j, n):
        acc = u_cols(j * D_MODEL + n * MXU_DIM, MXU_DIM)
        if j == 0:
            acc = acc * Q_SCALE
        if j < 3:
            for k in range(MXU_DIM // HEAD_PAIR):
                hp = n * (MXU_DIM // HEAD_PAIR) + k
                qkv_ref[j * N_PAIRS + hp] = acc[:, k * HEAD_PAIR:(k + 1) * HEAD_PAIR].astype(BF16)
        elif j == 4:
            gr_ref[:, n * MXU_DIM:(n + 1) * MXU_DIM] = acc
        else:
            c0 = (j - 5) * D_MODEL + n * MXU_DIM
            gates_ref[:, c0:c0 + MXU_DIM] = acc.astype(BF16)

    units = [(j, n) for j in (4, 0, 1, 2, 5, 6) for n in range(D_MODEL // MXU_DIM)]
    n_pieces = SUBLANES * (D_MODEL // LANES)
    total_cost = n_pieces * (COST_CONV + COST_GATE + COST_OUT) + (x_ref.shape[0] // SUBLANES) * COST_SCAN_STEP
    cost_per_unit = total_cost / len(units)
    spent = [0.0]

    def side_work(cost):
        spent[0] += cost
        while units and spent[0] >= cost_per_unit:
            spent[0] -= cost_per_unit
            unit(*units.pop(0))

    _rglru_tile(u_cols(3 * D_MODEL, D_MODEL), gr_ref, side_work,
                cw_ref, cb_ref, wr_ref, br_ref, wi_ref, bi_ref, lam_ref,
                hb_ref, xs_ref, xc_ref, rp_ref, ip_ref, a_ref, b_ref, hl_ref, pp_ref, h_ref)
    while units:
        unit(*units.pop(0))


BF16_SUBLANES = 2 * SUBLANES


def _cast_slice_spec(rows, cols, n_steps):
    repeat = next(k for k in (1, 2, 4, 8)
                  if rows * k % n_steps == 0 and (rows * k // n_steps) % BF16_SUBLANES == 0)
    return pl.BlockSpec((rows * repeat // n_steps, cols), lambda i: (i // repeat, 0))


def _inproj(x2, g, w, cw, cb, wr, br, wi, bi, lam, later_weights, seq):
    t = x2.shape[0]
    tm = RG_TILE
    n_steps = t // tm
    n_diag = D_MODEL // MXU_DIM
    scan_scratch = pltpu.VMEM((D_MODEL // LANES, SUBLANES * SEG_PITCH, LANES), F32)
    slab_scratch = pltpu.VMEM((tm, MXU_DIM), F32)
    cast_specs = [_cast_slice_spec(*wt.shape, n_steps) for wt in later_weights]
    outs = pl.pallas_call(
        functools.partial(_inproj_kernel, tiles_per_seq=seq // tm, n_cast=len(later_weights)),
        grid=(n_steps,),
        in_specs=[
            pl.BlockSpec((tm, D_MODEL), lambda i: (i, 0)),
            _resident((1, D_MODEL)),
            pl.BlockSpec(memory_space=pl.ANY),
            _resident((CONV_W, D_MODEL)),
            _resident((1, D_MODEL)),
            _resident((N_RG_BLOCKS, RG_BLOCK, RG_BLOCK)),
            _resident((1, D_MODEL)),
            _resident((N_RG_BLOCKS, RG_BLOCK, RG_BLOCK)),
            _resident((1, D_MODEL)),
            _resident((1, D_MODEL)),
            *cast_specs,
        ],
        out_specs=[
            pl.BlockSpec((3 * N_PAIRS, tm, HEAD_PAIR), lambda i: (0, i, 0)),
            pl.BlockSpec((tm, 2 * D_MODEL), lambda i: (i, 0)),
            pl.BlockSpec((tm, D_MODEL), lambda i: (i, 0)),
            *cast_specs,
        ],
        out_shape=[
            jax.ShapeDtypeStruct((3 * N_PAIRS, t, HEAD_PAIR), BF16),
            jax.ShapeDtypeStruct((t, 2 * D_MODEL), BF16),
            jax.ShapeDtypeStruct((t, D_MODEL), BF16),
            *[jax.ShapeDtypeStruct(wt.shape, BF16) for wt in later_weights],
        ],
        scratch_shapes=[
            pltpu.VMEM(w.shape, BF16),
            pltpu.VMEM((2, W_STAGE_ROWS, w.shape[1]), F32),
            pltpu.SemaphoreType.DMA((2,)),
            pltpu.VMEM((n_diag, MXU_DIM, MXU_DIM), BF16),
            pltpu.VMEM((n_diag, MXU_DIM, MXU_DIM), BF16),
            pltpu.VMEM((tm, D_MODEL), BF16),
            pltpu.VMEM((tm, D_MODEL), F32),
            pltpu.VMEM((D_MODEL // LANES, tm + SUBLANES, LANES), F32),
            slab_scratch, slab_scratch, slab_scratch,
            scan_scratch, scan_scratch, scan_scratch, scan_scratch,
            pltpu.VMEM((SUBLANES, D_MODEL), F32),
        ],
        compiler_params=pltpu.CompilerParams(
            dimension_semantics=("arbitrary",), vmem_limit_bytes=VMEM_LIMIT),
        name="inproj_rglru",
    )(x2, g, w, cw, cb, wr, br, wi, bi, lam, *later_weights)
    return outs[0], outs[1], outs[2], outs[3:]


def _band_geometry(ntiles):
    kw = ntiles * GROUP
    return kw, BAND - kw


def _attn_scores(q_ref, k_ref, s_ref, r0, ntiles):
    kw, c_off = _band_geometry(ntiles)
    lane = lax.broadcasted_iota(jnp.int32, (GROUP, HEAD_PAIR), 1)
    first_head = lane < HEAD_DIM
    qp = q_ref[0, pl.ds(r0, GROUP), :]
    zero = jnp.zeros_like(qp)
    qq = jnp.concatenate([jnp.where(first_head, qp, zero),
                          jnp.where(first_head, zero, qp)], axis=0)
    kk = k_ref[0, pl.ds(r0 + GROUP - kw, kw), :]
    s_ref[:, c_off:] = lax.dot_general(qq, kk, (((1,), (1,)), ((), ())),
                                       preferred_element_type=F32)


def _attn_softmax(s_ref, bias_ref, p_ref, ntiles):
    _, c_off = _band_geometry(ntiles)
    rb = 32
    lane_rb = lax.broadcasted_iota(jnp.int32, (rb, LANES), 1)
    for hh in range(2):
        for i in range(GROUP // CHUNK):
            lo_col = max(i * CHUNK, c_off)
            hi_col = i * CHUNK + (LEFT_CHUNKS + 1) * CHUNK
            ws = (lo_col // LANES) * LANES
            we = -(-hi_col // LANES) * LANES
            nv = (we - ws) // LANES
            rows_all = slice(hh * GROUP + i * CHUNK, hh * GROUP + (i + 1) * CHUNK)
            if ws > c_off:
                p_ref[rows_all, c_off:ws] = jnp.zeros((CHUNK, ws - c_off), BF16)
            if we < BAND:
                p_ref[rows_all, we:BAND] = jnp.zeros((CHUNK, BAND - we), BF16)
            for b in range(CHUNK // rb):
                rows = slice(hh * GROUP + i * CHUNK + b * rb, hh * GROUP + i * CHUNK + (b + 1) * rb)
                brows = slice(i * CHUNK + b * rb, i * CHUNK + (b + 1) * rb)
                first_biased = brows.start + (BAND - GROUP) - MAX_REL + 1
                cols = []
                for c in range(nv):
                    tile = slice(ws + c * LANES, ws + (c + 1) * LANES)
                    col = s_ref[rows, tile]
                    if tile.stop > first_biased:
                        col = col + bias_ref[hh, brows, tile]
                    cols.append(col)
                if lo_col % LANES:
                    cols[0] = jnp.where(lane_rb >= HEAD_DIM, cols[0], NEG_INF)
                if hi_col % LANES:
                    cols[-1] = jnp.where(lane_rb < HEAD_DIM, cols[-1], NEG_INF)
                x = jnp.concatenate(cols, axis=1)
                m = jnp.max(x, axis=-1, keepdims=True)
                p_ref[rows, ws:we] = jnp.exp2(x - m).astype(BF16)


def _attn_values(p_ref, v_ref, o_ref, r0, ntiles):
    kw, c_off = _band_geometry(ntiles)
    lane = lax.broadcasted_iota(jnp.int32, (GROUP, HEAD_PAIR), 1)
    vv = v_ref[0, pl.ds(r0 + GROUP - kw, kw), :]
    v_ones = jnp.concatenate([vv, jnp.ones((kw, LANES), BF16)], axis=1)
    o2 = jnp.dot(p_ref[:, c_off:], v_ones, preferred_element_type=F32)
    o2 = o2[:, :HEAD_PAIR] / o2[:, HEAD_PAIR:]
    o = jnp.where(lane < HEAD_DIM, o2[:GROUP], o2[GROUP:])
    o_ref[0, pl.ds(r0, GROUP), :] = o.astype(BF16)


def _attn_kernel(q_ref, k_ref, v_ref, w_ref, o_ref, bias_ref, s0, s1, p0, p1):
    n_groups = q_ref.shape[1] // GROUP
    full = BAND_TILES
    row0 = lambda g: pl.multiple_of(g * GROUP, GROUP)

    @pl.when(pl.program_id(1) == 0)
    def _():
        for hh in range(2):
            diag = jnp.broadcast_to(w_ref[hh], (GROUP, DIAG_LANES))
            rolled = pltpu.roll(diag, DIAG_LANES - (GROUP - 1), 1, stride=1, stride_axis=0)
            bias_ref[hh] = rolled[:, :BAND]

    def scores(g, s_ref, ntiles=full):
        _attn_scores(q_ref, k_ref, s_ref, row0(g), ntiles)

    def softmax(s_ref, p_ref, ntiles=full):
        _attn_softmax(s_ref, bias_ref, p_ref, ntiles)

    def values(g, p_ref, ntiles=full):
        _attn_values(p_ref, v_ref, o_ref, row0(g), ntiles)

    bufs = ((s0, p0), (s1, p1))
    ntiles = lambda g: min(g + 1, full)
    for t in range(n_groups + 2):
        if t >= 2:
            values(t - 2, bufs[t % 2][1], ntiles(t - 2))
        if 1 <= t <= n_groups:
            softmax(*bufs[(t - 1) % 2], ntiles(t - 1))
        if t < n_groups:
            scores(t, bufs[t % 2][0], ntiles(t))


def _attention(qkv, diagonals, batch, seq):
    t = batch * seq
    return pl.pallas_call(
        _attn_kernel,
        grid=(N_PAIRS, batch),
        in_specs=[
            pl.BlockSpec((1, seq, HEAD_PAIR), lambda hp, b: (hp, b, 0)),
            pl.BlockSpec((1, seq, HEAD_PAIR), lambda hp, b: (N_PAIRS + hp, b, 0)),
            pl.BlockSpec((1, seq, HEAD_PAIR), lambda hp, b: (2 * N_PAIRS + hp, b, 0)),
            pl.BlockSpec((2, 1, DIAG_LANES), lambda hp, b: (hp, 0, 0)),
        ],
        out_specs=pl.BlockSpec((1, seq, HEAD_PAIR), lambda hp, b: (hp, b, 0)),
        out_shape=jax.ShapeDtypeStruct((N_PAIRS, t, HEAD_PAIR), BF16),
        scratch_shapes=[
            pltpu.VMEM((2, GROUP, BAND), F32),
            pltpu.VMEM((2 * GROUP, BAND), F32),
            pltpu.VMEM((2 * GROUP, BAND), F32),
            pltpu.VMEM((2 * GROUP, BAND), BF16),
            pltpu.VMEM((2 * GROUP, BAND), BF16),
        ],
        compiler_params=pltpu.CompilerParams(
            dimension_semantics=("parallel", "arbitrary"), vmem_limit_bytes=VMEM_LIMIT),
        name="band_attention",
    )(qkv, qkv, qkv, diagonals)


def _band_bias_diagonals(rel_table):
    j = jnp.arange(DIAG_LANES)
    dist = (GROUP - 1 - j) + (BAND - GROUP)
    w = rel_table[:, jnp.clip(dist, -MAX_REL, MAX_REL) + MAX_REL] - rel_table[:, 2 * MAX_REL:]
    return (w * LOG2_E)[:, None, :]


def _gelu_tanh(x):
    k = math.sqrt(2.0 / math.pi)
    inner = x * (k + (k * 0.044715) * (x * x))
    return x * (0.5 + 0.5 * jnp.tanh(inner))


def _rglru_tile(xr, gr_ref, side_work, cw_ref, cb_ref, wr_ref, br_ref, wi_ref, bi_ref, lam_ref,
                o_ref, xs_ref, xc_ref, rp_ref, ip_ref, a_ref, b_ref, hl_ref, pp_ref, h_ref):
    tm = xr.shape[0]
    seg = tm // SUBLANES
    n_lane_tiles = D_MODEL // LANES
    tiles_per_slab = MXU_DIM // LANES
    lane_tile = lambda c: slice(c * LANES, (c + 1) * LANES)

    for c in range(n_lane_tiles):
        xs_ref[c, SUBLANES:, :] = xr[:, lane_tile(c)]

    for d in range(D_MODEL // MXU_DIM):
        for k in range(tiles_per_slab):
            c = d * tiles_per_slab + k
            for j in range(SUBLANES):
                acc = cb_ref[:, lane_tile(c)]
                for tap in range(CONV_W):
                    off = SUBLANES - (CONV_W - 1) + tap + j * seg
                    acc = acc + xs_ref[c, off:off + seg, :] * cw_ref[tap:tap + 1, lane_tile(c)]
                xc_ref[j * seg:(j + 1) * seg, lane_tile(k)] = acc
                side_work(COST_CONV)
            xs_ref[c, 0:SUBLANES, :] = xs_ref[c, tm:tm + SUBLANES, :]
        xcb = xc_ref[...].astype(BF16)
        rp_ref[...] = jnp.dot(xcb, wr_ref[d], preferred_element_type=F32)
        ip_ref[...] = jnp.dot(xcb, wi_ref[d], preferred_element_type=F32)
        for k in range(tiles_per_slab):
            c = d * tiles_per_slab + k
            nlam = -lam_ref[:, lane_tile(c)]
            softplus = jnp.maximum(nlam, 0.0) + jnp.log1p(jnp.exp(-jnp.abs(nlam)))
            log_a_scale = (-RG_C) * softplus
            for j in range(SUBLANES):
                rows = slice(j * seg, (j + 1) * seg)
                xc = xc_ref[rows, lane_tile(k)]
                r = jax.nn.sigmoid(rp_ref[rows, lane_tile(k)] + br_ref[:, lane_tile(c)])
                ig = jax.nn.sigmoid(ip_ref[rows, lane_tile(k)] + bi_ref[:, lane_tile(c)])
                log_a = r * log_a_scale
                a = jnp.exp(log_a)
                y = jnp.tanh(log_a) * (-1.0 - a * a)
                mult = jnp.where(y > 0.0, y * lax.rsqrt(y), 0.0)
                srows = slice(j * SEG_PITCH, j * SEG_PITCH + seg)
                a_ref[c, srows, :] = a
                b_ref[c, srows, :] = mult * (ig * xc)
                side_work(COST_GATE)

    hs = [jnp.zeros((SUBLANES, LANES), F32) for _ in range(n_lane_tiles)]
    ps = [jnp.ones((SUBLANES, LANES), F32) for _ in range(n_lane_tiles)]
    for t in range(seg):
        idx = pl.ds(t, SUBLANES, stride=SEG_PITCH)
        for c in range(n_lane_tiles):
            a_t = a_ref[c, idx, :]
            hs[c] = a_t * hs[c] + b_ref[c, idx, :]
            ps[c] = a_t * ps[c]
            hl_ref[c, idx, :] = hs[c]
            pp_ref[c, idx, :] = ps[c]
        side_work(COST_SCAN_STEP)
    h_end = jnp.concatenate(hs, axis=1)
    p_end = jnp.concatenate(ps, axis=1)

    h_in = h_ref[0:1, :]
    for j in range(SUBLANES):
        rows = slice(j * seg, (j + 1) * seg)
        srows = slice(j * SEG_PITCH, j * SEG_PITCH + seg)
        for c in range(n_lane_tiles):
            hj = hl_ref[c, srows, :] + pp_ref[c, srows, :] * h_in[:, lane_tile(c)]
            o_ref[rows, lane_tile(c)] = (hj * _gelu_tanh(gr_ref[rows, lane_tile(c)])).astype(BF16)
            side_work(COST_OUT)
        h_in = p_end[j:j + 1, :] * h_in + h_end[j:j + 1, :]
    h_ref[0:1, :] = h_in


def _mixffn_kernel(x_hbm, at_ref, hb_ref, ga_ref, gb_ref, bm_ref, wa_ref, wb_ref, wo_ref,
                   gn_ref, win_ref, wout_ref, gf_ref, o_ref, x_ring, x_sem, *, splits):
    step = pl.program_id(0)
    n_steps = pl.num_programs(0)
    tm = o_ref.shape[0]

    def x_copy(s):
        slot = s % X_RING_SLOTS
        return pltpu.make_async_copy(x_hbm.at[pl.ds(pl.multiple_of(s * tm, tm), tm), :],
                                     x_ring.at[slot], x_sem.at[slot])

    @pl.when(step == 0)
    def _():
        for s in range(X_RING_SLOTS - 1):
            x_copy(s).start()

    @pl.when(step + (X_RING_SLOTS - 1) < n_steps)
    def _():
        x_copy(step + (X_RING_SLOTS - 1)).start()

    x_copy(step).wait()
    x_ref = x_ring.at[step % X_RING_SLOTS]
    attn = jnp.concatenate([at_ref[hp] for hp in range(N_PAIRS)], axis=1)
    ya = jnp.dot(attn, wa_ref[...], preferred_element_type=F32)
    yb = jnp.dot(hb_ref[...], wb_ref[...], preferred_element_type=F32)
    g_a = jax.nn.sigmoid(ga_ref[...].astype(F32) + bm_ref[:, :D_MODEL])
    g_b = jax.nn.sigmoid(gb_ref[...].astype(F32) + bm_ref[:, D_MODEL:])
    mix = (g_a * ya + g_b * yb).astype(BF16)
    h = x_ref[...] + jnp.dot(mix, wo_ref[...], preferred_element_type=F32)
    hn = _rms(h, gn_ref[...]).astype(BF16)
    d_ff = wout_ref.shape[0]
    acc = h
    for c0, c1 in splits:
        g = jnp.dot(hn, win_ref[:, c0:c1], preferred_element_type=F32)
        up = jnp.dot(hn, win_ref[:, d_ff + c0:d_ff + c1], preferred_element_type=F32)
        act = (jax.nn.silu(g) * up).astype(BF16)
        acc = acc + jnp.dot(act, wout_ref[c0:c1, :], preferred_element_type=F32)
    o_ref[...] = _rms(acc, gf_ref[...])


def _mixffn(x2, attn, hb, gates, bm, wa, wb, wo, gn, win, wout, gf, tm):
    t = x2.shape[0]
    d_ff = wout.shape[0]
    edges = list(range(0, d_ff, FFN_CHUNK)) + [d_ff]
    splits = tuple(zip(edges[:-1], edges[1:]))
    row = lambda i: (i, 0)
    assert t // tm >= X_RING_SLOTS
    return pl.pallas_call(
        functools.partial(_mixffn_kernel, splits=splits),
        grid=(t // tm,),
        in_specs=[
            pl.BlockSpec(memory_space=pl.ANY),
            pl.BlockSpec((N_PAIRS, tm, HEAD_PAIR), lambda i: (0, i, 0)),
            pl.BlockSpec((tm, D_MODEL), row),
            pl.BlockSpec((tm, D_MODEL), lambda i: (i, 0)),
            pl.BlockSpec((tm, D_MODEL), lambda i: (i, 1)),
            _resident((1, 2 * D_MODEL)),
            _resident((D_MODEL, D_MODEL), single_buffer=True),
            _resident((D_MODEL, D_MODEL), single_buffer=True),
            _resident((D_MODEL, D_MODEL), single_buffer=True),
            _resident((1, D_MODEL)),
            _resident((D_MODEL, 2 * d_ff), single_buffer=True),
            _resident((d_ff, D_MODEL), single_buffer=True),
            _resident((1, D_MODEL)),
        ],
        out_specs=pl.BlockSpec((tm, D_MODEL), row),
        out_shape=jax.ShapeDtypeStruct((t, D_MODEL), F32),
        scratch_shapes=[pltpu.VMEM((X_RING_SLOTS, tm, D_MODEL), F32),
                        pltpu.SemaphoreType.DMA((X_RING_SLOTS,))],
        compiler_params=pltpu.CompilerParams(
            dimension_semantics=("arbitrary",), vmem_limit_bytes=VMEM_LIMIT),
        name="mix_ffn",
    )(x2, attn, hb, gates, gates, bm, wa, wb, wo, gn, win, wout, gf)


def kernel(x, norm_mix_g, w_in, b_merge, rel_table, w_attn_out, conv_w, conv_b, w_rg_r, b_rg_r,
           w_rg_i, b_rg_i, rg_lambda, w_rnn_out, w_o, norm_ffn_g, w_ffn_in, w_ffn_out, final_norm_g):
    batch, seq, d = x.shape
    assert d == D_MODEL and seq % GROUP == 0 and seq // GROUP >= BAND_TILES
    assert w_in.shape[0] == 1, "single-layer block"
    x2 = x.reshape(batch * seq, d)
    row = lambda v: v.reshape(1, -1)

    assert seq % RG_TILE == 0
    later_weights = (w_attn_out[0], w_rnn_out[0], w_o[0], w_ffn_in[0], w_ffn_out[0])
    qkv, gates, hb, (wa, wb, wo, wfi, wfo) = _inproj(
        x2, row(norm_mix_g[0]), w_in[0], conv_w[0], row(conv_b[0]),
        w_rg_r[0], row(b_rg_r[0]), w_rg_i[0], row(b_rg_i[0]), row(rg_lambda[0]),
        later_weights, seq)
    attn = _attention(qkv, _band_bias_diagonals(rel_table[0]), batch, seq)
    out = _mixffn(x2, attn, hb, gates, row(b_merge[0]), wa, wb, wo, row(norm_ffn_g[0]), wfi, wfo,
                  row(final_norm_g), tm=512)
    return out.reshape(batch, seq, d)
```
